```python
import math, functools
import jax, jax.numpy as jnp
from jax import lax
import numpy as np

D_MODEL = 1024
BATCH = 8
SEQ = 2048
DEPTH = 4
DEC_BATCH = 32
DEC_SEQ = 8
PAST_LEN = 8192
PAGE_SIZE = 128

N_MIXERS = 2
N_A = (DEPTH + 1) // 2
N_B = DEPTH // 2
GDN_HEADS = 8
GDN_DK = 128
GDN_DV = 128
CONV_W = 4
GDN_CHUNK = 64
GDN_QK = GDN_HEADS * GDN_DK
GDN_VW = GDN_HEADS * GDN_DV
CONV_DIM = 2 * GDN_QK + GDN_VW
GDN_IN = CONV_DIM + GDN_VW + 2 * GDN_HEADS
DA_HEADS = 8
DA_DH = 64
DA_VD = 2 * DA_DH
DA_HW = DA_HEADS * 2 * DA_DH
DA_IN = 3 * DA_HW
Q_BLOCK = 128
REL_BUCKETS = 32
REL_MAX_DIST = 128
MEM_LEN = 256
X_HEADS = 4
X_DH = D_MODEL // X_HEADS
D_FF = -(-8 * D_MODEL // (3 * 256)) * 256
RMS_EPS = 1e-6
NEG_INF = -1e30

kernel_name = 'hybrid_gdn_diffattn_decode_step'


def rmsnorm(x, g):
    xf = x.astype(jnp.float32)
    y = xf * lax.rsqrt(jnp.mean(xf * xf, axis=-1, keepdims=True) + RMS_EPS)
    return (y * g.astype(jnp.float32)).astype(x.dtype)


def l2norm(x):
    xf = x.astype(jnp.float32)
    return xf * lax.rsqrt(jnp.sum(xf * xf, axis=-1, keepdims=True) + RMS_EPS)


def causal_conv(x, buf, w):
    L = x.shape[1]
    xp = jnp.concatenate([buf.astype(x.dtype), x], axis=1)
    y = xp[:, 0:L] * w[0]
    for j in range(1, CONV_W):
        y = y + xp[:, j:j + L] * w[j]
    return jax.nn.silu(y), xp[:, L:]


def gated_delta_chunked(q, k, v, beta, g, s0):
    f32 = jnp.float32
    q, k, v, beta, g = (a.astype(f32) for a in (q, k, v, beta, g))
    bsz, L, H, _ = q.shape
    DV = v.shape[-1]
    C = min(GDN_CHUNK, L)
    n = -(-L // C)
    pad = n * C - L
    if pad:
        padw = lambda a: jnp.pad(a, [(0, 0), (0, pad)] + [(0, 0)] * (a.ndim - 2))
        q, k, v, beta, g = (padw(a) for a in (q, k, v, beta, g))

    def blocks(a):
        return jnp.moveaxis(a.reshape((bsz, n, C) + a.shape[2:]), 3, 1)

    q, k, v, beta, g = (blocks(a) for a in (q, k, v, beta, g))
    gc = jnp.cumsum(g, axis=-1)
    incl = jnp.tril(jnp.ones((C, C), bool))
    strict = jnp.tril(jnp.ones((C, C), bool), -1)
    decay = jnp.where(incl, jnp.exp(jnp.where(incl, gc[..., :, None] - gc[..., None, :], 0.0)), 0.0)
    kk = jnp.einsum('bhntd,bhnsd->bhnts', k, k)
    lmat = jnp.where(strict, beta[..., :, None] * decay * kk, 0.0) + jnp.eye(C, dtype=f32)
    solve = functools.partial(lax.linalg.triangular_solve, left_side=True, lower=True, unit_diagonal=True)
    w_v = solve(lmat, beta[..., None] * v)
    w_k = solve(lmat, (beta * jnp.exp(gc))[..., None] * k)
    a_qk = jnp.einsum('bhntd,bhnsd->bhnts', q, k) * decay
    q_g = q * jnp.exp(gc)[..., None]
    k_d = k * jnp.exp(gc[..., -1:] - gc)[..., None]
    g_last = jnp.exp(gc[..., -1])

    def step(s, xs):
        w_v_c, w_k_c, a_c, q_c, k_c, gl_c = xs
        u = w_v_c - jnp.einsum('bhtk,bhkv->bhtv', w_k_c, s)
        o = jnp.einsum('bhtk,bhkv->bhtv', q_c, s) + jnp.einsum('bhts,bhsv->bhtv', a_c, u)
        s = gl_c[..., None, None] * s + jnp.einsum('bhtk,bhtv->bhkv', k_c, u)
        return s, o

    xs = tuple(jnp.moveaxis(a, 2, 0) for a in (w_v, w_k, a_qk, q_g, k_d, g_last))
    s_fin, o = lax.scan(step, s0.astype(f32), xs)
    o = jnp.transpose(o, (1, 0, 3, 2, 4)).reshape(bsz, n * C, H, DV)[:, :L]
    return o, s_fin


def gdn_mixer(h, conv_buf, s0, w_in, conv_w, a_log, dt_bias, o_gain, w_out):
    bsz, L, _ = h.shape
    proj = h @ w_in
    qkv = proj[..., :CONV_DIM]
    z = proj[..., CONV_DIM:CONV_DIM + GDN_VW].reshape(bsz, L, GDN_HEADS, GDN_DV)
    b_raw = proj[..., CONV_DIM + GDN_VW:CONV_DIM + GDN_VW + GDN_HEADS]
    a_raw = proj[..., CONV_DIM + GDN_VW + GDN_HEADS:]
    qkv, new_buf = causal_conv(qkv, conv_buf, conv_w)
    q = l2norm(qkv[..., :GDN_QK].reshape(bsz, L, GDN_HEADS, GDN_DK)) * (GDN_DK ** -0.5)
    k = l2norm(qkv[..., GDN_QK:2 * GDN_QK].reshape(bsz, L, GDN_HEADS, GDN_DK))
    v = qkv[..., 2 * GDN_QK:].reshape(bsz, L, GDN_HEADS, GDN_DV)
    beta = jax.nn.sigmoid(b_raw.astype(jnp.float32))
    g = -jnp.exp(a_log.astype(jnp.float32)) * jax.nn.softplus(a_raw.astype(jnp.float32) + dt_bias.astype(jnp.float32))
    o, s_new = gated_delta_chunked(q, k, v, beta, g, s0)
    o = rmsnorm(o, o_gain) * jax.nn.silu(z.astype(jnp.float32))
    y = o.reshape(bsz, L, GDN_VW).astype(h.dtype) @ w_out
    return y, new_buf, s_new


def rel_bucket(dist):
    max_exact = REL_BUCKETS // 2
    n = jnp.maximum(dist, 0)
    large = max_exact + (jnp.log(jnp.maximum(n, 1).astype(jnp.float32) / max_exact)
                         / math.log(REL_MAX_DIST / max_exact) * (REL_BUCKETS - max_exact)).astype(jnp.int32)
    large = jnp.minimum(large, REL_BUCKETS - 1)
    return jnp.where(n < max_exact, n, large)


def diff_attention_core(q, k, v, q_pos, k_pos, rel_bias, lam):
    bsz, Lq = q.shape[:2]
    QB = min(Q_BLOCK, Lq)
    nb = -(-Lq // QB)
    pad = nb * QB - Lq
    if pad:
        q = jnp.pad(q, [(0, 0), (0, pad), (0, 0), (0, 0), (0, 0)])
        q_pos = jnp.pad(q_pos, (0, pad), mode='edge')
    qb = jnp.moveaxis(q.reshape((bsz, nb, QB) + q.shape[2:]), 1, 0)
    pb = q_pos.reshape(nb, QB)
    scale = DA_DH ** -0.5

    def block(args):
        qi, pi = args
        s = jnp.einsum('bqhcd,bkhcd->bhcqk', qi, k).astype(jnp.float32) * scale
        dist = pi[:, None] - k_pos[None, :]
        bias = jnp.transpose(rel_bias[rel_bucket(dist)], (2, 0, 1)).astype(jnp.float32)
        s = jnp.where((dist >= 0)[None, None, None], s + bias[None, :, None], NEG_INF)
        p = jax.nn.softmax(s, axis=-1)
        a = p[:, :, 0] - lam * p[:, :, 1]
        return jnp.einsum('bhqk,bkhe->bqhe', a.astype(v.dtype), v)

    out = lax.map(block, (qb, pb))
    return jnp.moveaxis(out, 0, 1).reshape(bsz, nb * QB, DA_HEADS, DA_VD)[:, :Lq]


def diff_mixer(h, past_k, past_v, q_pos, k_pos, lam_init, w_in, lq1, lk1, lq2, lk2, sub_gain, w_out, rel_bias):
    bsz, L, _ = h.shape
    proj = h @ w_in
    q = proj[..., :DA_HW].reshape(bsz, L, DA_HEADS, 2, DA_DH)
    k = proj[..., DA_HW:2 * DA_HW].reshape(bsz, L, DA_HEADS, 2, DA_DH)
    v = proj[..., 2 * DA_HW:].reshape(bsz, L, DA_HEADS, DA_VD)
    if past_k is None:
        keys, vals = k, v
    else:
        keys = jnp.concatenate([past_k.astype(k.dtype), k], axis=1)
        vals = jnp.concatenate([past_v.astype(v.dtype), v], axis=1)
    f = lambda a: a.astype(jnp.float32)
    lam = jnp.exp(jnp.sum(f(lq1) * f(lk1))) - jnp.exp(jnp.sum(f(lq2) * f(lk2))) + lam_init
    o = diff_attention_core(q, keys, vals, q_pos, k_pos, rel_bias, lam)
    o = rmsnorm(o, sub_gain) * (1.0 - lam_init)
    y = o.reshape(bsz, L, DA_HEADS * DA_VD).astype(h.dtype) @ w_out
    return y, k, v


def memory_kv(mem, g, w_kv):
    bsz, m, _ = mem.shape
    kv = rmsnorm(mem, g) @ w_kv
    return (kv[..., :D_MODEL].reshape(bsz, m, X_HEADS, X_DH),
            kv[..., D_MODEL:].reshape(bsz, m, X_HEADS, X_DH))


def cross_attn(h, mk, mv, w_q, w_o):
    bsz, L, _ = h.shape
    q = (h @ w_q).reshape(bsz, L, X_HEADS, X_DH)
    s = jnp.einsum('bqhd,bkhd->bhqk', q, mk.astype(q.dtype)).astype(jnp.float32) * (X_DH ** -0.5)
    p = jax.nn.softmax(s, axis=-1)
    o = jnp.einsum('bhqk,bkhd->bqhd', p.astype(h.dtype), mv.astype(h.dtype))
    return o.reshape(bsz, L, D_MODEL) @ w_o


def swiglu(h, w_gu, w_down):
    gu = h @ w_gu
    return (jax.nn.silu(gu[..., :D_FF]) * gu[..., D_FF:]) @ w_down


def setup_inputs(seed: int = 0) -> dict:
    key = jax.random.key(seed)
    ks = iter(jax.random.split(key, 40))
    nk = lambda: next(ks)
    f32 = jnp.float32
    nrm = lambda shape, s=1.0: s * jax.random.normal(nk(), shape, f32)
    w = lambda shape, fan_in: jax.random.normal(nk(), shape, f32) * fan_in ** -0.5
    gain = lambda shape: 1.0 + 0.05 * jax.random.normal(nk(), shape, f32)
    n_pages = PAST_LEN // PAGE_SIZE
    n_pool = (DEC_BATCH * n_pages * 5) // 4
    page_table = jax.random.permutation(nk(), n_pool)[:DEC_BATCH * n_pages].reshape(DEC_BATCH, n_pages).astype(jnp.int32)
    dt = jnp.exp(jax.random.uniform(nk(), (N_A, GDN_HEADS), f32, math.log(1e-3), math.log(1e-1)))
    return {
        'x_prompt': nrm((BATCH, SEQ, D_MODEL)),
        'x_sample': nrm((DEC_BATCH, DEC_SEQ, D_MODEL)),
        'state_gdn': nrm((N_A, DEC_BATCH, GDN_HEADS, GDN_DK, GDN_DV), 0.3),
        'state_conv': nrm((N_A, DEC_BATCH, CONV_W - 1, CONV_DIM)),
        'cache_k': nrm((N_B, n_pool, PAGE_SIZE, DA_HEADS, 2, DA_DH)),
        'cache_v': nrm((N_B, n_pool, PAGE_SIZE, DA_HEADS, DA_VD)),
        'cache_mem_k': nrm((DEPTH, DEC_BATCH, MEM_LEN, X_HEADS, X_DH)),
        'cache_mem_v': nrm((DEPTH, DEC_BATCH, MEM_LEN, X_HEADS, X_DH)),
        'page_table': page_table,
        'mem_prompt': nrm((BATCH, MEM_LEN, D_MODEL)),
        'rel_bias': nrm((REL_BUCKETS, DA_HEADS), 0.5),
        'norm_pre': gain((DEPTH, 3, D_MODEL)),
        'norm_post': gain((DEPTH, 3, D_MODEL)),
        'gdn_w_in': w((N_A, D_MODEL, GDN_IN), D_MODEL),
        'gdn_conv_w': w((N_A, CONV_W, CONV_DIM), CONV_W),
        'gdn_a_log': jnp.log(jax.random.uniform(nk(), (N_A, GDN_HEADS), f32, 1.0, 16.0)),
        'gdn_dt_bias': dt + jnp.log(-jnp.expm1(-dt)),
        'gdn_o_gain': gain((N_A, GDN_DV)),
        'gdn_w_out': w((N_A, GDN_VW, D_MODEL), GDN_VW),
        'da_w_in': w((N_B, D_MODEL, DA_IN), D_MODEL),
        'da_lq1': nrm((N_B, DA_DH), 0.1),
        'da_lk1': nrm((N_B, DA_DH), 0.1),
        'da_lq2': nrm((N_B, DA_DH), 0.1),
        'da_lk2': nrm((N_B, DA_DH), 0.1),
        'da_sub_gain': gain((N_B, DA_VD)),
        'da_w_out': w((N_B, DA_HEADS * DA_VD, D_MODEL), DA_HEADS * DA_VD),
        'mem_gain': gain((DEPTH, D_MODEL)),
        'w_xq': w((DEPTH, D_MODEL, D_MODEL), D_MODEL),
        'w_xkv': w((DEPTH, D_MODEL, 2 * D_MODEL), D_MODEL),
        'w_xo': w((DEPTH, D_MODEL, D_MODEL), D_MODEL),
        'ffn_w_gu': w((DEPTH, D_MODEL, 2 * D_FF), D_MODEL),
        'ffn_w_down': w((DEPTH, D_FF, D_MODEL), D_FF),
    }


def reference(x_prompt, x_sample, state_gdn, state_conv, cache_k, cache_v, cache_mem_k, cache_mem_v,
              page_table, mem_prompt, rel_bias, norm_pre, norm_post,
              gdn_w_in, gdn_conv_w, gdn_a_log, gdn_dt_bias, gdn_o_gain, gdn_w_out,
              da_w_in, da_lq1, da_lk1, da_lq2, da_lk2, da_sub_gain, da_w_out,
              mem_gain, w_xq, w_xkv, w_xo, ffn_w_gu, ffn_w_down):
    b_p, l_p, _ = x_prompt.shape
    b_s, l_s, _ = x_sample.shape
    past_len = page_table.shape[1] * PAGE_SIZE
    pos_p = jnp.arange(l_p, dtype=jnp.int32)
    pos_s = past_len + jnp.arange(l_s, dtype=jnp.int32)
    kpos_s = jnp.arange(past_len + l_s, dtype=jnp.int32)
    yp, ys = x_prompt, x_sample
    gdn_p, conv_p, gdn_s, conv_s = [], [], [], []
    k_p, v_p, k_s, v_s = [], [], [], []
    mk_p_l, mv_p_l = [], []
    for i in range(DEPTH):
        j = i // N_MIXERS
        hp = rmsnorm(yp, norm_pre[i, 0])
        hs = rmsnorm(ys, norm_pre[i, 0])
        if i % N_MIXERS == 0:
            gw = (gdn_w_in[j], gdn_conv_w[j], gdn_a_log[j], gdn_dt_bias[j], gdn_o_gain[j], gdn_w_out[j])
            buf0 = jnp.zeros((b_p, CONV_W - 1, CONV_DIM), hp.dtype)
            s0 = jnp.zeros((b_p, GDN_HEADS, GDN_DK, GDN_DV), jnp.float32)
            op, cb_p, st_p = gdn_mixer(hp, buf0, s0, *gw)
            os_, cb_s, st_s = gdn_mixer(hs, state_conv[j], state_gdn[j], *gw)
            gdn_p.append(st_p); conv_p.append(cb_p); gdn_s.append(st_s); conv_s.append(cb_s)
        else:
            lam_init = 0.8 - 0.6 * math.exp(-0.3 * i)
            dw = (da_w_in[j], da_lq1[j], da_lk1[j], da_lq2[j], da_lk2[j], da_sub_gain[j], da_w_out[j], rel_bias)
            op, kn_p, vn_p = diff_mixer(hp, None, None, pos_p, pos_p, lam_init, *dw)
            past_k = jnp.take(cache_k[j], page_table, axis=0).reshape(b_s, past_len, DA_HEADS, 2, DA_DH)
            past_v = jnp.take(cache_v[j], page_table, axis=0).reshape(b_s, past_len, DA_HEADS, DA_VD)
            os_, kn_s, vn_s = diff_mixer(hs, past_k, past_v, pos_s, kpos_s, lam_init, *dw)
            k_p.append(kn_p); v_p.append(vn_p); k_s.append(kn_s); v_s.append(vn_s)
        yp = yp + rmsnorm(op, norm_post[i, 0])
        ys = ys + rmsnorm(os_, norm_post[i, 0])
        mk_p, mv_p = memory_kv(mem_prompt, mem_gain[i], w_xkv[i])
        mk_p_l.append(mk_p); mv_p_l.append(mv_p)
        hp = rmsnorm(yp, norm_pre[i, 1])
        hs = rmsnorm(ys, norm_pre[i, 1])
        yp = yp + rmsnorm(cross_attn(hp, mk_p, mv_p, w_xq[i], w_xo[i]), norm_post[i, 1])
        ys = ys + rmsnorm(cross_attn(hs, cache_mem_k[i], cache_mem_v[i], w_xq[i], w_xo[i]), norm_post[i, 1])
        hp = rmsnorm(yp, norm_pre[i, 2])
        hs = rmsnorm(ys, norm_pre[i, 2])
        yp = yp + rmsnorm(swiglu(hp, ffn_w_gu[i], ffn_w_down[i]), norm_post[i, 2])
        ys = ys + rmsnorm(swiglu(hs, ffn_w_gu[i], ffn_w_down[i]), norm_post[i, 2])
    return (yp, ys,
            jnp.stack(gdn_p), jnp.stack(conv_p), jnp.stack(k_p), jnp.stack(v_p),
            jnp.stack(mk_p_l), jnp.stack(mv_p_l),
            jnp.stack(gdn_s), jnp.stack(conv_s), jnp.stack(k_s), jnp.stack(v_s))
```

```python
import functools
import math

import jax
import jax.numpy as jnp
from jax import lax
from jax.experimental import pallas as pl
from jax.experimental.pallas import tpu as pltpu

F32 = jnp.float32
BF16 = jnp.bfloat16

D_MODEL = 1024
N_MIXERS = 2
GDN_HEADS = 8
GDN_DK = 128
GDN_DV = 128
CONV_W = 4
GDN_CHUNK = 64
INV_BLOCK = 8
GDN_QK = GDN_HEADS * GDN_DK
GDN_VW = GDN_HEADS * GDN_DV
CONV_DIM = 2 * GDN_QK + GDN_VW
DA_HEADS = 8
DA_DH = 64
DA_VD = 2 * DA_DH
DA_HW = DA_HEADS * 2 * DA_DH
PAGE_SIZE = 128
REL_BUCKETS = 32
REL_MAX_DIST = 128
X_HEADS = 4
X_DH = D_MODEL // X_HEADS
D_FF = -(-8 * D_MODEL // (3 * 256)) * 256
RMS_EPS = 1e-6
NEG_INF = -1e30

LANES = 128
SUBLANES = 8
ROW_TILE = 512
ATTN_TILE = 256
PAGES_PER_STEP = 4
FFN_TILE = D_FF // 2
VMEM_LIMIT = 56 * 1024 * 1024

_NT = (((1,), (1,)), ((), ()))
_TN = (((0,), (0,)), ((), ()))


def _params(*sem):
    return pltpu.CompilerParams(dimension_semantics=sem, vmem_limit_bytes=VMEM_LIMIT)


def _row_tile(t):
    return ROW_TILE if t % ROW_TILE == 0 else t


def _rms(x, g):
    return x * lax.rsqrt(jnp.mean(x * x, axis=-1, keepdims=True) + RMS_EPS) * g


def _silu(x):
    return x * jax.nn.sigmoid(x)


def _mm(a, b):
    return jnp.dot(a, b, preferred_element_type=F32)


def _mm_nt(a, b):
    return lax.dot_general(a, b, _NT, preferred_element_type=F32)


def _mm_f32(a, b, dims=None):
    if dims is None:
        return jnp.dot(a, b, preferred_element_type=F32, precision=lax.Precision.HIGHEST)
    return lax.dot_general(a, b, dims, preferred_element_type=F32, precision=lax.Precision.HIGHEST)


def _norm_matmul_kernel(x_ref, g_ref, w_ref, o_ref, h_ref):
    @pl.when(pl.program_id(1) == 0)
    def _():
        h_ref[...] = _rms(x_ref[...], g_ref[...]).astype(BF16)

    o_ref[...] = _mm(h_ref[...], w_ref[...])


def _norm_matmul(x, g, w, tn):
    t, d = x.shape
    n = w.shape[1]
    tm = _row_tile(t)
    return pl.pallas_call(
        _norm_matmul_kernel,
        grid=(t // tm, n // tn),
        in_specs=[
            pl.BlockSpec((tm, d), lambda i, j: (i, 0)),
            pl.BlockSpec((1, d), lambda i, j: (0, 0)),
            pl.BlockSpec((d, tn), lambda i, j: (0, j)),
        ],
        out_specs=pl.BlockSpec((None, tm, tn), lambda i, j: (j, i, 0)),
        out_shape=jax.ShapeDtypeStruct((n // tn, t, tn), F32),
        scratch_shapes=[pltpu.VMEM((tm, d), BF16)],
        compiler_params=_params("parallel", "arbitrary"),
        name="norm_matmul",
    )(x, g.reshape(1, d), w)


def _matmul_postnorm_kernel(a_ref, w_ref, g_ref, r_ref, o_ref):
    y = _mm(a_ref[...].astype(BF16), w_ref[...])
    o_ref[...] = r_ref[...] + _rms(y, g_ref[...])


def _matmul_postnorm(a, w, g, res):
    t, k = a.shape
    d = w.shape[1]
    tm = _row_tile(t)
    return pl.pallas_call(
        _matmul_postnorm_kernel,
        grid=(t // tm,),
        in_specs=[
            pl.BlockSpec((tm, k), lambda i: (i, 0)),
            pl.BlockSpec((k, d), lambda i: (0, 0)),
            pl.BlockSpec((1, d), lambda i: (0, 0)),
            pl.BlockSpec((tm, d), lambda i: (i, 0)),
        ],
        out_specs=pl.BlockSpec((tm, d), lambda i: (i, 0)),
        out_shape=jax.ShapeDtypeStruct((t, d), F32),
        compiler_params=_params("parallel"),
        name="matmul_postnorm",
    )(a, w, g.reshape(1, d), res)


def _ffn_kernel(x_ref, gpre_ref, wg_ref, wu_ref, wd_ref, gpost_ref, o_ref, h_ref, acc_ref):
    j = pl.program_id(1)

    @pl.when(j == 0)
    def _():
        h_ref[...] = _rms(x_ref[...], gpre_ref[...]).astype(BF16)
        acc_ref[...] = jnp.zeros_like(acc_ref)

    h = h_ref[...]
    a = (_silu(_mm(h, wg_ref[...])) * _mm(h, wu_ref[...])).astype(BF16)
    acc_ref[...] += _mm(a, wd_ref[...])

    @pl.when(j == pl.num_programs(1) - 1)
    def _():
        o_ref[...] = x_ref[...] + _rms(acc_ref[...], gpost_ref[...])


def _ffn(x, g_pre, w_gu, w_down, g_post):
    t, d = x.shape
    tm = _row_tile(t)
    nf = D_FF // FFN_TILE
    return pl.pallas_call(
        _ffn_kernel,
        grid=(t // tm, nf),
        in_specs=[
            pl.BlockSpec((tm, d), lambda i, j: (i, 0)),
            pl.BlockSpec((1, d), lambda i, j: (0, 0)),
            pl.BlockSpec((d, FFN_TILE), lambda i, j: (0, j)),
            pl.BlockSpec((d, FFN_TILE), lambda i, j: (0, j + nf)),
            pl.BlockSpec((FFN_TILE, d), lambda i, j: (j, 0)),
            pl.BlockSpec((1, d), lambda i, j: (0, 0)),
        ],
        out_specs=pl.BlockSpec((tm, d), lambda i, j: (i, 0)),
        out_shape=jax.ShapeDtypeStruct((t, d), F32),
        scratch_shapes=[pltpu.VMEM((tm, d), BF16), pltpu.VMEM((tm, d), F32)],
        compiler_params=_params("parallel", "arbitrary"),
        name="ffn",
    )(x, g_pre.reshape(1, d), w_gu, w_gu, w_down, g_post.reshape(1, d))


def _cross_attn_kernel(y_ref, gpre_ref, wq_ref, mk_ref, mv_ref, wo_ref, gpost_ref, o_ref):
    y = y_ref[...]
    h = _rms(y, gpre_ref[...]).astype(BF16)
    q = (_mm(h, wq_ref[...]) * (X_DH ** -0.5)).astype(BF16)
    mk = mk_ref[...].astype(BF16)
    mv = mv_ref[...].astype(BF16)
    heads = []
    for hd in range(X_HEADS):
        sl = slice(hd * X_DH, (hd + 1) * X_DH)
        s = _mm_nt(q[:, sl], mk[:, sl])
        p = jnp.exp(s - jnp.max(s, axis=-1, keepdims=True))
        heads.append(_mm(p.astype(BF16), mv[:, sl]) / jnp.sum(p, axis=-1, keepdims=True))
    o = jnp.concatenate(heads, axis=-1).astype(BF16)
    o_ref[...] = y + _rms(_mm(o, wo_ref[...]), gpost_ref[...])


def _cross_attn(y, g_pre, w_q, mem_k, mem_v, layer_k, layer_v, w_o, g_post):
    b, l, d = y.shape
    m = mem_k.shape[2]
    tm = _row_tile(l)
    return pl.pallas_call(
        _cross_attn_kernel,
        grid=(b, l // tm),
        in_specs=[
            pl.BlockSpec((None, tm, d), lambda bi, i: (bi, i, 0)),
            pl.BlockSpec((1, d), lambda bi, i: (0, 0)),
            pl.BlockSpec((d, d), lambda bi, i: (0, 0)),
            pl.BlockSpec((None, None, m, d), lambda bi, i: (layer_k, bi, 0, 0)),
            pl.BlockSpec((None, None, m, d), lambda bi, i: (layer_v, bi, 0, 0)),
            pl.BlockSpec((d, d), lambda bi, i: (0, 0)),
            pl.BlockSpec((1, d), lambda bi, i: (0, 0)),
        ],
        out_specs=pl.BlockSpec((None, tm, d), lambda bi, i: (bi, i, 0)),
        out_shape=jax.ShapeDtypeStruct((b, l, d), F32),
        compiler_params=_params("parallel", "parallel"),
        name="cross_attn",
    )(y, g_pre.reshape(1, d), w_q, mem_k, mem_v, w_o, g_post.reshape(1, d))


def _gdn_kernel(alog_ref, dtb_ref, q_ref, k_ref, v_ref, z_ref, ba_ref, cb_ref, cw_ref, og_ref, s0_ref,
                o_ref, sn_ref,
                xq_ref, xk_ref, xv_ref, wv_ref, wk_ref, qg_ref, kd_ref, a_ref, gl_ref, *, seq, chunk, n_chunks):
    c_ = chunk
    padded = n_chunks * c_
    rows = min(seq, c_)
    h = pl.program_id(1)

    for part, (x_ref, src) in enumerate(((xq_ref, q_ref), (xk_ref, k_ref), (xv_ref, v_ref))):
        x_ref[0:SUBLANES, :] = jnp.zeros((SUBLANES, LANES), F32)
        x_ref[SUBLANES - (CONV_W - 1):SUBLANES, :] = cb_ref[part]
        x_ref[pl.ds(SUBLANES, seq), :] = src[...]
        if padded > seq:
            x_ref[pl.ds(SUBLANES + seq, padded - seq), :] = jnp.zeros((padded - seq, LANES), F32)

    alog = jnp.full((1, 1), alog_ref[h], F32)
    dtb = jnp.full((1, 1), dtb_ref[h], F32)
    r_i = lax.broadcasted_iota(jnp.int32, (c_, c_), 0)
    c_i = lax.broadcasted_iota(jnp.int32, (c_, c_), 1)
    incl = r_i >= c_i
    strict = r_i > c_i
    tril = jnp.where(incl, 1.0, 0.0).astype(F32)
    eye = jnp.where(r_i == c_i, 1.0, 0.0).astype(F32)
    blk = lambda size: (r_i // size) == (c_i // size)
    diag_blocks = blk(INV_BLOCK)
    merges = [jnp.logical_and(blk(2 * size), jnp.logical_not(blk(size)))
              for size in (INV_BLOCK << i for i in range(int(math.log2(c_ // INV_BLOCK))))]
    lane = lax.broadcasted_iota(jnp.int32, (c_, LANES), 1)

    def conv(x_ref, part, r0):
        win = x_ref[pl.ds(r0, c_ + SUBLANES), :]
        w = cw_ref[part]
        base = SUBLANES - (CONV_W - 1)
        y = win[base:base + c_] * w[0:1]
        for j in range(1, CONV_W):
            y = y + win[base + j:base + j + c_] * w[j:j + 1]
        return _silu(y)

    def pad_rows(x):
        if rows == c_:
            return x
        return jnp.concatenate([x, jnp.zeros((c_ - rows, x.shape[1]), x.dtype)], axis=0)

    def prep(c, carry):
        r0 = pl.multiple_of(c * c_, c_)
        qc = conv(xq_ref, 0, r0)
        kc = conv(xk_ref, 1, r0)
        v = conv(xv_ref, 2, r0)
        q = qc * lax.rsqrt(jnp.sum(qc * qc, axis=-1, keepdims=True) + RMS_EPS) * (GDN_DK ** -0.5)
        k = kc * lax.rsqrt(jnp.sum(kc * kc, axis=-1, keepdims=True) + RMS_EPS)
        ba = pad_rows(ba_ref[pl.ds(r0, rows), :])
        b_raw = jnp.sum(jnp.where(lane == h, ba, 0.0), axis=-1, keepdims=True)
        a_raw = jnp.sum(jnp.where(lane == h + GDN_HEADS, ba, 0.0), axis=-1, keepdims=True)
        beta = jax.nn.sigmoid(b_raw)
        x = a_raw + dtb
        g = -jnp.exp(alog) * (jnp.maximum(x, 0.0) + jnp.log1p(jnp.exp(-jnp.abs(x))))
        if padded > seq:
            valid = (lax.broadcasted_iota(jnp.int32, (c_, 1), 0) + r0) < seq
            k = jnp.where(valid, k, 0.0)
            beta = jnp.where(valid, beta, 0.0)
            g = jnp.where(valid, g, 0.0)
        gc = _mm_f32(tril, jnp.broadcast_to(g, (c_, LANES)))
        gc_row = _mm_f32(jnp.full((c_, LANES), 1.0 / LANES, F32), gc, _NT)
        decay = jnp.where(incl, jnp.exp(jnp.where(incl, gc[:, :c_] - gc_row, 0.0)), 0.0)
        kk = _mm_f32(k, k, _NT)
        qk = _mm_f32(q, k, _NT)
        low = jnp.where(strict, beta * decay * kk, 0.0)
        pw = -jnp.where(diag_blocks, low, 0.0)
        t_inv = eye + pw
        for _ in range(int(math.log2(INV_BLOCK)) - 1):
            pw = _mm_f32(pw, pw)
            t_inv = t_inv + _mm_f32(t_inv, pw)
        for merge in merges:
            t_inv = t_inv - _mm_f32(t_inv, _mm_f32(jnp.where(merge, low, 0.0), t_inv))
        egc = jnp.exp(gc)
        gc_last = gc[c_ - 1:c_, :]
        wv_ref[pl.ds(r0, c_), :] = _mm_f32(t_inv, beta * v)
        wk_ref[pl.ds(r0, c_), :] = _mm_f32(t_inv, (beta * egc) * k)
        a_ref[pl.ds(r0, c_), :] = qk * decay
        qg_ref[pl.ds(r0, c_), :] = q * egc
        kd_ref[pl.ds(r0, c_), :] = k * jnp.exp(gc_last - gc)
        gl_ref[pl.ds(c, 1), :] = jnp.exp(gc_last)
        return carry

    lax.fori_loop(0, n_chunks, prep, 0)

    og = og_ref[...]

    def scan(c, s):
        r0 = pl.multiple_of(c * c_, c_)
        u = wv_ref[pl.ds(r0, c_), :] - _mm_f32(wk_ref[pl.ds(r0, c_), :], s)
        o = _mm_f32(qg_ref[pl.ds(r0, c_), :], s) + _mm_f32(a_ref[pl.ds(r0, c_), :], u)
        s = gl_ref[pl.ds(c, 1), :] * s + _mm_f32(kd_ref[pl.ds(r0, c_), :], u, _TN)
        z = z_ref[pl.ds(r0, rows), :]
        o_ref[pl.ds(r0, rows), :] = _rms(o[:rows], og) * _silu(z)
        return s

    sn_ref[...] = lax.fori_loop(0, n_chunks, scan, s0_ref[...])


def _gdn_core(qkvz, ba, conv_buf, conv_w, a_log, dt_bias, o_gain, s0):
    _, b, seq, _ = qkvz.shape
    chunk = GDN_CHUNK
    n_chunks = -(-seq // chunk)
    padded = n_chunks * chunk
    kern = functools.partial(_gdn_kernel, seq=seq, chunk=chunk, n_chunks=n_chunks)
    head_cols = lambda part: pl.BlockSpec((None, None, seq, LANES), lambda bi, hi: (part, bi, 0, hi))
    smem = pl.BlockSpec(memory_space=pltpu.SMEM)
    return pl.pallas_call(
        kern,
        grid=(b, GDN_HEADS),
        in_specs=[
            smem, smem,
            head_cols(0), head_cols(1), head_cols(2), head_cols(3),
            pl.BlockSpec((None, seq, LANES), lambda bi, hi: (bi, 0, 0)),
            pl.BlockSpec((None, 3, CONV_W - 1, LANES), lambda bi, hi: (bi, 0, 0, hi)),
            pl.BlockSpec((3, CONV_W, LANES), lambda bi, hi: (0, 0, hi)),
            pl.BlockSpec((1, GDN_DV), lambda bi, hi: (0, 0)),
            pl.BlockSpec((None, None, GDN_DK, GDN_DV), lambda bi, hi: (bi, hi, 0, 0)),
        ],
        out_specs=[
            pl.BlockSpec((None, seq, LANES), lambda bi, hi: (bi, 0, hi)),
            pl.BlockSpec((None, None, GDN_DK, GDN_DV), lambda bi, hi: (bi, hi, 0, 0)),
        ],
        out_shape=[
            jax.ShapeDtypeStruct((b, seq, GDN_VW), F32),
            jax.ShapeDtypeStruct((b, GDN_HEADS, GDN_DK, GDN_DV), F32),
        ],
        scratch_shapes=[
            pltpu.VMEM((padded + SUBLANES, LANES), F32),
            pltpu.VMEM((padded + SUBLANES, LANES), F32),
            pltpu.VMEM((padded + SUBLANES, LANES), F32),
            pltpu.VMEM((padded, GDN_DV), F32),
            pltpu.VMEM((padded, GDN_DK), F32),
            pltpu.VMEM((padded, GDN_DK), F32),
            pltpu.VMEM((padded, GDN_DK), F32),
            pltpu.VMEM((padded, chunk), F32),
            pltpu.VMEM((n_chunks, LANES), F32),
        ],
        compiler_params=_params("parallel", "parallel"),
        name="gdn_core",
    )(a_log, dt_bias, qkvz, qkvz, qkvz, qkvz, ba, conv_buf, conv_w, o_gain.reshape(1, GDN_DV), s0)


def _rel_bucket(dist):
    max_exact = REL_BUCKETS // 2
    n = jnp.maximum(dist, 0)
    large = max_exact + (jnp.log(jnp.maximum(n, 1).astype(F32) / max_exact)
                         / math.log(REL_MAX_DIST / max_exact) * (REL_BUCKETS - max_exact)).astype(jnp.int32)
    large = jnp.minimum(large, REL_BUCKETS - 1)
    return jnp.where(n < max_exact, n, large)


def _bias_tiles_kernel(rb_ref, o_ref, *, t):
    h = pl.program_id(0)
    d = pl.program_id(1)
    r = lax.broadcasted_iota(jnp.int32, (t, t), 0)
    c = lax.broadcasted_iota(jnp.int32, (t, t), 1)
    dist = d * t + r - c
    bucket = _rel_bucket(dist)
    bias = jnp.zeros((t, t), F32)
    for b in range(REL_BUCKETS):
        bias = jnp.where(bucket == b, rb_ref[b, h], bias)
    o_ref[...] = jnp.where(dist >= 0, bias, NEG_INF)


def _bias_tiles(rel_bias, t):
    assert t >= REL_MAX_DIST
    return pl.pallas_call(
        functools.partial(_bias_tiles_kernel, t=t),
        grid=(DA_HEADS, 3),
        in_specs=[pl.BlockSpec(memory_space=pltpu.SMEM)],
        out_specs=pl.BlockSpec((None, None, t, t), lambda h, d: (h, d, 0, 0)),
        out_shape=jax.ShapeDtypeStruct((DA_HEADS, 3, t, t), F32),
        compiler_params=_params("parallel", "parallel"),
        name="rel_bias_tiles",
    )(rel_bias)


def _lambda(lq1_ref, lk1_ref, lq2_ref, lk2_ref, lam_init):
    dot = lambda a, b: jnp.sum(a[...] * b[...], axis=-1, keepdims=True)
    return jnp.exp(dot(lq1_ref, lk1_ref)) - jnp.exp(dot(lq2_ref, lk2_ref)) + lam_init


def _da_prompt_kernel(q_ref, k_ref, v_ref, b_ref, lq1_ref, lk1_ref, lq2_ref, lk2_ref, sg_ref, o_ref,
                      m_ref, l_ref, acc_ref, *, t, lam_init):
    qi = pl.program_id(2)
    ki = pl.program_id(3)

    @pl.when(ki == 0)
    def _():
        m_ref[...] = jnp.full(m_ref.shape, NEG_INF, F32)
        l_ref[...] = jnp.zeros_like(l_ref)
        acc_ref[...] = jnp.zeros_like(acc_ref)

    @pl.when(ki <= qi)
    def _():
        q = q_ref[...] * (DA_DH ** -0.5)
        lane = lax.broadcasted_iota(jnp.int32, (t, LANES), 1)
        qq = jnp.concatenate([jnp.where(lane < DA_DH, q, 0.0), jnp.where(lane >= DA_DH, q, 0.0)], axis=0)
        s = _mm_nt(qq.astype(BF16), k_ref[...].astype(BF16))
        bias = b_ref[...]
        s = s + jnp.concatenate([bias, bias], axis=0)
        m_prev = m_ref[...]
        m_new = jnp.maximum(m_prev, jnp.max(s, axis=-1, keepdims=True))
        alpha = jnp.exp(m_prev - m_new)
        p = jnp.exp(s - m_new)
        l_ref[...] = alpha * l_ref[...] + jnp.sum(p, axis=-1, keepdims=True)
        acc_ref[...] = alpha * acc_ref[...] + _mm(p.astype(BF16), v_ref[...].astype(BF16))
        m_ref[...] = m_new

    @pl.when(ki == pl.num_programs(3) - 1)
    def _():
        a = acc_ref[...] / l_ref[...]
        lam = _lambda(lq1_ref, lk1_ref, lq2_ref, lk2_ref, lam_init)
        o = a[:t] - lam * a[t:]
        o_ref[...] = _rms(o, sg_ref[...]) * (1.0 - lam_init)


def _da_prompt(qkv, bias_tiles, lq1, lk1, lq2, lk2, sub_gain, lam_init):
    _, b, seq, _ = qkv.shape
    t = min(ATTN_TILE, seq)
    n = seq // t
    vec = lambda x: x.reshape(1, -1)
    vspec = lambda w: pl.BlockSpec((1, w), lambda bi, hi, qi, ki: (0, 0))
    kv = lambda part: pl.BlockSpec((None, None, t, LANES), lambda bi, hi, qi, ki: (part, bi, jnp.minimum(ki, qi), hi))
    return pl.pallas_call(
        functools.partial(_da_prompt_kernel, t=t, lam_init=lam_init),
        grid=(b, DA_HEADS, n, n),
        in_specs=[
            pl.BlockSpec((None, None, t, LANES), lambda bi, hi, qi, ki: (0, bi, qi, hi)),
            kv(1), kv(2),
            pl.BlockSpec((None, None, t, t), lambda bi, hi, qi, ki: (hi, jnp.clip(qi - ki, 0, 2), 0, 0)),
            vspec(DA_DH), vspec(DA_DH), vspec(DA_DH), vspec(DA_DH), vspec(DA_VD),
        ],
        out_specs=pl.BlockSpec((None, t, LANES), lambda bi, hi, qi, ki: (bi, qi, hi)),
        out_shape=jax.ShapeDtypeStruct((b, seq, DA_HEADS * DA_VD), F32),
        scratch_shapes=[pltpu.VMEM((2 * t, 1), F32), pltpu.VMEM((2 * t, 1), F32), pltpu.VMEM((2 * t, DA_VD), F32)],
        compiler_params=_params("parallel", "parallel", "parallel", "arbitrary"),
        name="da_prompt",
    )(qkv, qkv, qkv, bias_tiles, vec(lq1), vec(lk1), vec(lq2), vec(lk2), vec(sub_gain))


def _da_sample_kernel(pt_ref, q_ref, kn_ref, vn_ref, rbt_ref, lq1_ref, lk1_ref, lq2_ref, lk2_ref, sg_ref, *rest,
                      pages, n_pages, lq, lam_init):
    del pt_ref
    k_refs = rest[:pages]
    v_refs = rest[pages:2 * pages]
    o_ref, qbd_ref, m_ref, l_ref, acc_ref, bias_ref = rest[2 * pages:]
    step = pl.program_id(1)
    rows = DA_HEADS * 2 * lq
    past_len = n_pages * PAGE_SIZE

    @pl.when(step == 0)
    def _():
        q = q_ref[...] * (DA_DH ** -0.5)
        qrep = jnp.concatenate([q] * (DA_HEADS * 2), axis=0)
        r_hc = lax.broadcasted_iota(jnp.int32, (rows, DA_HW), 0) // lq
        c_hc = lax.broadcasted_iota(jnp.int32, (rows, DA_HW), 1) // DA_DH
        qbd_ref[...] = jnp.where(r_hc == c_hc, qrep, 0.0).astype(BF16)
        m_ref[...] = jnp.full(m_ref.shape, NEG_INF, F32)
        l_ref[...] = jnp.zeros_like(l_ref)
        acc_ref[...] = jnp.zeros_like(acc_ref)

    def near_bias(k_start, n_valid):
        row = lax.broadcasted_iota(jnp.int32, (rows, PAGE_SIZE), 0)
        col = lax.broadcasted_iota(jnp.int32, (rows, PAGE_SIZE), 1)
        dist = past_len + row % lq - (k_start + col)
        bucket = _rel_bucket(dist)
        bias = jnp.zeros((rows, PAGE_SIZE), F32)
        for b in range(REL_BUCKETS):
            bias = jnp.where(bucket == b, rbt_ref[:, b:b + 1], bias)
        return jnp.where(jnp.logical_and(dist >= 0, col < n_valid), bias, NEG_INF)

    def attend(kb, vb):
        s = _mm_nt(qbd_ref[...], kb) + bias_ref[...]
        m_prev = m_ref[...]
        m_new = jnp.maximum(m_prev, jnp.max(s, axis=-1, keepdims=True))
        alpha = jnp.exp(m_prev - m_new)
        p = jnp.exp(s - m_new)
        l_ref[...] = alpha * l_ref[...] + jnp.sum(p, axis=-1, keepdims=True)
        pb = p.astype(BF16)
        pv = [_mm(pb[h * 2 * lq:(h + 1) * 2 * lq], vb[:, h * DA_VD:(h + 1) * DA_VD]) for h in range(DA_HEADS)]
        acc_ref[...] = alpha * acc_ref[...] + jnp.concatenate(pv, axis=0)
        m_ref[...] = m_new

    for r in range(pages):
        k_start = (step * pages + r) * PAGE_SIZE
        bias_ref[...] = jnp.broadcast_to(rbt_ref[:, REL_BUCKETS - 1:REL_BUCKETS], (rows, PAGE_SIZE))

        @pl.when(past_len - (k_start + PAGE_SIZE - 1) < REL_MAX_DIST)
        def _():
            bias_ref[...] = near_bias(k_start, PAGE_SIZE)

        attend(k_refs[r][...].astype(BF16), v_refs[r][...].astype(BF16))

    @pl.when(step == pl.num_programs(1) - 1)
    def _():
        zeros = jnp.zeros((PAGE_SIZE - lq, DA_HW), F32)
        bias_ref[...] = near_bias(past_len, lq)
        attend(jnp.concatenate([kn_ref[...], zeros], axis=0).astype(BF16),
               jnp.concatenate([vn_ref[...], zeros], axis=0).astype(BF16))
        a = acc_ref[...] / l_ref[...]
        lam = _lambda(lq1_ref, lk1_ref, lq2_ref, lk2_ref, lam_init)
        sg = sg_ref[...]
        heads = []
        for h in range(DA_HEADS):
            o = a[h * 2 * lq:h * 2 * lq + lq] - lam * a[h * 2 * lq + lq:(h + 1) * 2 * lq]
            heads.append(_rms(o, sg) * (1.0 - lam_init))
        o_ref[...] = jnp.concatenate(heads, axis=-1)


def _da_sample(qkv, cache_k, cache_v, layer, page_table, rel_bias, lq1, lk1, lq2, lk2, sub_gain, lam_init):
    _, b, lq, _ = qkv.shape
    n_pages = page_table.shape[1]
    pages = math.gcd(PAGES_PER_STEP, n_pages)
    rows = DA_HEADS * 2 * lq
    assert rows == LANES and lq == SUBLANES
    rbt = jnp.repeat(rel_bias.T, 2 * lq, axis=0)
    vec = lambda x: x.reshape(1, -1)
    vspec = lambda w: pl.BlockSpec((1, w), lambda bi, si, pt: (0, 0))
    new = lambda part: pl.BlockSpec((None, None, lq, DA_HW), lambda bi, si, pt: (part, bi, 0, 0))
    page = lambda r: pl.BlockSpec((None, None, PAGE_SIZE, DA_HW),
                                  lambda bi, si, pt: (layer, pt[bi, si * pages + r], 0, 0))
    grid_spec = pltpu.PrefetchScalarGridSpec(
        num_scalar_prefetch=1,
        grid=(b, n_pages // pages),
        in_specs=[new(0), new(1), new(2),
                  pl.BlockSpec((rows, REL_BUCKETS), lambda bi, si, pt: (0, 0)),
                  vspec(DA_DH), vspec(DA_DH), vspec(DA_DH), vspec(DA_DH), vspec(DA_VD)]
                 + [page(r) for r in range(pages)] + [page(r) for r in range(pages)],
        out_specs=pl.BlockSpec((None, lq, DA_HW), lambda bi, si, pt: (bi, 0, 0)),
        scratch_shapes=[pltpu.VMEM((rows, DA_HW), BF16), pltpu.VMEM((rows, 1), F32), pltpu.VMEM((rows, 1), F32),
                        pltpu.VMEM((rows, DA_VD), F32), pltpu.VMEM((rows, PAGE_SIZE), F32)],
    )
    return pl.pallas_call(
        functools.partial(_da_sample_kernel, pages=pages, n_pages=n_pages, lq=lq, lam_init=lam_init),
        grid_spec=grid_spec,
        out_shape=jax.ShapeDtypeStruct((b, lq, DA_HW), F32),
        compiler_params=_params("parallel", "arbitrary"),
        name="da_sample",
    )(page_table, qkv, qkv, qkv, rbt, vec(lq1), vec(lk1), vec(lq2), vec(lk2), vec(sub_gain),
      *([cache_k] * pages), *([cache_v] * pages))


def _gdn_layer(y, conv_buf, s0, g_pre, w_main, w_ba, conv_w, a_log, dt_bias, o_gain, w_out, g_post):
    b, seq, d = y.shape
    yf = y.reshape(b * seq, d)
    qkvz = _norm_matmul(yf, g_pre, w_main, D_MODEL).reshape(4, b, seq, D_MODEL)
    ba = _norm_matmul(yf, g_pre, w_ba, LANES).reshape(b, seq, LANES)
    cb = jnp.transpose(conv_buf.reshape(b, CONV_W - 1, 3, D_MODEL), (0, 2, 1, 3))
    o, s_new = _gdn_core(qkvz, ba, cb, conv_w, a_log, dt_bias, o_gain, s0)
    tail = jnp.concatenate([qkvz[part][:, -(CONV_W - 1):] for part in range(3)], axis=-1)
    pre = jnp.concatenate([conv_buf, tail], axis=1)
    new_buf = pre[:, -(CONV_W - 1):]
    y = _matmul_postnorm(o.reshape(b * seq, GDN_VW), w_out, g_post, yf).reshape(b, seq, d)
    return y, new_buf, s_new


def kernel(x_prompt, x_sample, state_gdn, state_conv, cache_k, cache_v, cache_mem_k, cache_mem_v, page_table,
           mem_prompt, rel_bias, norm_pre, norm_post, gdn_w_in, gdn_conv_w, gdn_a_log, gdn_dt_bias, gdn_o_gain,
           gdn_w_out, da_w_in, da_lq1, da_lk1, da_lq2, da_lk2, da_sub_gain, da_w_out, mem_gain, w_xq, w_xkv, w_xo,
           ffn_w_gu, ffn_w_down):
    depth = norm_pre.shape[0]
    b_p, l_p, d = x_prompt.shape
    b_s, l_s, _ = x_sample.shape
    m_len = mem_prompt.shape[1]
    yp, ys = x_prompt, x_sample
    mem_flat = mem_prompt.reshape(b_p * m_len, d)
    ck = cache_k.reshape(cache_k.shape[:3] + (DA_HW,))
    cv = cache_v.reshape(cache_v.shape[:3] + (DA_HW,))
    cmk = cache_mem_k.reshape(cache_mem_k.shape[:3] + (d,))
    cmv = cache_mem_v.reshape(cache_mem_v.shape[:3] + (d,))
    bias_tiles = _bias_tiles(rel_bias, min(ATTN_TILE, l_p))
    gdn_p, conv_p, gdn_s, conv_s = [], [], [], []
    k_p, v_p, k_s, v_s = [], [], [], []
    mk_p, mv_p = [], []
    for i in range(depth):
        j = i // N_MIXERS
        g_pre, g_post = norm_pre[i], norm_post[i]
        if i % N_MIXERS == 0:
            w_in = gdn_w_in[j]
            n_main = CONV_DIM + GDN_VW
            w_main = w_in[:, :n_main].astype(BF16)
            w_ba = jnp.pad(w_in[:, n_main:], ((0, 0), (0, LANES - 2 * GDN_HEADS))).astype(BF16)
            conv_w = jnp.transpose(gdn_conv_w[j].reshape(CONV_W, 3, D_MODEL), (1, 0, 2))
            gw = (g_pre[0], w_main, w_ba, conv_w, gdn_a_log[j], gdn_dt_bias[j], gdn_o_gain[j],
                  gdn_w_out[j].astype(BF16), g_post[0])
            buf0 = jnp.zeros((b_p, CONV_W - 1, CONV_DIM), F32)
            s0 = jnp.zeros((b_p, GDN_HEADS, GDN_DK, GDN_DV), F32)
            yp, cb, st = _gdn_layer(yp, buf0, s0, *gw)
            gdn_p.append(st)
            conv_p.append(cb)
            ys, cb, st = _gdn_layer(ys, state_conv[j], state_gdn[j], *gw)
            gdn_s.append(st)
            conv_s.append(cb)
        else:
            lam_init = 0.8 - 0.6 * math.exp(-0.3 * i)
            w_in = da_w_in[j].astype(BF16)
            w_out = da_w_out[j].astype(BF16)
            lw = (da_lq1[j], da_lk1[j], da_lq2[j], da_lk2[j], da_sub_gain[j], lam_init)
            qkv = _norm_matmul(yp.reshape(b_p * l_p, d), g_pre[0], w_in, DA_HW).reshape(3, b_p, l_p, DA_HW)
            o = _da_prompt(qkv, bias_tiles, *lw)
            yp = _matmul_postnorm(o.reshape(b_p * l_p, DA_HW), w_out, g_post[0],
                                  yp.reshape(b_p * l_p, d)).reshape(b_p, l_p, d)
            k_p.append(qkv[1].reshape(b_p, l_p, DA_HEADS, 2, DA_DH))
            v_p.append(qkv[2].reshape(b_p, l_p, DA_HEADS, DA_VD))
            qkv = _norm_matmul(ys.reshape(b_s * l_s, d), g_pre[0], w_in, DA_HW).reshape(3, b_s, l_s, DA_HW)
            o = _da_sample(qkv, ck, cv, j, page_table, rel_bias, *lw)
            ys = _matmul_postnorm(o.reshape(b_s * l_s, DA_HW), w_out, g_post[0],
                                  ys.reshape(b_s * l_s, d)).reshape(b_s, l_s, d)
            k_s.append(qkv[1].reshape(b_s, l_s, DA_HEADS, 2, DA_DH))
            v_s.append(qkv[2].reshape(b_s, l_s, DA_HEADS, DA_VD))
        mkv = _norm_matmul(mem_flat, mem_gain[i], w_xkv[i].astype(BF16), d).reshape(2, b_p, m_len, d)
        mk_p.append(mkv[0].reshape(b_p, m_len, X_HEADS, X_DH))
        mv_p.append(mkv[1].reshape(b_p, m_len, X_HEADS, X_DH))
        w_q, w_o = w_xq[i].astype(BF16), w_xo[i].astype(BF16)
        yp = _cross_attn(yp, g_pre[1], w_q, mkv, mkv, 0, 1, w_o, g_post[1])
        ys = _cross_attn(ys, g_pre[1], w_q, cmk, cmv, i, i, w_o, g_post[1])
        w_gu, w_down = ffn_w_gu[i].astype(BF16), ffn_w_down[i].astype(BF16)
        yp = _ffn(yp.reshape(b_p * l_p, d), g_pre[2], w_gu, w_down, g_post[2]).reshape(b_p, l_p, d)
        ys = _ffn(ys.reshape(b_s * l_s, d), g_pre[2], w_gu, w_down, g_post[2]).reshape(b_s, l_s, d)
    return (yp, ys,
            jnp.stack(gdn_p), jnp.stack(conv_p), jnp.stack(k_p), jnp.stack(v_p),
            jnp.stack(mk_p), jnp.stack(mv_p),
            jnp.stack(gdn_s), jnp.stack(conv_s), jnp.stack(k_s), jnp.stack(v_s))
```

```python
import functools
import math

import jax
import jax.numpy as jnp
from jax import lax
from jax.experimental import pallas as pl
from jax.experimental.pallas import tpu as pltpu

F32 = jnp.float32
BF16 = jnp.bfloat16

D_MODEL = 1024
N_MIXERS = 2
GDN_HEADS = 8
GDN_DK = 128
GDN_DV = 128
CONV_W = 4
GDN_CHUNK = 64
INV_BLOCK = 8
GDN_GROUP = 8
SCAN_UNROLL = 4
GDN_QK = GDN_HEADS * GDN_DK
GDN_VW = GDN_HEADS * GDN_DV
CONV_DIM = 2 * GDN_QK + GDN_VW
DA_HEADS = 8
DA_DH = 64
DA_VD = 2 * DA_DH
DA_HW = DA_HEADS * 2 * DA_DH
PAGE_SIZE = 128
REL_BUCKETS = 32
REL_MAX_DIST = 128
X_HEADS = 4
X_DH = D_MODEL // X_HEADS
D_FF = -(-8 * D_MODEL // (3 * 256)) * 256
RMS_EPS = 1e-6
NEG_INF = -1e30

LANES = 128
SUBLANES = 8
ROW_TILE = 512
ATTN_TILE = 256
PAGES_PER_STEP = 8
FFN_TILE = D_FF // 2
VMEM_LIMIT = 56 * 1024 * 1024

_NT = (((1,), (1,)), ((), ()))


def _params(*sem):
    return pltpu.CompilerParams(dimension_semantics=sem, vmem_limit_bytes=VMEM_LIMIT)


def _row_tile(t):
    return ROW_TILE if t % ROW_TILE == 0 else t


def _rms(x, g):
    return x * lax.rsqrt(jnp.mean(x * x, axis=-1, keepdims=True) + RMS_EPS) * g


def _silu(x):
    return x * jax.nn.sigmoid(x)


def _mm(a, b):
    return jnp.dot(a, b, preferred_element_type=F32)


def _mm_nt(a, b):
    return lax.dot_general(a, b, _NT, preferred_element_type=F32)


def _split(x):
    hi = x.astype(BF16)
    return hi, (x - hi.astype(F32)).astype(BF16)


def _mm3(a, b):
    a_hi, a_lo = _split(a)
    b_hi, b_lo = _split(b)
    return _mm(a_hi, b_hi) + (_mm(a_hi, b_lo) + _mm(a_lo, b_hi))


def _norm_matmul_kernel(x_ref, g_ref, w_ref, o_ref, h_ref):
    @pl.when(pl.program_id(1) == 0)
    def _():
        h_ref[...] = _rms(x_ref[...], g_ref[...]).astype(BF16)

    o_ref[...] = _mm(h_ref[...], w_ref[...])


def _norm_matmul(x, g, w, tn):
    t, d = x.shape
    n = w.shape[1]
    tm = _row_tile(t)
    return pl.pallas_call(
        _norm_matmul_kernel,
        grid=(t // tm, n // tn),
        in_specs=[
            pl.BlockSpec((tm, d), lambda i, j: (i, 0)),
            pl.BlockSpec((1, d), lambda i, j: (0, 0)),
            pl.BlockSpec((d, tn), lambda i, j: (0, j)),
        ],
        out_specs=pl.BlockSpec((None, tm, tn), lambda i, j: (j, i, 0)),
        out_shape=jax.ShapeDtypeStruct((n // tn, t, tn), F32),
        scratch_shapes=[pltpu.VMEM((tm, d), BF16)],
        compiler_params=_params("parallel", "arbitrary"),
        name="norm_matmul",
    )(x, g.reshape(1, d), w)


def _matmul_postnorm_kernel(a_ref, w_ref, g_ref, r_ref, o_ref):
    y = _mm(a_ref[...].astype(BF16), w_ref[...])
    o_ref[...] = r_ref[...] + _rms(y, g_ref[...])


def _matmul_postnorm(a, w, g, res):
    t, k = a.shape
    d = w.shape[1]
    tm = _row_tile(t)
    return pl.pallas_call(
        _matmul_postnorm_kernel,
        grid=(t // tm,),
        in_specs=[
            pl.BlockSpec((tm, k), lambda i: (i, 0)),
            pl.BlockSpec((k, d), lambda i: (0, 0)),
            pl.BlockSpec((1, d), lambda i: (0, 0)),
            pl.BlockSpec((tm, d), lambda i: (i, 0)),
        ],
        out_specs=pl.BlockSpec((tm, d), lambda i: (i, 0)),
        out_shape=jax.ShapeDtypeStruct((t, d), F32),
        compiler_params=_params("parallel"),
        name="matmul_postnorm",
    )(a, w, g.reshape(1, d), res)


def _ffn_kernel(x_ref, gpre_ref, wg_ref, wu_ref, wd_ref, gpost_ref, o_ref, h_ref, acc_ref):
    j = pl.program_id(1)

    @pl.when(j == 0)
    def _():
        h_ref[...] = _rms(x_ref[...], gpre_ref[...]).astype(BF16)
        acc_ref[...] = jnp.zeros_like(acc_ref)

    h = h_ref[...]
    a = (_silu(_mm(h, wg_ref[...])) * _mm(h, wu_ref[...])).astype(BF16)
    acc_ref[...] += _mm(a, wd_ref[...])

    @pl.when(j == pl.num_programs(1) - 1)
    def _():
        o_ref[...] = x_ref[...] + _rms(acc_ref[...], gpost_ref[...])


def _ffn(x, g_pre, w_gu, w_down, g_post):
    t, d = x.shape
    tm = _row_tile(t)
    nf = D_FF // FFN_TILE
    return pl.pallas_call(
        _ffn_kernel,
        grid=(t // tm, nf),
        in_specs=[
            pl.BlockSpec((tm, d), lambda i, j: (i, 0)),
            pl.BlockSpec((1, d), lambda i, j: (0, 0)),
            pl.BlockSpec((d, FFN_TILE), lambda i, j: (0, j)),
            pl.BlockSpec((d, FFN_TILE), lambda i, j: (0, j + nf)),
            pl.BlockSpec((FFN_TILE, d), lambda i, j: (j, 0)),
            pl.BlockSpec((1, d), lambda i, j: (0, 0)),
        ],
        out_specs=pl.BlockSpec((tm, d), lambda i, j: (i, 0)),
        out_shape=jax.ShapeDtypeStruct((t, d), F32),
        scratch_shapes=[pltpu.VMEM((tm, d), BF16), pltpu.VMEM((tm, d), F32)],
        compiler_params=_params("parallel", "arbitrary"),
        name="ffn",
    )(x, g_pre.reshape(1, d), w_gu, w_gu, w_down, g_post.reshape(1, d))


def _cross_attn_kernel(y_ref, gpre_ref, wq_ref, mk_ref, mv_ref, wo_ref, gpost_ref, o_ref):
    y = y_ref[...]
    h = _rms(y, gpre_ref[...]).astype(BF16)
    q = (_mm(h, wq_ref[...]) * (X_DH ** -0.5)).astype(BF16)
    mk = mk_ref[...].astype(BF16)
    mv = mv_ref[...].astype(BF16)
    heads = []
    for hd in range(X_HEADS):
        sl = slice(hd * X_DH, (hd + 1) * X_DH)
        s = _mm_nt(q[:, sl], mk[:, sl])
        p = jnp.exp(s - jnp.max(s, axis=-1, keepdims=True))
        heads.append(_mm(p.astype(BF16), mv[:, sl]) / jnp.sum(p, axis=-1, keepdims=True))
    o = jnp.concatenate(heads, axis=-1).astype(BF16)
    o_ref[...] = y + _rms(_mm(o, wo_ref[...]), gpost_ref[...])


def _cross_attn(y, g_pre, w_q, mem_k, mem_v, layer_k, layer_v, w_o, g_post):
    b, l, d = y.shape
    m = mem_k.shape[2]
    tm = _row_tile(l)
    return pl.pallas_call(
        _cross_attn_kernel,
        grid=(b, l // tm),
        in_specs=[
            pl.BlockSpec((None, tm, d), lambda bi, i: (bi, i, 0)),
            pl.BlockSpec((1, d), lambda bi, i: (0, 0)),
            pl.BlockSpec((d, d), lambda bi, i: (0, 0)),
            pl.BlockSpec((None, None, m, d), lambda bi, i: (layer_k, bi, 0, 0)),
            pl.BlockSpec((None, None, m, d), lambda bi, i: (layer_v, bi, 0, 0)),
            pl.BlockSpec((d, d), lambda bi, i: (0, 0)),
            pl.BlockSpec((1, d), lambda bi, i: (0, 0)),
        ],
        out_specs=pl.BlockSpec((None, tm, d), lambda bi, i: (bi, i, 0)),
        out_shape=jax.ShapeDtypeStruct((b, l, d), F32),
        compiler_params=_params("parallel", "parallel"),
        name="cross_attn",
    )(y, g_pre.reshape(1, d), w_q, mem_k, mem_v, w_o, g_post.reshape(1, d))


def _gdn_kernel(alog_ref, dtb_ref, q_ref, k_ref, v_ref, z_ref, ba_ref, cb_ref, cw_ref, og_ref, s0_ref,
                o_ref, sn_ref,
                xq_ref, xk_ref, xv_ref, sm_ref, sb_ref, oq_ref, ob_ref, gl_ref, *, seq, chunk, n_chunks, group):
    c_ = chunk
    padded = n_chunks * c_
    rows = min(seq, c_)
    h = pl.program_id(1)

    for part, (x_ref, src) in enumerate(((xq_ref, q_ref), (xk_ref, k_ref), (xv_ref, v_ref))):
        x_ref[0:SUBLANES, :] = jnp.zeros((SUBLANES, LANES), F32)
        x_ref[SUBLANES - (CONV_W - 1):SUBLANES, :] = cb_ref[part]
        x_ref[pl.ds(SUBLANES, seq), :] = src[...]
        if padded > seq:
            x_ref[pl.ds(SUBLANES + seq, padded - seq), :] = jnp.zeros((padded - seq, LANES), F32)

    alog = jnp.full((1, 1), alog_ref[h], F32)
    dtb = jnp.full((1, 1), dtb_ref[h], F32)
    r_i = lax.broadcasted_iota(jnp.int32, (c_, c_), 0)
    c_i = lax.broadcasted_iota(jnp.int32, (c_, c_), 1)
    incl = r_i >= c_i
    strict = r_i > c_i
    tril = jnp.where(incl, 1.0, 0.0).astype(BF16)
    eye = jnp.where(r_i == c_i, 1.0, 0.0).astype(F32)
    blk = lambda size: (r_i // size) == (c_i // size)
    diag_blocks = blk(INV_BLOCK)
    merges = [jnp.logical_and(blk(2 * size), jnp.logical_not(blk(size)))
              for size in (INV_BLOCK << i for i in range(int(math.log2(c_ // INV_BLOCK))))]
    lane = lax.broadcasted_iota(jnp.int32, (c_, LANES), 1)

    def conv(x_ref, part, r0):
        win = x_ref[pl.ds(r0, c_ + SUBLANES), :]
        w = cw_ref[part]
        base = SUBLANES - (CONV_W - 1)
        y = win[base:base + c_] * w[0:1]
        for j in range(1, CONV_W):
            y = y + win[base + j:base + j + c_] * w[j:j + 1]
        return _silu(y)

    def pad_rows(x):
        if rows == c_:
            return x
        return jnp.concatenate([x, jnp.zeros((c_ - rows, x.shape[1]), x.dtype)], axis=0)

    def rows_of(c):
        return c * c_ if isinstance(c, int) else pl.multiple_of(c * c_, c_)

    def gates(c, r0):
        qc = conv(xq_ref, 0, r0)
        kc = conv(xk_ref, 1, r0)
        v = conv(xv_ref, 2, r0)
        q = qc * lax.rsqrt(jnp.sum(qc * qc, axis=-1, keepdims=True) + RMS_EPS) * (GDN_DK ** -0.5)
        k = kc * lax.rsqrt(jnp.sum(kc * kc, axis=-1, keepdims=True) + RMS_EPS)
        ba = pad_rows(ba_ref[pl.ds(r0, rows), :])
        b_raw = jnp.sum(jnp.where(lane == h, ba, 0.0), axis=-1, keepdims=True)
        a_raw = jnp.sum(jnp.where(lane == h + GDN_HEADS, ba, 0.0), axis=-1, keepdims=True)
        beta = jax.nn.sigmoid(b_raw)
        x = a_raw + dtb
        g = -jnp.exp(alog) * (jnp.maximum(x, 0.0) + jnp.log1p(jnp.exp(-jnp.abs(x))))
        if padded > seq:
            valid = (lax.broadcasted_iota(jnp.int32, (c_, 1), 0) + r0) < seq
            k = jnp.where(valid, k, 0.0)
            beta = jnp.where(valid, beta, 0.0)
            g = jnp.where(valid, g, 0.0)
        return q, k, v, beta, jnp.broadcast_to(g, (c_, LANES))

    def prep_group(gi, carry):
        cs = [gi * group + i for i in range(group)]
        r0s = [rows_of(c) for c in cs]
        every = lambda f, *xs: [f(*args) for args in zip(*xs)]
        q, k, v, beta, g = zip(*every(gates, cs, r0s))
        g_hi, g_lo = zip(*every(_split, g))
        gc = every(lambda hi, lo: _mm(tril, hi) + _mm(tril, lo), g_hi, g_lo)
        kb = every(lambda x: x.astype(BF16), k)
        kk = every(_mm_nt, kb, kb)
        qk = every(lambda x, y: _mm_nt(x.astype(BF16), y), q, kb)
        decay = every(lambda x: jnp.where(incl, jnp.exp(jnp.where(incl, x[:, :c_] - x.T[:c_], 0.0)), 0.0), gc)
        low = every(lambda b_, d_, kk_: jnp.where(strict, b_ * d_ * kk_, 0.0), beta, decay, kk)
        pw = every(lambda x: -jnp.where(diag_blocks, x, 0.0), low)
        t_inv = every(lambda x: eye + x, pw)
        for _ in range(int(math.log2(INV_BLOCK)) - 1):
            pw = every(_mm3, pw, pw)
            t_inv = every(lambda t_, p_: t_ + _mm3(t_, p_), t_inv, pw)
        for merge in merges:
            off = every(lambda x, t_: _mm3(jnp.where(merge, x, 0.0), t_), low, t_inv)
            t_inv = every(lambda t_, x: t_ - _mm3(t_, x), t_inv, off)
        egc = every(jnp.exp, gc)
        rhs = every(lambda b_, e_, k_, v_: jnp.concatenate([(b_ * e_) * k_, b_ * v_], axis=1), beta, egc, k, v)
        w = every(lambda t_, x: _mm3(t_, x).astype(BF16), t_inv, rhs)
        gc_last = every(lambda x: x[c_ - 1:c_, :], gc)
        kdt = every(lambda k_, x, l_: (k_ * jnp.exp(l_ - x)).T.astype(BF16), k, gc, gc_last)
        a = every(lambda x, d_: (x * d_).astype(BF16), qk, decay)
        kw = every(_mm, kdt, w)
        aw = every(_mm, a, w)
        for i, (c, r0) in enumerate(zip(cs, r0s)):
            sm_ref[c] = kw[i][:, :GDN_DK].astype(BF16)
            sb_ref[c] = kw[i][:, GDN_DK:]
            oq_ref[pl.ds(r0, c_), :] = (q[i] * egc[i] - aw[i][:, :GDN_DK]).astype(BF16)
            ob_ref[pl.ds(r0, c_), :] = aw[i][:, GDN_DK:]
            gl_ref[pl.ds(c, 1), :] = jnp.exp(gc_last[i])
        return carry

    og = og_ref[...]

    def scan(c, s):
        r0 = rows_of(c)
        sb = s.astype(BF16)
        s_next = gl_ref[pl.ds(c, 1), :] * s - _mm(sm_ref[c], sb) + sb_ref[c]
        o = _mm(oq_ref[pl.ds(r0, c_), :], sb) + ob_ref[pl.ds(r0, c_), :]
        z = z_ref[pl.ds(r0, rows), :]
        o_ref[pl.ds(r0, rows), :] = _rms(o[:rows], og) * _silu(z)
        return s_next

    lax.fori_loop(0, n_chunks // group, prep_group, 0)
    sn_ref[...] = lax.fori_loop(0, n_chunks, scan, s0_ref[...], unroll=math.gcd(SCAN_UNROLL, n_chunks))


def _gdn_core(qkvz, ba, conv_buf, conv_w, a_log, dt_bias, o_gain, s0):
    _, b, seq, _ = qkvz.shape
    chunk = GDN_CHUNK
    n_chunks = -(-seq // chunk)
    padded = n_chunks * chunk
    group = math.gcd(GDN_GROUP, n_chunks)
    kern = functools.partial(_gdn_kernel, seq=seq, chunk=chunk, n_chunks=n_chunks, group=group)
    head_cols = lambda part: pl.BlockSpec((None, None, seq, LANES), lambda bi, hi: (part, bi, 0, hi))
    smem = pl.BlockSpec(memory_space=pltpu.SMEM)
    return pl.pallas_call(
        kern,
        grid=(b, GDN_HEADS),
        in_specs=[
            smem, smem,
            head_cols(0), head_cols(1), head_cols(2), head_cols(3),
            pl.BlockSpec((None, seq, LANES), lambda bi, hi: (bi, 0, 0)),
            pl.BlockSpec((None, 3, CONV_W - 1, LANES), lambda bi, hi: (bi, 0, 0, hi)),
            pl.BlockSpec((3, CONV_W, LANES), lambda bi, hi: (0, 0, hi)),
            pl.BlockSpec((1, GDN_DV), lambda bi, hi: (0, 0)),
            pl.BlockSpec((None, None, GDN_DK, GDN_DV), lambda bi, hi: (bi, hi, 0, 0)),
        ],
        out_specs=[
            pl.BlockSpec((None, seq, LANES), lambda bi, hi: (bi, 0, hi)),
            pl.BlockSpec((None, None, GDN_DK, GDN_DV), lambda bi, hi: (bi, hi, 0, 0)),
        ],
        out_shape=[
            jax.ShapeDtypeStruct((b, seq, GDN_VW), F32),
            jax.ShapeDtypeStruct((b, GDN_HEADS, GDN_DK, GDN_DV), F32),
        ],
        scratch_shapes=[
            pltpu.VMEM((padded + SUBLANES, LANES), F32),
            pltpu.VMEM((padded + SUBLANES, LANES), F32),
            pltpu.VMEM((padded + SUBLANES, LANES), F32),
            pltpu.VMEM((n_chunks, GDN_DK, GDN_DK), BF16),
            pltpu.VMEM((n_chunks, GDN_DK, GDN_DV), F32),
            pltpu.VMEM((padded, GDN_DK), BF16),
            pltpu.VMEM((padded, GDN_DV), F32),
            pltpu.VMEM((n_chunks, LANES), F32),
        ],
        compiler_params=_params("parallel", "parallel"),
        name="gdn_core",
    )(a_log, dt_bias, qkvz, qkvz, qkvz, qkvz, ba, conv_buf, conv_w, o_gain.reshape(1, GDN_DV), s0)


def _rel_bucket(dist):
    max_exact = REL_BUCKETS // 2
    n = jnp.maximum(dist, 0)
    large = max_exact + (jnp.log(jnp.maximum(n, 1).astype(F32) / max_exact)
                         / math.log(REL_MAX_DIST / max_exact) * (REL_BUCKETS - max_exact)).astype(jnp.int32)
    large = jnp.minimum(large, REL_BUCKETS - 1)
    return jnp.where(n < max_exact, n, large)


def _bias_tiles_kernel(rb_ref, o_ref, *, t):
    h = pl.program_id(0)
    d = pl.program_id(1)
    r = lax.broadcasted_iota(jnp.int32, (t, t), 0)
    c = lax.broadcasted_iota(jnp.int32, (t, t), 1)
    dist = d * t + r - c
    bucket = _rel_bucket(dist)
    bias = jnp.zeros((t, t), F32)
    for b in range(REL_BUCKETS):
        bias = jnp.where(bucket == b, rb_ref[b, h], bias)
    o_ref[...] = jnp.where(dist >= 0, bias - rb_ref[REL_BUCKETS - 1, h], NEG_INF)


def _bias_tiles(rel_bias, t):
    assert t >= REL_MAX_DIST
    return pl.pallas_call(
        functools.partial(_bias_tiles_kernel, t=t),
        grid=(DA_HEADS, 2),
        in_specs=[pl.BlockSpec(memory_space=pltpu.SMEM)],
        out_specs=pl.BlockSpec((None, None, t, t), lambda h, d: (h, d, 0, 0)),
        out_shape=jax.ShapeDtypeStruct((DA_HEADS, 2, t, t), F32),
        compiler_params=_params("parallel", "parallel"),
        name="rel_bias_tiles",
    )(rel_bias)


def _lambda(lq1_ref, lk1_ref, lq2_ref, lk2_ref, lam_init):
    dot = lambda a, b: jnp.sum(a[...] * b[...], axis=-1, keepdims=True)
    return jnp.exp(dot(lq1_ref, lk1_ref)) - jnp.exp(dot(lq2_ref, lk2_ref)) + lam_init


def _da_prompt_kernel(q_ref, k_ref, v_ref, b_ref, lq1_ref, lk1_ref, lq2_ref, lk2_ref, sg_ref, o_ref,
                      kb_ref, vb_ref, s_ref, *, t, lam_init):
    seq = q_ref.shape[0]
    kb_ref[...] = k_ref[...].astype(BF16)
    vb_ref[:, :DA_VD] = v_ref[...].astype(BF16)
    vb_ref[:, DA_VD:] = jnp.ones((seq, LANES), BF16)
    lam = _lambda(lq1_ref, lk1_ref, lq2_ref, lk2_ref, lam_init)
    sg = sg_ref[...]
    lane = lax.broadcasted_iota(jnp.int32, (t, LANES), 1)
    stack = lambda x: jnp.concatenate([x, x], axis=0)
    for qi in range(seq // t):
        q = q_ref[qi * t:(qi + 1) * t, :] * (DA_DH ** -0.5)
        qq = jnp.concatenate([jnp.where(lane < DA_DH, q, 0.0), jnp.where(lane >= DA_DH, q, 0.0)], axis=0)
        n_k = (qi + 1) * t
        s_ref[:, :n_k] = _mm_nt(qq.astype(BF16), kb_ref[:n_k, :])
        s_ref[:, qi * t:n_k] += stack(b_ref[0])
        if qi >= 1:
            s_ref[:, (qi - 1) * t:qi * t] += stack(b_ref[1])
        s = s_ref[:, :n_k]
        p = jnp.exp(s - jnp.max(s, axis=-1, keepdims=True)).astype(BF16)
        acc = _mm(p, vb_ref[:n_k, :])
        a = acc[:, :DA_VD] / acc[:, DA_VD:]
        o = a[:t] - lam * a[t:]
        o_ref[qi * t:(qi + 1) * t, :] = _rms(o, sg) * (1.0 - lam_init)


def _da_prompt(qkv, bias_tiles, lq1, lk1, lq2, lk2, sub_gain, lam_init):
    _, b, seq, _ = qkv.shape
    t = bias_tiles.shape[-1]
    vec = lambda x: x.reshape(1, -1)
    vspec = lambda w: pl.BlockSpec((1, w), lambda bi, hi: (0, 0))
    head = lambda part: pl.BlockSpec((None, None, seq, LANES), lambda bi, hi: (part, bi, 0, hi))
    return pl.pallas_call(
        functools.partial(_da_prompt_kernel, t=t, lam_init=lam_init),
        grid=(b, DA_HEADS),
        in_specs=[
            head(0), head(1), head(2),
            pl.BlockSpec((None, 2, t, t), lambda bi, hi: (hi, 0, 0, 0)),
            vspec(DA_DH), vspec(DA_DH), vspec(DA_DH), vspec(DA_DH), vspec(DA_VD),
        ],
        out_specs=pl.BlockSpec((None, seq, LANES), lambda bi, hi: (bi, 0, hi)),
        out_shape=jax.ShapeDtypeStruct((b, seq, DA_HEADS * DA_VD), F32),
        scratch_shapes=[pltpu.VMEM((seq, DA_DH * 2), BF16), pltpu.VMEM((seq, DA_VD + LANES), BF16),
                        pltpu.VMEM((2 * t, seq), F32)],
        compiler_params=_params("parallel", "parallel"),
        name="da_prompt",
    )(qkv, qkv, qkv, bias_tiles, vec(lq1), vec(lk1), vec(lq2), vec(lk2), vec(sub_gain))


def _da_sample_kernel(pt_ref, q_ref, kn_ref, vn_ref, rbt_ref, lq1_ref, lk1_ref, lq2_ref, lk2_ref, sg_ref, *rest,
                      pages, n_pages, lq, lam_init):
    del pt_ref
    k_refs = rest[:pages]
    v_refs = rest[pages:2 * pages]
    o_ref, qbd_ref, m_ref, l_ref, acc_ref, bias_ref = rest[2 * pages:]
    step = pl.program_id(1)
    rows = DA_HEADS * 2 * lq
    past_len = n_pages * PAGE_SIZE

    @pl.when(step == 0)
    def _():
        q = q_ref[...] * (DA_DH ** -0.5)
        qrep = jnp.concatenate([q] * (DA_HEADS * 2), axis=0)
        r_hc = lax.broadcasted_iota(jnp.int32, (rows, DA_HW), 0) // lq
        c_hc = lax.broadcasted_iota(jnp.int32, (rows, DA_HW), 1) // DA_DH
        qbd_ref[...] = jnp.where(r_hc == c_hc, qrep, 0.0).astype(BF16)
        m_ref[...] = jnp.full(m_ref.shape, NEG_INF, F32)
        l_ref[...] = jnp.zeros_like(l_ref)
        acc_ref[...] = jnp.zeros_like(acc_ref)

    def near_bias(k_start, n_valid):
        row = lax.broadcasted_iota(jnp.int32, (rows, PAGE_SIZE), 0)
        col = lax.broadcasted_iota(jnp.int32, (rows, PAGE_SIZE), 1)
        dist = past_len + row % lq - (k_start + col)
        bucket = _rel_bucket(dist)
        bias = jnp.zeros((rows, PAGE_SIZE), F32)
        for b in range(REL_BUCKETS):
            bias = jnp.where(bucket == b, rbt_ref[:, b:b + 1], bias)
        return jnp.where(jnp.logical_and(dist >= 0, col < n_valid), bias, NEG_INF)

    def attend(s, v_heads):
        s = s + bias_ref[...]
        m_prev = m_ref[...]
        m_new = jnp.maximum(m_prev, jnp.max(s, axis=-1, keepdims=True))
        alpha = jnp.exp(m_prev - m_new)
        p = jnp.exp(s - m_new)
        l_ref[...] = alpha * l_ref[...] + jnp.sum(p, axis=-1, keepdims=True)
        pb = p.astype(BF16)
        pv = [_mm(pb[h * 2 * lq:(h + 1) * 2 * lq], v_heads[h]) for h in range(DA_HEADS)]
        acc_ref[...] = alpha * acc_ref[...] + jnp.concatenate(pv, axis=0)
        m_ref[...] = m_new

    for r in range(pages):
        k_start = (step * pages + r) * PAGE_SIZE
        bias_ref[...] = jnp.broadcast_to(rbt_ref[:, REL_BUCKETS - 1:REL_BUCKETS], (rows, PAGE_SIZE))

        @pl.when(past_len - (k_start + PAGE_SIZE - 1) < REL_MAX_DIST)
        def _():
            bias_ref[...] = near_bias(k_start, PAGE_SIZE)

        s = _mm(qbd_ref[...], k_refs[r][...].astype(BF16))
        attend(s, [v_refs[r][pl.ds(h, PAGE_SIZE, stride=DA_HEADS), :].astype(BF16) for h in range(DA_HEADS)])

    @pl.when(step == pl.num_programs(1) - 1)
    def _():
        zeros = jnp.zeros((PAGE_SIZE - lq, DA_HW), F32)
        bias_ref[...] = near_bias(past_len, lq)
        kn = jnp.concatenate([kn_ref[...], zeros], axis=0).astype(BF16)
        vn = jnp.concatenate([vn_ref[...], zeros], axis=0).astype(BF16)
        attend(_mm_nt(qbd_ref[...], kn), [vn[:, h * DA_VD:(h + 1) * DA_VD] for h in range(DA_HEADS)])
        a = acc_ref[...] / l_ref[...]
        lam = _lambda(lq1_ref, lk1_ref, lq2_ref, lk2_ref, lam_init)
        sg = sg_ref[...]
        heads = []
        for h in range(DA_HEADS):
            o = a[h * 2 * lq:h * 2 * lq + lq] - lam * a[h * 2 * lq + lq:(h + 1) * 2 * lq]
            heads.append(_rms(o, sg) * (1.0 - lam_init))
        o_ref[...] = jnp.concatenate(heads, axis=-1)


def _da_sample(qkv, cache_k, cache_v, layer, page_table, rel_bias, lq1, lk1, lq2, lk2, sub_gain, lam_init):
    _, b, lq, _ = qkv.shape
    n_pages = page_table.shape[1]
    pages = math.gcd(PAGES_PER_STEP, n_pages)
    rows = DA_HEADS * 2 * lq
    assert rows == LANES and lq == SUBLANES
    rbt = jnp.repeat(rel_bias.T, 2 * lq, axis=0)
    vec = lambda x: x.reshape(1, -1)
    vspec = lambda w: pl.BlockSpec((1, w), lambda bi, si, pt: (0, 0))
    new = lambda part: pl.BlockSpec((None, None, lq, DA_HW), lambda bi, si, pt: (part, bi, 0, 0))
    page = lambda r, shape: pl.BlockSpec((None, None) + shape,
                                         lambda bi, si, pt: (layer, pt[bi, si * pages + r], 0, 0))
    k_shape, v_shape = cache_k.shape[2:], cache_v.shape[2:]
    grid_spec = pltpu.PrefetchScalarGridSpec(
        num_scalar_prefetch=1,
        grid=(b, n_pages // pages),
        in_specs=[new(0), new(1), new(2),
                  pl.BlockSpec((rows, REL_BUCKETS), lambda bi, si, pt: (0, 0)),
                  vspec(DA_DH), vspec(DA_DH), vspec(DA_DH), vspec(DA_DH), vspec(DA_VD)]
                 + [page(r, k_shape) for r in range(pages)] + [page(r, v_shape) for r in range(pages)],
        out_specs=pl.BlockSpec((None, lq, DA_HW), lambda bi, si, pt: (bi, 0, 0)),
        scratch_shapes=[pltpu.VMEM((rows, DA_HW), BF16), pltpu.VMEM((rows, LANES), F32),
                        pltpu.VMEM((rows, LANES), F32), pltpu.VMEM((rows, DA_VD), F32),
                        pltpu.VMEM((rows, PAGE_SIZE), F32)],
    )
    return pl.pallas_call(
        functools.partial(_da_sample_kernel, pages=pages, n_pages=n_pages, lq=lq, lam_init=lam_init),
        grid_spec=grid_spec,
        out_shape=jax.ShapeDtypeStruct((b, lq, DA_HW), F32),
        compiler_params=_params("parallel", "arbitrary"),
        name="da_sample",
    )(page_table, qkv, qkv, qkv, rbt, vec(lq1), vec(lk1), vec(lq2), vec(lk2), vec(sub_gain),
      *([cache_k] * pages), *([cache_v] * pages))


def _gdn_layer(y, conv_buf, s0, g_pre, w_main, w_ba, conv_w, a_log, dt_bias, o_gain, w_out, g_post):
    b, seq, d = y.shape
    yf = y.reshape(b * seq, d)
    qkvz = _norm_matmul(yf, g_pre, w_main, D_MODEL).reshape(4, b, seq, D_MODEL)
    ba = _norm_matmul(yf, g_pre, w_ba, LANES).reshape(b, seq, LANES)
    cb = jnp.transpose(conv_buf.reshape(b, CONV_W - 1, 3, D_MODEL), (0, 2, 1, 3))
    o, s_new = _gdn_core(qkvz, ba, cb, conv_w, a_log, dt_bias, o_gain, s0)
    tail = jnp.concatenate([qkvz[part][:, -(CONV_W - 1):] for part in range(3)], axis=-1)
    pre = jnp.concatenate([conv_buf, tail], axis=1)
    new_buf = pre[:, -(CONV_W - 1):]
    y = _matmul_postnorm(o.reshape(b * seq, GDN_VW), w_out, g_post, yf).reshape(b, seq, d)
    return y, new_buf, s_new


def kernel(x_prompt, x_sample, state_gdn, state_conv, cache_k, cache_v, cache_mem_k, cache_mem_v, page_table,
           mem_prompt, rel_bias, norm_pre, norm_post, gdn_w_in, gdn_conv_w, gdn_a_log, gdn_dt_bias, gdn_o_gain,
           gdn_w_out, da_w_in, da_lq1, da_lk1, da_lq2, da_lk2, da_sub_gain, da_w_out, mem_gain, w_xq, w_xkv, w_xo,
           ffn_w_gu, ffn_w_down):
    depth = norm_pre.shape[0]
    b_p, l_p, d = x_prompt.shape
    b_s, l_s, _ = x_sample.shape
    m_len = mem_prompt.shape[1]
    yp, ys = x_prompt, x_sample
    mem_flat = mem_prompt.reshape(b_p * m_len, d)
    ck = jnp.transpose(cache_k, (0, 1, 3, 4, 5, 2)).reshape(cache_k.shape[:2] + (DA_HW, PAGE_SIZE))
    cv = cache_v.reshape(cache_v.shape[:2] + (PAGE_SIZE * DA_HEADS, DA_VD))
    cmk = cache_mem_k.reshape(cache_mem_k.shape[:3] + (d,))
    cmv = cache_mem_v.reshape(cache_mem_v.shape[:3] + (d,))
    bias_tiles = _bias_tiles(rel_bias, min(ATTN_TILE, l_p))
    gdn_p, conv_p, gdn_s, conv_s = [], [], [], []
    k_p, v_p, k_s, v_s = [], [], [], []
    mk_p, mv_p = [], []
    for i in range(depth):
        j = i // N_MIXERS
        g_pre, g_post = norm_pre[i], norm_post[i]
        if i % N_MIXERS == 0:
            w_in = gdn_w_in[j]
            n_main = CONV_DIM + GDN_VW
            w_main = w_in[:, :n_main].astype(BF16)
            w_ba = jnp.pad(w_in[:, n_main:], ((0, 0), (0, LANES - 2 * GDN_HEADS))).astype(BF16)
            conv_w = jnp.transpose(gdn_conv_w[j].reshape(CONV_W, 3, D_MODEL), (1, 0, 2))
            gw = (g_pre[0], w_main, w_ba, conv_w, gdn_a_log[j], gdn_dt_bias[j], gdn_o_gain[j],
                  gdn_w_out[j].astype(BF16), g_post[0])
            buf0 = jnp.zeros((b_p, CONV_W - 1, CONV_DIM), F32)
            s0 = jnp.zeros((b_p, GDN_HEADS, GDN_DK, GDN_DV), F32)
            yp, cb, st = _gdn_layer(yp, buf0, s0, *gw)
            gdn_p.append(st)
            conv_p.append(cb)
            ys, cb, st = _gdn_layer(ys, state_conv[j], state_gdn[j], *gw)
            gdn_s.append(st)
            conv_s.append(cb)
        else:
            lam_init = 0.8 - 0.6 * math.exp(-0.3 * i)
            w_in = da_w_in[j].astype(BF16)
            w_out = da_w_out[j].astype(BF16)
            lw = (da_lq1[j], da_lk1[j], da_lq2[j], da_lk2[j], da_sub_gain[j], lam_init)
            qkv = _norm_matmul(yp.reshape(b_p * l_p, d), g_pre[0], w_in, DA_HW).reshape(3, b_p, l_p, DA_HW)
            o = _da_prompt(qkv, bias_tiles, *lw)
            yp = _matmul_postnorm(o.reshape(b_p * l_p, DA_HW), w_out, g_post[0],
                                  yp.reshape(b_p * l_p, d)).reshape(b_p, l_p, d)
            k_p.append(qkv[1].reshape(b_p, l_p, DA_HEADS, 2, DA_DH))
            v_p.append(qkv[2].reshape(b_p, l_p, DA_HEADS, DA_VD))
            qkv = _norm_matmul(ys.reshape(b_s * l_s, d), g_pre[0], w_in, DA_HW).reshape(3, b_s, l_s, DA_HW)
            o = _da_sample(qkv, ck, cv, j, page_table, rel_bias, *lw)
            ys = _matmul_postnorm(o.reshape(b_s * l_s, DA_HW), w_out, g_post[0],
                                  ys.reshape(b_s * l_s, d)).reshape(b_s, l_s, d)
            k_s.append(qkv[1].reshape(b_s, l_s, DA_HEADS, 2, DA_DH))
            v_s.append(qkv[2].reshape(b_s, l_s, DA_HEADS, DA_VD))
        mkv = _norm_matmul(mem_flat, mem_gain[i], w_xkv[i].astype(BF16), d).reshape(2, b_p, m_len, d)
        mk_p.append(mkv[0].reshape(b_p, m_len, X_HEADS, X_DH))
        mv_p.append(mkv[1].reshape(b_p, m_len, X_HEADS, X_DH))
        w_q, w_o = w_xq[i].astype(BF16), w_xo[i].astype(BF16)
        yp = _cross_attn(yp, g_pre[1], w_q, mkv, mkv, 0, 1, w_o, g_post[1])
        ys = _cross_attn(ys, g_pre[1], w_q, cmk, cmv, i, i, w_o, g_post[1])
        w_gu, w_down = ffn_w_gu[i].astype(BF16), ffn_w_down[i].astype(BF16)
        yp = _ffn(yp.reshape(b_p * l_p, d), g_pre[2], w_gu, w_down, g_post[2]).reshape(b_p, l_p, d)
        ys = _ffn(ys.reshape(b_s * l_s, d), g_pre[2], w_gu, w_down, g_post[2]).reshape(b_s, l_s, d)
    return (yp, ys,
            jnp.stack(gdn_p), jnp.stack(conv_p), jnp.stack(k_p), jnp.stack(v_p),
            jnp.stack(mk_p), jnp.stack(mv_p),
            jnp.stack(gdn_s), jnp.stack(conv_s), jnp.stack(k_s), jnp.stack(v_s))
```

```python
import functools
import math

import jax
import jax.numpy as jnp
from jax import lax
from jax.experimental import pallas as pl
from jax.experimental.pallas import tpu as pltpu

F32 = jnp.float32
BF16 = jnp.bfloat16

D_MODEL = 1024
N_MIXERS = 2
GDN_HEADS = 8
GDN_DK = 128
GDN_DV = 128
CONV_W = 4
GDN_CHUNK = 64
INV_BLOCK = 8
GDN_GROUP = 8
GDN_MIN_HEADS = 2
SCAN_UNROLL = 4
GDN_QK = GDN_HEADS * GDN_DK
GDN_VW = GDN_HEADS * GDN_DV
CONV_DIM = 2 * GDN_QK + GDN_VW
DA_HEADS = 8
DA_DH = 64
DA_VD = 2 * DA_DH
DA_HW = DA_HEADS * 2 * DA_DH
PAGE_SIZE = 128
REL_BUCKETS = 32
REL_MAX_DIST = 128
X_HEADS = 4
X_DH = D_MODEL // X_HEADS
D_FF = -(-8 * D_MODEL // (3 * 256)) * 256
RMS_EPS = 1e-6
NEG_INF = -1e30

LANES = 128
SUBLANES = 8
ROW_TILE = 512
PROJ_ROW_TILE = 1024
ATTN_TILE = 256
PAGES_PER_STEP = 8
FFN_TILE = D_FF // 2
VMEM_LIMIT = 56 * 1024 * 1024

_NT = (((1,), (1,)), ((), ()))


def _params(*sem):
    return pltpu.CompilerParams(dimension_semantics=sem, vmem_limit_bytes=VMEM_LIMIT)


def _row_tile(t, rows=ROW_TILE):
    return rows if t % rows == 0 else t


def _rms(x, g):
    return x * lax.rsqrt(jnp.mean(x * x, axis=-1, keepdims=True) + RMS_EPS) * g


def _silu(x):
    return x * jax.nn.sigmoid(x)


def _mm(a, b):
    return jnp.dot(a, b, preferred_element_type=F32)


def _mm_nt(a, b):
    return lax.dot_general(a, b, _NT, preferred_element_type=F32)


def _split(x):
    hi = x.astype(BF16)
    return hi, (x - hi.astype(F32)).astype(BF16)


def _mm3(a, b):
    a_hi, a_lo = _split(a)
    b_hi, b_lo = _split(b)
    return _mm(a_hi, b_hi) + (_mm(a_hi, b_lo) + _mm(a_lo, b_hi))


def _norm_matmul_kernel(x_ref, g_ref, w_ref, o_ref, h_ref):
    @pl.when(pl.program_id(1) == 0)
    def _():
        h_ref[...] = _rms(x_ref[...], g_ref[...]).astype(BF16)

    o_ref[...] = _mm(h_ref[...], w_ref[...])


def _norm_matmul(x, g, w, tn):
    t, d = x.shape
    n = w.shape[1]
    tm = _row_tile(t, PROJ_ROW_TILE)
    return pl.pallas_call(
        _norm_matmul_kernel,
        grid=(t // tm, n // tn),
        in_specs=[
            pl.BlockSpec((tm, d), lambda i, j: (i, 0)),
            pl.BlockSpec((1, d), lambda i, j: (0, 0)),
            pl.BlockSpec((d, tn), lambda i, j: (0, j)),
        ],
        out_specs=pl.BlockSpec((None, tm, tn), lambda i, j: (j, i, 0)),
        out_shape=jax.ShapeDtypeStruct((n // tn, t, tn), F32),
        scratch_shapes=[pltpu.VMEM((tm, d), BF16)],
        compiler_params=_params("parallel", "arbitrary"),
        name="norm_matmul",
    )(x, g.reshape(1, d), w)


def _matmul_postnorm_kernel(a_ref, w_ref, g_ref, r_ref, o_ref):
    y = _mm(a_ref[...].astype(BF16), w_ref[...])
    o_ref[...] = r_ref[...] + _rms(y, g_ref[...])


def _matmul_postnorm(a, w, g, res):
    t, k = a.shape
    d = w.shape[1]
    tm = _row_tile(t)
    return pl.pallas_call(
        _matmul_postnorm_kernel,
        grid=(t // tm,),
        in_specs=[
            pl.BlockSpec((tm, k), lambda i: (i, 0)),
            pl.BlockSpec((k, d), lambda i: (0, 0)),
            pl.BlockSpec((1, d), lambda i: (0, 0)),
            pl.BlockSpec((tm, d), lambda i: (i, 0)),
        ],
        out_specs=pl.BlockSpec((tm, d), lambda i: (i, 0)),
        out_shape=jax.ShapeDtypeStruct((t, d), F32),
        compiler_params=_params("parallel"),
        name="matmul_postnorm",
    )(a, w, g.reshape(1, d), res)


def _ffn_kernel(x_ref, gpre_ref, wg_ref, wu_ref, wd_ref, gpost_ref, o_ref, h_ref, acc_ref):
    j = pl.program_id(1)

    @pl.when(j == 0)
    def _():
        h_ref[...] = _rms(x_ref[...], gpre_ref[...]).astype(BF16)
        acc_ref[...] = jnp.zeros_like(acc_ref)

    h = h_ref[...]
    a = (_silu(_mm(h, wg_ref[...])) * _mm(h, wu_ref[...])).astype(BF16)
    acc_ref[...] += _mm(a, wd_ref[...])

    @pl.when(j == pl.num_programs(1) - 1)
    def _():
        o_ref[...] = x_ref[...] + _rms(acc_ref[...], gpost_ref[...])


def _ffn(x, g_pre, w_gu, w_down, g_post):
    t, d = x.shape
    tm = _row_tile(t)
    nf = D_FF // FFN_TILE
    return pl.pallas_call(
        _ffn_kernel,
        grid=(t // tm, nf),
        in_specs=[
            pl.BlockSpec((tm, d), lambda i, j: (i, 0)),
            pl.BlockSpec((1, d), lambda i, j: (0, 0)),
            pl.BlockSpec((d, FFN_TILE), lambda i, j: (0, j)),
            pl.BlockSpec((d, FFN_TILE), lambda i, j: (0, j + nf)),
            pl.BlockSpec((FFN_TILE, d), lambda i, j: (j, 0)),
            pl.BlockSpec((1, d), lambda i, j: (0, 0)),
        ],
        out_specs=pl.BlockSpec((tm, d), lambda i, j: (i, 0)),
        out_shape=jax.ShapeDtypeStruct((t, d), F32),
        scratch_shapes=[pltpu.VMEM((tm, d), BF16), pltpu.VMEM((tm, d), F32)],
        compiler_params=_params("parallel", "arbitrary"),
        name="ffn",
    )(x, g_pre.reshape(1, d), w_gu, w_gu, w_down, g_post.reshape(1, d))


def _cross_attn_kernel(y_ref, gpre_ref, wq_ref, mk_ref, mv_ref, wo_ref, gpost_ref, o_ref):
    y = y_ref[...]
    h = _rms(y, gpre_ref[...]).astype(BF16)
    q = (_mm(h, wq_ref[...]) * (X_DH ** -0.5)).astype(BF16)
    mk = mk_ref[...].astype(BF16)
    mv = mv_ref[...].astype(BF16)
    heads = []
    for hd in range(X_HEADS):
        sl = slice(hd * X_DH, (hd + 1) * X_DH)
        s = _mm_nt(q[:, sl], mk[:, sl])
        p = jnp.exp(s - jnp.max(s, axis=-1, keepdims=True))
        heads.append(_mm(p.astype(BF16), mv[:, sl]) / jnp.sum(p, axis=-1, keepdims=True))
    o = jnp.concatenate(heads, axis=-1).astype(BF16)
    o_ref[...] = y + _rms(_mm(o, wo_ref[...]), gpost_ref[...])


def _cross_attn(y, g_pre, w_q, mem_k, mem_v, layer_k, layer_v, w_o, g_post):
    b, l, d = y.shape
    m = mem_k.shape[2]
    tm = _row_tile(l)
    return pl.pallas_call(
        _cross_attn_kernel,
        grid=(b, l // tm),
        in_specs=[
            pl.BlockSpec((None, tm, d), lambda bi, i: (bi, i, 0)),
            pl.BlockSpec((1, d), lambda bi, i: (0, 0)),
            pl.BlockSpec((d, d), lambda bi, i: (0, 0)),
            pl.BlockSpec((None, None, m, d), lambda bi, i: (layer_k, bi, 0, 0)),
            pl.BlockSpec((None, None, m, d), lambda bi, i: (layer_v, bi, 0, 0)),
            pl.BlockSpec((d, d), lambda bi, i: (0, 0)),
            pl.BlockSpec((1, d), lambda bi, i: (0, 0)),
        ],
        out_specs=pl.BlockSpec((None, tm, d), lambda bi, i: (bi, i, 0)),
        out_shape=jax.ShapeDtypeStruct((b, l, d), F32),
        compiler_params=_params("parallel", "parallel"),
        name="cross_attn",
    )(y, g_pre.reshape(1, d), w_q, mem_k, mem_v, w_o, g_post.reshape(1, d))


def _gdn_kernel(alog_ref, dtb_ref, q_ref, k_ref, v_ref, z_ref, ba_ref, cb_ref, cw_ref, og_ref, s0_ref,
                o_ref, sn_ref,
                xq_ref, xk_ref, xv_ref, sm_ref, sb_ref, oq_ref, ob_ref, gl_ref, *, seq, chunk, n_chunks, group, hb):
    c_ = chunk
    padded = n_chunks * c_
    rows = min(seq, c_)
    width = hb * LANES
    heads = [pl.program_id(1) * hb + hh for hh in range(hb)]
    cols = lambda hh: slice(hh * LANES, (hh + 1) * LANES)

    for part, (x_ref, src) in enumerate(((xq_ref, q_ref), (xk_ref, k_ref), (xv_ref, v_ref))):
        x_ref[0:SUBLANES, :] = jnp.zeros((SUBLANES, width), F32)
        x_ref[SUBLANES - (CONV_W - 1):SUBLANES, :] = cb_ref[part]
        x_ref[pl.ds(SUBLANES, seq), :] = src[...]
        if padded > seq:
            x_ref[pl.ds(SUBLANES + seq, padded - seq), :] = jnp.zeros((padded - seq, width), F32)

    alog = [jnp.full((1, 1), alog_ref[h], F32) for h in heads]
    dtb = [jnp.full((1, 1), dtb_ref[h], F32) for h in heads]
    r_i =lax.broadcasted_iota(jnp.int32, (c_, c_), 0)
    c_i = lax.broadcasted_iota(jnp.int32, (c_, c_), 1)
    incl = r_i >= c_i
    strict = r_i > c_i
    tril = jnp.where(incl, 1.0, 0.0).astype(BF16)
    eye = jnp.where(r_i == c_i, 1.0, 0.0).astype(F32)
    blk = lambda size: (r_i // size) == (c_i // size)
    diag_blocks = blk(INV_BLOCK)
    merges = [jnp.logical_and(blk(2 * size), jnp.logical_not(blk(size)))
              for size in (INV_BLOCK << i for i in range(int(math.log2(c_ // INV_BLOCK))))]
    lane = lax.broadcasted_iota(jnp.int32, (c_, LANES), 1)

    def conv(x_ref, part, hh, r0):
        win = x_ref[pl.ds(r0, c_ + SUBLANES), cols(hh)]
        w = cw_ref[part][:, cols(hh)]
        base = SUBLANES - (CONV_W - 1)
        y = win[base:base + c_] * w[0:1]
        for j in range(1, CONV_W):
            y = y + win[base + j:base + j + c_] * w[j:j + 1]
        return _silu(y)

    def pad_rows(x):
        if rows == c_:
            return x
        return jnp.concatenate([x, jnp.zeros((c_ - rows, x.shape[1]), x.dtype)], axis=0)

    def rows_of(c):
        return c * c_ if isinstance(c, int) else pl.multiple_of(c * c_, c_)

    def gates(hh, r0):
        h = heads[hh]
        qc = conv(xq_ref, 0, hh, r0)
        kc = conv(xk_ref, 1, hh, r0)
        v = conv(xv_ref, 2, hh, r0)
        q = qc * lax.rsqrt(jnp.sum(qc * qc, axis=-1, keepdims=True) + RMS_EPS) * (GDN_DK ** -0.5)
        k = kc * lax.rsqrt(jnp.sum(kc * kc, axis=-1, keepdims=True) + RMS_EPS)
        ba = pad_rows(ba_ref[pl.ds(r0, rows), :])
        b_raw = jnp.sum(jnp.where(lane == h, ba, 0.0), axis=-1, keepdims=True)
        a_raw = jnp.sum(jnp.where(lane == h + GDN_HEADS, ba, 0.0), axis=-1, keepdims=True)
        beta = jax.nn.sigmoid(b_raw)
        x = a_raw + dtb[hh]
        g = -jnp.exp(alog[hh]) * (jnp.maximum(x, 0.0) + jnp.log1p(jnp.exp(-jnp.abs(x))))
        if padded > seq:
            valid = (lax.broadcasted_iota(jnp.int32, (c_, 1), 0) + r0) < seq
            k = jnp.where(valid, k, 0.0)
            beta = jnp.where(valid, beta, 0.0)
            g = jnp.where(valid, g, 0.0)
        return q, k, v, beta, jnp.broadcast_to(g, (c_, LANES))

    def prep_group(gi, carry):
        hhs = [hh for hh in range(hb) for _ in range(group)]
        cs = [gi * group + i for _ in range(hb) for i in range(group)]
        r0s = [rows_of(c) for c in cs]
        every = lambda f, *xs: [f(*args) for args in zip(*xs)]
        q, k, v, beta, g = zip(*every(gates, hhs, r0s))
        g_hi, g_lo = zip(*every(_split, g))
        gc = every(lambda hi, lo: _mm(tril, hi) + _mm(tril, lo), g_hi, g_lo)
        kb = every(lambda x: x.astype(BF16), k)
        kk = every(_mm_nt, kb, kb)
        qk = every(lambda x, y: _mm_nt(x.astype(BF16), y), q, kb)
        decay = every(lambda x: jnp.where(incl, jnp.exp(jnp.where(incl, x[:, :c_] - x.T[:c_], 0.0)), 0.0), gc)
        low = every(lambda b_, d_, kk_: jnp.where(strict, b_ * d_ * kk_, 0.0), beta, decay, kk)
        pw = every(lambda x: -jnp.where(diag_blocks, x, 0.0), low)
        t_inv = every(lambda x: eye + x, pw)
        for _ in range(int(math.log2(INV_BLOCK)) - 1):
            pw = every(_mm3, pw, pw)
            t_inv = every(lambda t_, p_: t_ + _mm3(t_, p_), t_inv, pw)
        for merge in merges:
            off = every(lambda x, t_: _mm3(jnp.where(merge, x, 0.0), t_), low, t_inv)
            t_inv = every(lambda t_, x: t_ - _mm3(t_, x), t_inv, off)
        egc = every(jnp.exp, gc)
        rhs = every(lambda b_, e_, k_, v_: jnp.concatenate([(b_ * e_) * k_, b_ * v_], axis=1), beta, egc, k, v)
        w = every(lambda t_, x: _mm3(t_, x).astype(BF16), t_inv, rhs)
        gc_last = every(lambda x: x[c_ - 1:c_, :], gc)
        kdt = every(lambda k_, x, l_: (k_ * jnp.exp(l_ - x)).T.astype(BF16), k, gc, gc_last)
        a = every(lambda x, d_: (x * d_).astype(BF16), qk, decay)
        kw = every(_mm, kdt, w)
        aw = every(_mm, a, w)
        for i, (hh, c, r0) in enumerate(zip(hhs, cs, r0s)):
            sm_ref[hh * n_chunks + c] = kw[i][:, :GDN_DK].astype(BF16)
            sb_ref[hh * n_chunks + c] = kw[i][:, GDN_DK:]
            oq_ref[pl.ds(r0, c_), cols(hh)] = (q[i] * egc[i] - aw[i][:, :GDN_DK]).astype(BF16)
            ob_ref[pl.ds(r0, c_), cols(hh)] = aw[i][:, GDN_DK:]
            gl_ref[pl.ds(hh * n_chunks + c, 1), :] = jnp.exp(gc_last[i])
        return carry

    og = og_ref[...]

    def scan(c, states):
        r0 = rows_of(c)
        at = [hh * n_chunks + c for hh in range(hb)]
        sb = [s.astype(BF16) for s in states]
        upd = [_mm(sm_ref[i], x) for i, x in zip(at, sb)]
        o = [_mm(oq_ref[pl.ds(r0, c_), cols(hh)], sb[hh]) + ob_ref[pl.ds(r0, c_), cols(hh)] for hh in range(hb)]
        for hh in range(hb):
            z = z_ref[pl.ds(r0, rows), cols(hh)]
            o_ref[pl.ds(r0, rows), cols(hh)] = _rms(o[hh][:rows], og) * _silu(z)
        return tuple(gl_ref[pl.ds(i, 1), :] * s - u + sb_ref[i] for i, s, u in zip(at, states, upd))

    lax.fori_loop(0, n_chunks // group, prep_group, 0)
    final = lax.fori_loop(0, n_chunks, scan, tuple(s0_ref[hh] for hh in range(hb)),
                          unroll=math.gcd(SCAN_UNROLL, n_chunks))
    for hh in range(hb):
        sn_ref[hh] = final[hh]


def _gdn_core(qkvz, ba, conv_buf, conv_w, a_log, dt_bias, o_gain, s0):
    _, b, seq, _ = qkvz.shape
    chunk = GDN_CHUNK
    n_chunks = -(-seq // chunk)
    padded = n_chunks * chunk
    group = math.gcd(GDN_GROUP // GDN_MIN_HEADS, n_chunks)
    hb = math.gcd(GDN_GROUP // group, GDN_HEADS)
    width = hb * LANES
    kern = functools.partial(_gdn_kernel, seq=seq, chunk=chunk, n_chunks=n_chunks, group=group, hb=hb)
    head_cols = lambda part: pl.BlockSpec((None, None, seq, width), lambda bi, hi: (part, bi, 0, hi))
    smem = pl.BlockSpec(memory_space=pltpu.SMEM)
    return pl.pallas_call(
        kern,
        grid=(b, GDN_HEADS // hb),
        in_specs=[
            smem, smem,
            head_cols(0), head_cols(1), head_cols(2), head_cols(3),
            pl.BlockSpec((None, seq, LANES), lambda bi, hi: (bi, 0, 0)),
            pl.BlockSpec((None, 3, CONV_W - 1, width), lambda bi, hi: (bi, 0, 0, hi)),
            pl.BlockSpec((3, CONV_W, width), lambda bi, hi: (0, 0, hi)),
            pl.BlockSpec((1, GDN_DV), lambda bi, hi: (0, 0)),
            pl.BlockSpec((None, hb, GDN_DK, GDN_DV), lambda bi, hi: (bi, hi, 0, 0)),
        ],
        out_specs=[
            pl.BlockSpec((None, seq, width), lambda bi, hi: (bi, 0, hi)),
            pl.BlockSpec((None, hb, GDN_DK, GDN_DV), lambda bi, hi: (bi, hi, 0, 0)),
        ],
        out_shape=[
            jax.ShapeDtypeStruct((b, seq, GDN_VW), F32),
            jax.ShapeDtypeStruct((b, GDN_HEADS, GDN_DK, GDN_DV), F32),
        ],
        scratch_shapes=[
            pltpu.VMEM((padded + SUBLANES, width), F32),
            pltpu.VMEM((padded + SUBLANES, width), F32),
            pltpu.VMEM((padded + SUBLANES, width), F32),
            pltpu.VMEM((hb * n_chunks, GDN_DK, GDN_DK), BF16),
            pltpu.VMEM((hb * n_chunks, GDN_DK, GDN_DV), F32),
            pltpu.VMEM((padded, width), BF16),
            pltpu.VMEM((padded, width), F32),
            pltpu.VMEM((hb * n_chunks, LANES), F32),
        ],
        compiler_params=_params("parallel", "parallel"),
        name="gdn_core",
    )(a_log, dt_bias, qkvz, qkvz, qkvz, qkvz, ba, conv_buf, conv_w, o_gain.reshape(1, GDN_DV), s0)


def _rel_bucket(dist):
    max_exact = REL_BUCKETS // 2
    n = jnp.maximum(dist, 0)
    large = max_exact + (jnp.log(jnp.maximum(n, 1).astype(F32) / max_exact)
                         / math.log(REL_MAX_DIST / max_exact) * (REL_BUCKETS - max_exact)).astype(jnp.int32)
    large = jnp.minimum(large, REL_BUCKETS - 1)
    return jnp.where(n < max_exact, n, large)


def _bias_tiles_kernel(rb_ref, o_ref, *, t):
    h = pl.program_id(0)
    d = pl.program_id(1)
    r = lax.broadcasted_iota(jnp.int32, (t, t), 0)
    c = lax.broadcasted_iota(jnp.int32, (t, t), 1)
    dist = d * t + r - c
    bucket = _rel_bucket(dist)
    bias = jnp.zeros((t, t), F32)
    for b in range(REL_BUCKETS):
        bias = jnp.where(bucket == b, rb_ref[b, h], bias)
    o_ref[...] = jnp.where(dist >= 0, bias - rb_ref[REL_BUCKETS - 1, h], NEG_INF)


def _bias_tiles(rel_bias, t):
    assert t >= REL_MAX_DIST
    return pl.pallas_call(
        functools.partial(_bias_tiles_kernel, t=t),
        grid=(DA_HEADS, 2),
        in_specs=[pl.BlockSpec(memory_space=pltpu.SMEM)],
        out_specs=pl.BlockSpec((None, None, t, t), lambda h, d: (h, d, 0, 0)),
        out_shape=jax.ShapeDtypeStruct((DA_HEADS, 2, t, t), F32),
        compiler_params=_params("parallel", "parallel"),
        name="rel_bias_tiles",
    )(rel_bias)


def _lambda(lq1_ref, lk1_ref, lq2_ref, lk2_ref, lam_init):
    dot = lambda a, b: jnp.sum(a[...] * b[...], axis=-1, keepdims=True)
    return jnp.exp(dot(lq1_ref, lk1_ref)) - jnp.exp(dot(lq2_ref, lk2_ref)) + lam_init


def _da_prompt_kernel(q_ref, k_ref, v_ref, b_ref, lq1_ref, lk1_ref, lq2_ref, lk2_ref, sg_ref, o_ref,
                      kb_ref, vb_ref, s_ref, *, t, lam_init):
    seq = q_ref.shape[0]
    kb_ref[...] = k_ref[...].astype(BF16)
    vb_ref[:, :DA_VD] = v_ref[...].astype(BF16)
    vb_ref[:, DA_VD:] = jnp.ones((seq, LANES), BF16)
    lam = _lambda(lq1_ref, lk1_ref, lq2_ref, lk2_ref, lam_init)
    sg = sg_ref[...]
    lane = lax.broadcasted_iota(jnp.int32, (t, LANES), 1)
    stack = lambda x: jnp.concatenate([x, x], axis=0)
    for qi in range(seq // t):
        q = q_ref[qi * t:(qi + 1) * t, :] * (DA_DH ** -0.5)
        qq = jnp.concatenate([jnp.where(lane < DA_DH, q, 0.0), jnp.where(lane >= DA_DH, q, 0.0)], axis=0)
        n_k = (qi + 1) * t
        s_ref[:, :n_k] = _mm_nt(qq.astype(BF16), kb_ref[:n_k, :])
        s_ref[:, qi * t:n_k] += stack(b_ref[0])
        if qi >= 1:
            s_ref[:, (qi - 1) * t:qi * t] += stack(b_ref[1])
        s = s_ref[:, :n_k]
        p = jnp.exp(s - jnp.max(s, axis=-1, keepdims=True)).astype(BF16)
        acc = _mm(p, vb_ref[:n_k, :])
        a = acc[:, :DA_VD] / acc[:, DA_VD:]
        o = a[:t] - lam * a[t:]
        o_ref[qi * t:(qi + 1) * t, :] = _rms(o, sg) * (1.0 - lam_init)


def _da_prompt(qkv, bias_tiles, lq1, lk1, lq2, lk2, sub_gain, lam_init):
    _, b, seq, _ = qkv.shape
    t = bias_tiles.shape[-1]
    vec = lambda x: x.reshape(1, -1)
    vspec = lambda w: pl.BlockSpec((1, w), lambda bi, hi: (0, 0))
    head = lambda part: pl.BlockSpec((None, None, seq, LANES), lambda bi, hi: (part, bi, 0, hi))
    return pl.pallas_call(
        functools.partial(_da_prompt_kernel, t=t, lam_init=lam_init),
        grid=(b, DA_HEADS),
        in_specs=[
            head(0), head(1), head(2),
            pl.BlockSpec((None, 2, t, t), lambda bi, hi: (hi, 0, 0, 0)),
            vspec(DA_DH), vspec(DA_DH), vspec(DA_DH), vspec(DA_DH), vspec(DA_VD),
        ],
        out_specs=pl.BlockSpec((None, seq, LANES), lambda bi, hi: (bi, 0, hi)),
        out_shape=jax.ShapeDtypeStruct((b, seq, DA_HEADS * DA_VD), F32),
        scratch_shapes=[pltpu.VMEM((seq, DA_DH * 2), BF16), pltpu.VMEM((seq, DA_VD + LANES), BF16),
                        pltpu.VMEM((2 * t, seq), F32)],
        compiler_params=_params("parallel", "parallel"),
        name="da_prompt",
    )(qkv, qkv, qkv, bias_tiles, vec(lq1), vec(lk1), vec(lq2), vec(lk2), vec(sub_gain))


def _da_sample_kernel(pt_ref, q_ref, kn_ref, vn_ref, rbt_ref, lq1_ref, lk1_ref, lq2_ref, lk2_ref, sg_ref, *rest,
                      pages, n_pages, lq, lam_init):
    del pt_ref
    k_refs = rest[:pages]
    v_refs = rest[pages:2 * pages]
    o_ref, qbd_ref, m_ref, l_ref, acc_ref, s_ref = rest[2 * pages:]
    step = pl.program_id(1)
    rows = DA_HEADS * 2 * lq
    past_len = n_pages * PAGE_SIZE

    @pl.when(step == 0)
    def _():
        q = q_ref[...] * (DA_DH ** -0.5)
        qrep = jnp.concatenate([q] * (DA_HEADS * 2), axis=0)
        r_hc = lax.broadcasted_iota(jnp.int32, (rows, DA_HW), 0) // lq
        c_hc = lax.broadcasted_iota(jnp.int32, (rows, DA_HW), 1) // DA_DH
        qbd_ref[...] = jnp.where(r_hc == c_hc, qrep, 0.0).astype(BF16)
        m_ref[...] = jnp.full(m_ref.shape, NEG_INF, F32)
        l_ref[...] = jnp.zeros_like(l_ref)
        acc_ref[...] = jnp.zeros_like(acc_ref)

    def near_bias(k_start, n_valid):
        row = lax.broadcasted_iota(jnp.int32, (rows, PAGE_SIZE), 0)
        col = lax.broadcasted_iota(jnp.int32, (rows, PAGE_SIZE), 1)
        dist = past_len + row % lq - (k_start + col)
        bucket = _rel_bucket(dist)
        bias = jnp.zeros((rows, PAGE_SIZE), F32)
        for b in range(REL_BUCKETS):
            bias = jnp.where(bucket == b, rbt_ref[:, b:b + 1], bias)
        bias = bias - rbt_ref[:, REL_BUCKETS - 1:REL_BUCKETS]
        return jnp.where(jnp.logical_and(dist >= 0, col < n_valid), bias, NEG_INF)

    def attend(s_blocks, v_blocks):
        s_max = functools.reduce(jnp.maximum, s_blocks)
        m_prev = m_ref[...]
        m_new = jnp.maximum(m_prev, jnp.max(s_max, axis=-1, keepdims=True))
        alpha = jnp.exp(m_prev - m_new)
        p = [jnp.exp(s - m_new) for s in s_blocks]
        l_ref[...] = alpha * l_ref[...] + jnp.sum(functools.reduce(jnp.add, p), axis=-1, keepdims=True)
        pb = jnp.concatenate([x.astype(BF16) for x in p], axis=1)
        pv = [_mm(pb[h * 2 * lq:(h + 1) * 2 * lq], jnp.concatenate([v[h] for v in v_blocks], axis=0))
              for h in range(DA_HEADS)]
        acc_ref[...] = alpha * acc_ref[...] + jnp.concatenate(pv, axis=0)
        m_ref[...] = m_new

    for r in range(pages):
        s_ref[r] = _mm(qbd_ref[...], k_refs[r][...].astype(BF16))

    last = step == pl.num_programs(1) - 1

    @pl.when(last)
    def _():
        s_ref[pages - 1] += near_bias(past_len - PAGE_SIZE, PAGE_SIZE)

    attend([s_ref[r] for r in range(pages)],
           [[v_refs[r][pl.ds(h, PAGE_SIZE, stride=DA_HEADS), :].astype(BF16) for h in range(DA_HEADS)]
            for r in range(pages)])

    @pl.when(last)
    def _():
        zeros = jnp.zeros((PAGE_SIZE - lq, DA_HW), F32)
        kn = jnp.concatenate([kn_ref[...], zeros], axis=0).astype(BF16)
        vn = jnp.concatenate([vn_ref[...], zeros], axis=0).astype(BF16)
        attend([_mm_nt(qbd_ref[...], kn) + near_bias(past_len, lq)],
               [[vn[:, h * DA_VD:(h + 1) * DA_VD] for h in range(DA_HEADS)]])
        a = acc_ref[...] / l_ref[...]
        lam = _lambda(lq1_ref, lk1_ref, lq2_ref, lk2_ref, lam_init)
        sg = sg_ref[...]
        heads = []
        for h in range(DA_HEADS):
            o = a[h * 2 * lq:h * 2 * lq + lq] - lam * a[h * 2 * lq + lq:(h + 1) * 2 * lq]
            heads.append(_rms(o, sg) * (1.0 - lam_init))
        o_ref[...] = jnp.concatenate(heads, axis=-1)


def _da_sample(qkv, cache_k, cache_v, layer, page_table, rel_bias, lq1, lk1, lq2, lk2, sub_gain, lam_init):
    _, b, lq, _ = qkv.shape
    n_pages = page_table.shape[1]
    pages = math.gcd(PAGES_PER_STEP, n_pages)
    rows = DA_HEADS * 2 * lq
    assert rows == LANES and lq == SUBLANES
    rbt = jnp.repeat(rel_bias.T, 2 * lq, axis=0)
    vec = lambda x: x.reshape(1, -1)
    vspec = lambda w: pl.BlockSpec((1, w), lambda bi, si, pt: (0, 0))
    new = lambda part: pl.BlockSpec((None, None, lq, DA_HW), lambda bi, si, pt: (part, bi, 0, 0))
    page = lambda r, shape: pl.BlockSpec((None, None) + shape,
                                         lambda bi, si, pt: (layer, pt[bi, si * pages + r], 0, 0))
    k_shape, v_shape = cache_k.shape[2:], cache_v.shape[2:]
    grid_spec = pltpu.PrefetchScalarGridSpec(
        num_scalar_prefetch=1,
        grid=(b, n_pages // pages),
        in_specs=[new(0), new(1), new(2),
                  pl.BlockSpec((rows, REL_BUCKETS), lambda bi, si, pt: (0, 0)),
                  vspec(DA_DH), vspec(DA_DH), vspec(DA_DH), vspec(DA_DH), vspec(DA_VD)]
                 + [page(r, k_shape) for r in range(pages)] + [page(r, v_shape) for r in range(pages)],
        out_specs=pl.BlockSpec((None, lq, DA_HW), lambda bi, si, pt: (bi, 0, 0)),
        scratch_shapes=[pltpu.VMEM((rows, DA_HW), BF16), pltpu.VMEM((rows, LANES), F32),
                        pltpu.VMEM((rows, LANES), F32), pltpu.VMEM((rows, DA_VD), F32),
                        pltpu.VMEM((pages, rows, PAGE_SIZE), F32)],
    )
    return pl.pallas_call(
        functools.partial(_da_sample_kernel, pages=pages, n_pages=n_pages, lq=lq, lam_init=lam_init),
        grid_spec=grid_spec,
        out_shape=jax.ShapeDtypeStruct((b, lq, DA_HW), F32),
        compiler_params=_params("parallel", "arbitrary"),
        name="da_sample",
    )(page_table, qkv, qkv, qkv, rbt, vec(lq1), vec(lk1), vec(lq2), vec(lk2), vec(sub_gain),
      *([cache_k] * pages), *([cache_v] * pages))


def _gdn_layer(y, conv_buf, s0, g_pre, w_main, w_ba, conv_w, a_log, dt_bias, o_gain, w_out, g_post):
    b, seq, d = y.shape
    yf = y.reshape(b * seq, d)
    qkvz = _norm_matmul(yf, g_pre, w_main, D_MODEL).reshape(4, b, seq, D_MODEL)
    ba = _norm_matmul(yf, g_pre, w_ba, LANES).reshape(b, seq, LANES)
    cb = jnp.transpose(conv_buf.reshape(b, CONV_W - 1, 3, D_MODEL), (0, 2, 1, 3))
    o, s_new = _gdn_core(qkvz, ba, cb, conv_w, a_log, dt_bias, o_gain, s0)
    tail = jnp.concatenate([qkvz[part][:, -(CONV_W - 1):] for part in range(3)], axis=-1)
    pre = jnp.concatenate([conv_buf, tail], axis=1)
    new_buf = pre[:, -(CONV_W - 1):]
    y = _matmul_postnorm(o.reshape(b * seq, GDN_VW), w_out, g_post, yf).reshape(b, seq, d)
    return y, new_buf, s_new


def kernel(x_prompt, x_sample, state_gdn, state_conv, cache_k, cache_v, cache_mem_k, cache_mem_v, page_table,
           mem_prompt, rel_bias, norm_pre, norm_post, gdn_w_in, gdn_conv_w, gdn_a_log, gdn_dt_bias, gdn_o_gain,
           gdn_w_out, da_w_in, da_lq1, da_lk1, da_lq2, da_lk2, da_sub_gain, da_w_out, mem_gain, w_xq, w_xkv, w_xo,
           ffn_w_gu, ffn_w_down):
    depth = norm_pre.shape[0]
    b_p, l_p, d = x_prompt.shape
    b_s, l_s, _ = x_sample.shape
    m_len = mem_prompt.shape[1]
    yp, ys = x_prompt, x_sample
    mem_flat = mem_prompt.reshape(b_p * m_len, d)
    ck = jnp.transpose(cache_k, (0, 1, 3, 4, 5, 2)).reshape(cache_k.shape[:2] + (DA_HW, PAGE_SIZE))
    cv = cache_v.reshape(cache_v.shape[:2] + (PAGE_SIZE * DA_HEADS, DA_VD))
    cmk = cache_mem_k.reshape(cache_mem_k.shape[:3] + (d,))
    cmv = cache_mem_v.reshape(cache_mem_v.shape[:3] + (d,))
    bias_tiles = _bias_tiles(rel_bias, min(ATTN_TILE, l_p))
    gdn_p, conv_p, gdn_s, conv_s = [], [], [], []
    k_p, v_p, k_s, v_s = [], [], [], []
    mk_p, mv_p = [], []
    for i in range(depth):
        j = i // N_MIXERS
        g_pre, g_post = norm_pre[i], norm_post[i]
        if i % N_MIXERS == 0:
            w_in = gdn_w_in[j]
            n_main = CONV_DIM + GDN_VW
            w_main = w_in[:, :n_main].astype(BF16)
            w_ba = jnp.pad(w_in[:, n_main:], ((0, 0), (0, LANES - 2 * GDN_HEADS))).astype(BF16)
            conv_w = jnp.transpose(gdn_conv_w[j].reshape(CONV_W, 3, D_MODEL), (1, 0, 2))
            gw = (g_pre[0], w_main, w_ba, conv_w, gdn_a_log[j], gdn_dt_bias[j], gdn_o_gain[j],
                  gdn_w_out[j].astype(BF16), g_post[0])
            buf0 = jnp.zeros((b_p, CONV_W - 1, CONV_DIM), F32)
            s0 = jnp.zeros((b_p, GDN_HEADS, GDN_DK, GDN_DV), F32)
            yp, cb, st = _gdn_layer(yp, buf0, s0, *gw)
            gdn_p.append(st)
            conv_p.append(cb)
            ys, cb, st = _gdn_layer(ys, state_conv[j], state_gdn[j], *gw)
            gdn_s.append(st)
            conv_s.append(cb)
        else:
            lam_init = 0.8 - 0.6 * math.exp(-0.3 * i)
            w_in = da_w_in[j].astype(BF16)
            w_out = da_w_out[j].astype(BF16)
            lw = (da_lq1[j], da_lk1[j], da_lq2[j], da_lk2[j], da_sub_gain[j], lam_init)
            qkv = _norm_matmul(yp.reshape(b_p * l_p, d), g_pre[0], w_in, DA_HW).reshape(3, b_p, l_p, DA_HW)
            o = _da_prompt(qkv, bias_tiles, *lw)
            yp = _matmul_postnorm(o.reshape(b_p * l_p, DA_HW), w_out, g_post[0],
                                  yp.reshape(b_p * l_p, d)).reshape(b_p, l_p, d)
            k_p.append(qkv[1].reshape(b_p, l_p, DA_HEADS, 2, DA_DH))
            v_p.append(qkv[2].reshape(b_p, l_p, DA_HEADS, DA_VD))
            qkv = _norm_matmul(ys.reshape(b_s * l_s, d), g_pre[0], w_in, DA_HW).reshape(3, b_s, l_s, DA_HW)
            o = _da_sample(qkv, ck, cv, j, page_table, rel_bias, *lw)
            ys = _matmul_postnorm(o.reshape(b_s * l_s, DA_HW), w_out, g_post[0],
                                  ys.reshape(b_s * l_s, d)).reshape(b_s, l_s, d)
            k_s.append(qkv[1].reshape(b_s, l_s, DA_HEADS, 2, DA_DH))
            v_s.append(qkv[2].reshape(b_s, l_s, DA_HEADS, DA_VD))
        mkv = _norm_matmul(mem_flat, mem_gain[i], w_xkv[i].astype(BF16), d).reshape(2, b_p, m_len, d)
        mk_p.append(mkv[0].reshape(b_p, m_len, X_HEADS, X_DH))
        mv_p.append(mkv[1].reshape(b_p, m_len, X_HEADS, X_DH))
        w_q, w_o = w_xq[i].astype(BF16), w_xo[i].astype(BF16)
        yp = _cross_attn(yp, g_pre[1], w_q, mkv, mkv, 0, 1, w_o, g_post[1])
        ys = _cross_attn(ys, g_pre[1], w_q, cmk, cmv, i, i, w_o, g_post[1])
        w_gu, w_down = ffn_w_gu[i].astype(BF16), ffn_w_down[i].astype(BF16)
        yp = _ffn(yp.reshape(b_p * l_p, d), g_pre[2], w_gu, w_down, g_post[2]).reshape(b_p, l_p, d)
        ys = _ffn(ys.reshape(b_s * l_s, d), g_pre[2], w_gu, w_down, g_post[2]).reshape(b_s, l_s, d)
    return (yp, ys,
            jnp.stack(gdn_p), jnp.stack(conv_p), jnp.stack(k_p), jnp.stack(v_p),
            jnp.stack(mk_p), jnp.stack(mv_p),
            jnp.stack(gdn_s), jnp.stack(conv_s), jnp.stack(k_s), jnp.stack(v_s))
```

```python
import functools
import math

import jax
import jax.numpy as jnp
from jax import lax
from jax.experimental import pallas as pl
from jax.experimental.pallas import tpu as pltpu

F32 = jnp.float32
BF16 = jnp.bfloat16

D_MODEL = 1024
N_MIXERS = 2
GDN_HEADS = 8
GDN_DK = 128
GDN_DV = 128
CONV_W = 4
GDN_CHUNK = 64
INV_BLOCK = 8
GDN_GROUP = 8
GDN_MIN_HEADS = 2
SCAN_UNROLL = 4
GDN_QK = GDN_HEADS * GDN_DK
GDN_VW = GDN_HEADS * GDN_DV
CONV_DIM = 2 * GDN_QK + GDN_VW
DA_HEADS = 8
DA_DH = 64
DA_VD = 2 * DA_DH
DA_HW = DA_HEADS * 2 * DA_DH
PAGE_SIZE = 128
REL_BUCKETS = 32
REL_MAX_DIST = 128
X_HEADS = 4
X_DH = D_MODEL // X_HEADS
D_FF = -(-8 * D_MODEL // (3 * 256)) * 256
RMS_EPS = 1e-6
NEG_INF = -1e30

LANES = 128
SUBLANES = 8
ROW_TILE = 512
PROJ_ROW_TILE = 1024
ATTN_TILE = 256
PAGES_PER_STEP = 8
VMEM_LIMIT = 56 * 1024 * 1024

_NT = (((1,), (1,)), ((), ()))


def _params(*sem):
    return pltpu.CompilerParams(dimension_semantics=sem, vmem_limit_bytes=VMEM_LIMIT)


def _row_tile(t, rows=ROW_TILE):
    return rows if t % rows == 0 else t


def _rms(x, g):
    return x * lax.rsqrt(jnp.mean(x * x, axis=-1, keepdims=True) + RMS_EPS) * g


def _silu(x):
    return x * jax.nn.sigmoid(x)


def _mm(a, b):
    return jnp.dot(a, b, preferred_element_type=F32)


def _mm_nt(a, b):
    return lax.dot_general(a, b, _NT, preferred_element_type=F32)


def _split(x):
    hi = x.astype(BF16)
    return hi, (x - hi.astype(F32)).astype(BF16)


def _mmb(a, b):
    return _mm(a.astype(BF16), b.astype(BF16))


def _norm_matmul_kernel(x_ref, g_ref, w_ref, *rest):
    *narrow, o_ref, h_ref = rest

    @pl.when(pl.program_id(1) == 0)
    def _():
        h_ref[...] = _rms(x_ref[...], g_ref[...]).astype(BF16)
        if narrow:
            w_narrow_ref, o_narrow_ref = narrow
            o_narrow_ref[...] = _mm(h_ref[...], w_narrow_ref[...])

    o_ref[...] = _mm(h_ref[...], w_ref[...])


def _norm_matmul(x, g, w, tn, w_narrow=None):
    t, d = x.shape
    n = w.shape[1]
    tm = _row_tile(t, PROJ_ROW_TILE)
    in_specs = [
        pl.BlockSpec((tm, d), lambda i, j: (i, 0)),
        pl.BlockSpec((1, d), lambda i, j: (0, 0)),
        pl.BlockSpec((d, tn), lambda i, j: (0, j)),
    ]
    out_specs = [pl.BlockSpec((None, tm, tn), lambda i, j: (j, i, 0))]
    out_shape = [jax.ShapeDtypeStruct((n // tn, t, tn), F32)]
    operands = [x, g.reshape(1, d), w]
    if w_narrow is not None:
        in_specs.append(pl.BlockSpec((d, LANES), lambda i, j: (0, 0)))
        out_specs.insert(0, pl.BlockSpec((tm, LANES), lambda i, j: (i, 0)))
        out_shape.insert(0, jax.ShapeDtypeStruct((t, LANES), F32))
        operands.append(w_narrow)
    out = pl.pallas_call(
        _norm_matmul_kernel,
        grid=(t // tm, n // tn),
        in_specs=in_specs,
        out_specs=out_specs,
        out_shape=out_shape,
        scratch_shapes=[pltpu.VMEM((tm, d), BF16)],
        compiler_params=_params("parallel", "arbitrary"),
        name="norm_matmul",
    )(*operands)
    return out[0] if w_narrow is None else (out[1], out[0])


def _matmul_postnorm_kernel(a_ref, w_ref, g_ref, r_ref, o_ref):
    y = _mm(a_ref[...].astype(BF16), w_ref[...])
    o_ref[...] = r_ref[...] + _rms(y, g_ref[...])


def _matmul_postnorm(a, w, g, res):
    t, k = a.shape
    d = w.shape[1]
    tm = _row_tile(t)
    return pl.pallas_call(
        _matmul_postnorm_kernel,
        grid=(t // tm,),
        in_specs=[
            pl.BlockSpec((tm, k), lambda i: (i, 0)),
            pl.BlockSpec((k, d), lambda i: (0, 0)),
            pl.BlockSpec((1, d), lambda i: (0, 0)),
            pl.BlockSpec((tm, d), lambda i: (i, 0)),
        ],
        out_specs=pl.BlockSpec((tm, d), lambda i: (i, 0)),
        out_shape=jax.ShapeDtypeStruct((t, d), F32),
        compiler_params=_params("parallel"),
        name="matmul_postnorm",
    )(a, w, g.reshape(1, d), res)


def _ffn_kernel(x_ref, gpre_ref, wg_ref, wu_ref, wd_ref, gpost_ref, o_ref):
    x = x_ref[...]
    h = _rms(x, gpre_ref[...]).astype(BF16)
    a = (_silu(_mm(h, wg_ref[...])) * _mm(h, wu_ref[...])).astype(BF16)
    o_ref[...] = x + _rms(_mm(a, wd_ref[...]), gpost_ref[...])


def _ffn(x, g_pre, w_gu, w_down, g_post):
    t, d = x.shape
    tm = _row_tile(t)
    resident = lambda shape, col: pl.BlockSpec(shape, lambda i: (0, col), pipeline_mode=pl.Buffered(1))
    return pl.pallas_call(
        _ffn_kernel,
        grid=(t // tm,),
        in_specs=[
            pl.BlockSpec((tm, d), lambda i: (i, 0)),
            pl.BlockSpec((1, d), lambda i: (0, 0)),
            resident((d, D_FF), 0),
            resident((d, D_FF), 1),
            resident((D_FF, d), 0),
            pl.BlockSpec((1, d), lambda i: (0, 0)),
        ],
        out_specs=pl.BlockSpec((tm, d), lambda i: (i, 0)),
        out_shape=jax.ShapeDtypeStruct((t, d), F32),
        compiler_params=_params("parallel"),
        name="ffn",
    )(x, g_pre.reshape(1, d), w_gu, w_gu, w_down, g_post.reshape(1, d))


def _cross_attn_kernel(y_ref, gpre_ref, wq_ref, mk_ref, mv_ref, wo_ref, gpost_ref, o_ref, *, tiled_mem):
    y = y_ref[...]
    h = _rms(y, gpre_ref[...]).astype(BF16)
    q = (_mm(h, wq_ref[...]) * (X_DH ** -0.5)).astype(BF16)
    if tiled_mem:
        tiles = X_DH // LANES
        m = mk_ref.shape[0] // (tiles * X_HEADS)
        gather = lambda ref: jnp.concatenate(
            [ref[pl.ds(lt * X_HEADS + hd, m, stride=tiles * X_HEADS), :]
             for hd in range(X_HEADS) for lt in range(tiles)], axis=1).astype(BF16)
        mk, mv = gather(mk_ref), gather(mv_ref)
    else:
        mk = mk_ref[...].astype(BF16)
        mv = mv_ref[...].astype(BF16)
    heads = []
    for hd in range(X_HEADS):
        sl = slice(hd * X_DH, (hd + 1) * X_DH)
        s = _mm_nt(q[:, sl], mk[:, sl])
        p = jnp.exp(s - jnp.max(s, axis=-1, keepdims=True))
        heads.append(_mm(p.astype(BF16), mv[:, sl]) / jnp.sum(p, axis=-1, keepdims=True))
    o = jnp.concatenate(heads, axis=-1).astype(BF16)
    o_ref[...] = y + _rms(_mm(o, wo_ref[...]), gpost_ref[...])


def _cross_attn(y, g_pre, w_q, mem_k, mem_v, layer_k, layer_v, w_o, g_post):
    b, l, d = y.shape
    mem_block = mem_k.shape[2:]
    tm = _row_tile(l)
    return pl.pallas_call(
        functools.partial(_cross_attn_kernel, tiled_mem=mem_block[1] != d),
        grid=(b, l // tm),
        in_specs=[
            pl.BlockSpec((None, tm, d), lambda bi, i: (bi, i, 0)),
            pl.BlockSpec((1, d), lambda bi, i: (0, 0)),
            pl.BlockSpec((d, d), lambda bi, i: (0, 0)),
            pl.BlockSpec((None, None) + mem_block, lambda bi, i: (layer_k, bi, 0, 0)),
            pl.BlockSpec((None, None) + mem_block, lambda bi, i: (layer_v, bi, 0, 0)),
            pl.BlockSpec((d, d), lambda bi, i: (0, 0)),
            pl.BlockSpec((1, d), lambda bi, i: (0, 0)),
        ],
        out_specs=pl.BlockSpec((None, tm, d), lambda bi, i: (bi, i, 0)),
        out_shape=jax.ShapeDtypeStruct((b, l, d), F32),
        compiler_params=_params("parallel", "parallel"),
        name="cross_attn",
    )(y, g_pre.reshape(1, d), w_q, mem_k, mem_v, w_o, g_post.reshape(1, d))


def _gdn_kernel(alog_ref, dtb_ref, q_ref, k_ref, v_ref, z_ref, ba_ref, cb_ref, cw_ref, og_ref, s0_ref,
                o_ref, sn_ref,
                xq_ref, xk_ref, xv_ref, sm_ref, sb_ref, oq_ref, ob_ref, gl_ref, *, seq, chunk, n_chunks, group, hb):
    c_ = chunk
    padded = n_chunks * c_
    rows = min(seq, c_)
    width = hb * LANES
    heads = [pl.program_id(1) * hb + hh for hh in range(hb)]
    cols = lambda hh: slice(hh * LANES, (hh + 1) * LANES)

    for part, (x_ref, src) in enumerate(((xq_ref, q_ref), (xk_ref, k_ref), (xv_ref, v_ref))):
        x_ref[0:SUBLANES, :] = jnp.zeros((SUBLANES, width), F32)
        x_ref[SUBLANES - (CONV_W - 1):SUBLANES, :] = cb_ref[part]
        x_ref[pl.ds(SUBLANES, seq), :] = src[...]
        if padded > seq:
            x_ref[pl.ds(SUBLANES + seq, padded - seq), :] = jnp.zeros((padded - seq, width), F32)

    alog = [jnp.full((1, 1), alog_ref[h], F32) for h in heads]
    dtb = [jnp.full((1, 1), dtb_ref[h], F32) for h in heads]
    r_i =lax.broadcasted_iota(jnp.int32, (c_, c_), 0)
    c_i = lax.broadcasted_iota(jnp.int32, (c_, c_), 1)
    incl = r_i >= c_i
    strict = r_i > c_i
    tril = jnp.where(incl, 1.0, 0.0).astype(BF16)
    eye = jnp.where(r_i == c_i, 1.0, 0.0).astype(F32)
    blk = lambda size: (r_i // size) == (c_i // size)
    diag_blocks = blk(INV_BLOCK)
    merges = [jnp.logical_and(blk(2 * size), jnp.logical_not(blk(size)))
              for size in (INV_BLOCK << i for i in range(int(math.log2(c_ // INV_BLOCK))))]
    lane = lax.broadcasted_iota(jnp.int32, (c_, LANES), 1)

    def conv(x_ref, part, hh, r0):
        win = x_ref[pl.ds(r0, c_ + SUBLANES), cols(hh)]
        w = cw_ref[part][:, cols(hh)]
        base = SUBLANES - (CONV_W - 1)
        y = win[base:base + c_] * w[0:1]
        for j in range(1, CONV_W):
            y = y + win[base + j:base + j + c_] * w[j:j + 1]
        return _silu(y)

    def pad_rows(x):
        if rows == c_:
            return x
        return jnp.concatenate([x, jnp.zeros((c_ - rows, x.shape[1]), x.dtype)], axis=0)

    def rows_of(c):
        return c * c_ if isinstance(c, int) else pl.multiple_of(c * c_, c_)

    def gates(hh, r0):
        h = heads[hh]
        qc = conv(xq_ref, 0, hh, r0)
        kc = conv(xk_ref, 1, hh, r0)
        v = conv(xv_ref, 2, hh, r0)
        q = qc * lax.rsqrt(jnp.sum(qc * qc, axis=-1, keepdims=True) + RMS_EPS) * (GDN_DK ** -0.5)
        k = kc * lax.rsqrt(jnp.sum(kc * kc, axis=-1, keepdims=True) + RMS_EPS)
        ba = pad_rows(ba_ref[pl.ds(r0, rows), :])
        b_raw = jnp.sum(jnp.where(lane == h, ba, 0.0), axis=-1, keepdims=True)
        a_raw = jnp.sum(jnp.where(lane == h + GDN_HEADS, ba, 0.0), axis=-1, keepdims=True)
        beta = jax.nn.sigmoid(b_raw)
        x = a_raw + dtb[hh]
        g = -jnp.exp(alog[hh]) * (jnp.maximum(x, 0.0) + jnp.log1p(jnp.exp(-jnp.abs(x))))
        if padded > seq:
            valid = (lax.broadcasted_iota(jnp.int32, (c_, 1), 0) + r0) < seq
            k = jnp.where(valid, k, 0.0)
            beta = jnp.where(valid, beta, 0.0)
            g = jnp.where(valid, g, 0.0)
        return q, k, v, beta, jnp.broadcast_to(g, (c_, LANES))

    def prep_group(gi, carry):
        hhs = [hh for hh in range(hb) for _ in range(group)]
        cs = [gi * group + i for _ in range(hb) for i in range(group)]
        r0s = [rows_of(c) for c in cs]
        every = lambda f, *xs: [f(*args) for args in zip(*xs)]
        q, k, v, beta, g = zip(*every(gates, hhs, r0s))
        g_hi, g_lo = zip(*every(_split, g))
        gc = every(lambda hi, lo: _mm(tril, hi) + _mm(tril, lo), g_hi, g_lo)
        kb = every(lambda x: x.astype(BF16), k)
        kk = every(_mm_nt, kb, kb)
        qk = every(lambda x, y: _mm_nt(x.astype(BF16), y), q, kb)
        decay = every(lambda x: jnp.where(incl, jnp.exp(jnp.where(incl, x[:, :c_] - x.T[:c_], 0.0)), 0.0), gc)
        low = every(lambda b_, d_, kk_: jnp.where(strict, b_ * d_ * kk_, 0.0), beta, decay, kk)
        pw = every(lambda x: -jnp.where(diag_blocks, x, 0.0), low)
        t_inv = every(lambda x: eye + x, pw)
        for _ in range(int(math.log2(INV_BLOCK)) - 1):
            pw = every(_mmb, pw, pw)
            t_inv = every(lambda t_, p_: t_ + _mmb(t_, p_), t_inv, pw)
        for merge in merges:
            off = every(lambda x, t_: _mmb(jnp.where(merge, x, 0.0), t_), low, t_inv)
            t_inv = every(lambda t_, x: t_ - _mmb(t_, x), t_inv, off)
        egc = every(jnp.exp, gc)
        rhs = every(lambda b_, e_, k_, v_: jnp.concatenate([(b_ * e_) * k_, b_ * v_], axis=1), beta, egc, k, v)
        w = every(lambda t_, x: _mmb(t_, x).astype(BF16), t_inv, rhs)
        gc_last = every(lambda x: x[c_ - 1:c_, :], gc)
        kdt = every(lambda k_, x, l_: (k_ * jnp.exp(l_ - x)).T.astype(BF16), k, gc, gc_last)
        a = every(lambda x, d_: (x * d_).astype(BF16), qk, decay)
        kw = every(_mm, kdt, w)
        aw = every(_mm, a, w)
        for i, (hh, c, r0) in enumerate(zip(hhs, cs, r0s)):
            sm_ref[hh * n_chunks + c] = kw[i][:, :GDN_DK].astype(BF16)
            sb_ref[hh * n_chunks + c] = kw[i][:, GDN_DK:]
            oq_ref[pl.ds(r0, c_), cols(hh)] = (q[i] * egc[i] - aw[i][:, :GDN_DK]).astype(BF16)
            ob_ref[pl.ds(r0, c_), cols(hh)] = aw[i][:, GDN_DK:]
            gl_ref[pl.ds(hh * n_chunks + c, 1), :] = jnp.exp(gc_last[i])
        return carry

    og = og_ref[...]

    def scan(c, states):
        r0 = rows_of(c)
        at = [hh * n_chunks + c for hh in range(hb)]
        sb = [s.astype(BF16) for s in states]
        upd = [_mm(sm_ref[i], x) for i, x in zip(at, sb)]
        o = [_mm(oq_ref[pl.ds(r0, c_), cols(hh)], sb[hh]) + ob_ref[pl.ds(r0, c_), cols(hh)] for hh in range(hb)]
        for hh in range(hb):
            z = z_ref[pl.ds(r0, rows), cols(hh)]
            o_ref[pl.ds(r0, rows), cols(hh)] = _rms(o[hh][:rows], og) * _silu(z)
        return tuple(gl_ref[pl.ds(i, 1), :] * s - u + sb_ref[i] for i, s, u in zip(at, states, upd))

    lax.fori_loop(0, n_chunks // group, prep_group, 0)
    final = lax.fori_loop(0, n_chunks, scan, tuple(s0_ref[hh] for hh in range(hb)),
                          unroll=math.gcd(SCAN_UNROLL, n_chunks))
    for hh in range(hb):
        sn_ref[hh] = final[hh]


def _gdn_core(qkvz, ba, conv_buf, conv_w, a_log, dt_bias, o_gain, s0):
    _, b, seq, _ = qkvz.shape
    chunk = GDN_CHUNK
    n_chunks = -(-seq // chunk)
    padded = n_chunks * chunk
    group = math.gcd(GDN_GROUP // GDN_MIN_HEADS, n_chunks)
    hb = math.gcd(GDN_GROUP // group, GDN_HEADS)
    width = hb * LANES
    kern = functools.partial(_gdn_kernel, seq=seq, chunk=chunk, n_chunks=n_chunks, group=group, hb=hb)
    head_cols = lambda part: pl.BlockSpec((None, None, seq, width), lambda bi, hi: (part, bi, 0, hi))
    smem = pl.BlockSpec(memory_space=pltpu.SMEM)
    return pl.pallas_call(
        kern,
        grid=(b, GDN_HEADS // hb),
        in_specs=[
            smem, smem,
            head_cols(0), head_cols(1), head_cols(2), head_cols(3),
            pl.BlockSpec((None, seq, LANES), lambda bi, hi: (bi, 0, 0)),
            pl.BlockSpec((None, 3, CONV_W - 1, width), lambda bi, hi: (bi, 0, 0, hi)),
            pl.BlockSpec((3, CONV_W, width), lambda bi, hi: (0, 0, hi)),
            pl.BlockSpec((1, GDN_DV), lambda bi, hi: (0, 0)),
            pl.BlockSpec((None, hb, GDN_DK, GDN_DV), lambda bi, hi: (bi, hi, 0, 0)),
        ],
        out_specs=[
            pl.BlockSpec((None, seq, width), lambda bi, hi: (bi, 0, hi)),
            pl.BlockSpec((None, hb, GDN_DK, GDN_DV), lambda bi, hi: (bi, hi, 0, 0)),
        ],
        out_shape=[
            jax.ShapeDtypeStruct((b, seq, GDN_VW), F32),
            jax.ShapeDtypeStruct((b, GDN_HEADS, GDN_DK, GDN_DV), F32),
        ],
        scratch_shapes=[
            pltpu.VMEM((padded + SUBLANES, width), F32),
            pltpu.VMEM((padded + SUBLANES, width), F32),
            pltpu.VMEM((padded + SUBLANES, width), F32),
            pltpu.VMEM((hb * n_chunks, GDN_DK, GDN_DK), BF16),
            pltpu.VMEM((hb * n_chunks, GDN_DK, GDN_DV), F32),
            pltpu.VMEM((padded, width), BF16),
            pltpu.VMEM((padded, width), F32),
            pltpu.VMEM((hb * n_chunks, LANES), F32),
        ],
        compiler_params=_params("parallel", "parallel"),
        name="gdn_core",
    )(a_log, dt_bias, qkvz, qkvz, qkvz, qkvz, ba, conv_buf, conv_w, o_gain.reshape(1, GDN_DV), s0)


def _rel_bucket(dist):
    max_exact = REL_BUCKETS // 2
    n = jnp.maximum(dist, 0)
    large = max_exact + (jnp.log(jnp.maximum(n, 1).astype(F32) / max_exact)
                         / math.log(REL_MAX_DIST / max_exact) * (REL_BUCKETS - max_exact)).astype(jnp.int32)
    large = jnp.minimum(large, REL_BUCKETS - 1)
    return jnp.where(n < max_exact, n, large)


def _bias_tiles_kernel(rb_ref, o_ref, *, t):
    h = pl.program_id(0)
    d = pl.program_id(1)
    r = lax.broadcasted_iota(jnp.int32, (t, t), 0)
    c = lax.broadcasted_iota(jnp.int32, (t, t), 1)
    dist = d * t + r - c
    bucket = _rel_bucket(dist)
    bias = jnp.zeros((t, t), F32)
    for b in range(REL_BUCKETS):
        bias = jnp.where(bucket == b, rb_ref[b, h], bias)
    o_ref[...] = jnp.where(dist >= 0, bias - rb_ref[REL_BUCKETS - 1, h], NEG_INF)


def _bias_tiles(rel_bias, t):
    assert t >= REL_MAX_DIST
    return pl.pallas_call(
        functools.partial(_bias_tiles_kernel, t=t),
        grid=(DA_HEADS, 2),
        in_specs=[pl.BlockSpec(memory_space=pltpu.SMEM)],
        out_specs=pl.BlockSpec((None, None, t, t), lambda h, d: (h, d, 0, 0)),
        out_shape=jax.ShapeDtypeStruct((DA_HEADS, 2, t, t), F32),
        compiler_params=_params("parallel", "parallel"),
        name="rel_bias_tiles",
    )(rel_bias)


def _lambda(lq1_ref, lk1_ref, lq2_ref, lk2_ref, lam_init):
    dot = lambda a, b: jnp.sum(a[...] * b[...], axis=-1, keepdims=True)
    return jnp.exp(dot(lq1_ref, lk1_ref)) - jnp.exp(dot(lq2_ref, lk2_ref)) + lam_init


def _da_prompt_kernel(q_ref, k_ref, v_ref, b_ref, lq1_ref, lk1_ref, lq2_ref, lk2_ref, sg_ref, o_ref,
                      kb_ref, vb_ref, s_ref, *, t, lam_init):
    seq = q_ref.shape[0]
    kb_ref[...] = k_ref[...].astype(BF16)
    vb_ref[:, :DA_VD] = v_ref[...].astype(BF16)
    vb_ref[:, DA_VD:] = jnp.ones((seq, LANES), BF16)
    lam = _lambda(lq1_ref, lk1_ref, lq2_ref, lk2_ref, lam_init)
    sg = sg_ref[...]
    lane = lax.broadcasted_iota(jnp.int32, (t, LANES), 1)
    stack = lambda x: jnp.concatenate([x, x], axis=0)
    for qi in range(seq // t):
        q = q_ref[qi * t:(qi + 1) * t, :] * (DA_DH ** -0.5)
        qq = jnp.concatenate([jnp.where(lane < DA_DH, q, 0.0), jnp.where(lane >= DA_DH, q, 0.0)], axis=0)
        n_k = (qi + 1) * t
        s_ref[:, :n_k] = _mm_nt(qq.astype(BF16), kb_ref[:n_k, :])
        s_ref[:, qi * t:n_k] += stack(b_ref[0])
        if qi >= 1:
            s_ref[:, (qi - 1) * t:qi * t] += stack(b_ref[1])
        s = s_ref[:, :n_k]
        p = jnp.exp(s - jnp.max(s, axis=-1, keepdims=True)).astype(BF16)
        acc = _mm(p, vb_ref[:n_k, :])
        a = acc[:, :DA_VD] / acc[:, DA_VD:]
        o = a[:t] - lam * a[t:]
        o_ref[qi * t:(qi + 1) * t, :] = _rms(o, sg) * (1.0 - lam_init)


def _da_prompt(qkv, bias_tiles, lq1, lk1, lq2, lk2, sub_gain, lam_init):
    _, b, seq, _ = qkv.shape
    t = bias_tiles.shape[-1]
    vec = lambda x: x.reshape(1, -1)
    vspec = lambda w: pl.BlockSpec((1, w), lambda bi, hi: (0, 0))
    head = lambda part: pl.BlockSpec((None, None, seq, LANES), lambda bi, hi: (part, bi, 0, hi))
    return pl.pallas_call(
        functools.partial(_da_prompt_kernel, t=t, lam_init=lam_init),
        grid=(b, DA_HEADS),
        in_specs=[
            head(0), head(1), head(2),
            pl.BlockSpec((None, 2, t, t), lambda bi, hi: (hi, 0, 0, 0)),
            vspec(DA_DH), vspec(DA_DH), vspec(DA_DH), vspec(DA_DH), vspec(DA_VD),
        ],
        out_specs=pl.BlockSpec((None, seq, LANES), lambda bi, hi: (bi, 0, hi)),
        out_shape=jax.ShapeDtypeStruct((b, seq, DA_HEADS * DA_VD), F32),
        scratch_shapes=[pltpu.VMEM((seq, DA_DH * 2), BF16), pltpu.VMEM((seq, DA_VD + LANES), BF16),
                        pltpu.VMEM((2 * t, seq), F32)],
        compiler_params=_params("parallel", "parallel"),
        name="da_prompt",
    )(qkv, qkv, qkv, bias_tiles, vec(lq1), vec(lk1), vec(lq2), vec(lk2), vec(sub_gain))


def _da_sample_kernel(pt_ref, q_ref, kn_ref, vn_ref, rbt_ref, lq1_ref, lk1_ref, lq2_ref, lk2_ref, sg_ref, *rest,
                      pages, n_pages, lq, lam_init):
    del pt_ref
    k_refs = rest[:pages]
    v_refs = rest[pages:2 * pages]
    o_ref, qbd_ref, m_ref, l_ref, acc_ref, s_ref = rest[2 * pages:]
    step = pl.program_id(1)
    rows = DA_HEADS * 2 * lq
    past_len = n_pages * PAGE_SIZE

    @pl.when(step == 0)
    def _():
        q = q_ref[...] * (DA_DH ** -0.5)
        qrep = jnp.concatenate([q] * (DA_HEADS * 2), axis=0)
        r_hc = lax.broadcasted_iota(jnp.int32, (rows, DA_HW), 0) // lq
        c_hc = lax.broadcasted_iota(jnp.int32, (rows, DA_HW), 1) // DA_DH
        qbd_ref[...] = jnp.where(r_hc == c_hc, qrep, 0.0).astype(BF16)
        m_ref[...] = jnp.full(m_ref.shape, NEG_INF, F32)
        l_ref[...] = jnp.zeros_like(l_ref)
        acc_ref[...] = jnp.zeros_like(acc_ref)

    def near_bias(k_start, n_valid):
        row = lax.broadcasted_iota(jnp.int32, (rows, PAGE_SIZE), 0)
        col = lax.broadcasted_iota(jnp.int32, (rows, PAGE_SIZE), 1)
        dist = past_len + row % lq - (k_start + col)
        bucket = _rel_bucket(dist)
        bias = jnp.zeros((rows, PAGE_SIZE), F32)
        for b in range(REL_BUCKETS):
            bias = jnp.where(bucket == b, rbt_ref[:, b:b + 1], bias)
        bias = bias - rbt_ref[:, REL_BUCKETS - 1:REL_BUCKETS]
        return jnp.where(jnp.logical_and(dist >= 0, col < n_valid), bias, NEG_INF)

    def attend(s_blocks, v_blocks):
        s_max = functools.reduce(jnp.maximum, s_blocks)
        m_prev = m_ref[...]
        m_new = jnp.maximum(m_prev, jnp.max(s_max, axis=-1, keepdims=True))
        alpha = jnp.exp(m_prev - m_new)
        p = [jnp.exp(s - m_new) for s in s_blocks]
        l_ref[...] = alpha * l_ref[...] + jnp.sum(functools.reduce(jnp.add, p), axis=-1, keepdims=True)
        pb = jnp.concatenate([x.astype(BF16) for x in p], axis=1)
        pv = [_mm(pb[h * 2 * lq:(h + 1) * 2 * lq], jnp.concatenate([v[h] for v in v_blocks], axis=0))
              for h in range(DA_HEADS)]
        acc_ref[...] = alpha * acc_ref[...] + jnp.concatenate(pv, axis=0)
        m_ref[...] = m_new

    for r in range(pages):
        s_ref[r] = _mm(qbd_ref[...], k_refs[r][...].astype(BF16))

    last = step == pl.num_programs(1) - 1

    @pl.when(last)
    def _():
        s_ref[pages - 1] += near_bias(past_len - PAGE_SIZE, PAGE_SIZE)

    attend([s_ref[r] for r in range(pages)],
           [[v_refs[r][pl.ds(h, PAGE_SIZE, stride=DA_HEADS), :].astype(BF16) for h in range(DA_HEADS)]
            for r in range(pages)])

    @pl.when(last)
    def _():
        zeros = jnp.zeros((PAGE_SIZE - lq, DA_HW), F32)
        kn = jnp.concatenate([kn_ref[...], zeros], axis=0).astype(BF16)
        vn = jnp.concatenate([vn_ref[...], zeros], axis=0).astype(BF16)
        attend([_mm_nt(qbd_ref[...], kn) + near_bias(past_len, lq)],
               [[vn[:, h * DA_VD:(h + 1) * DA_VD] for h in range(DA_HEADS)]])
        a = acc_ref[...] / l_ref[...]
        lam = _lambda(lq1_ref, lk1_ref, lq2_ref, lk2_ref, lam_init)
        sg = sg_ref[...]
        heads = []
        for h in range(DA_HEADS):
            o = a[h * 2 * lq:h * 2 * lq + lq] - lam * a[h * 2 * lq + lq:(h + 1) * 2 * lq]
            heads.append(_rms(o, sg) * (1.0 - lam_init))
        o_ref[...] = jnp.concatenate(heads, axis=-1)


def _da_sample(qkv, cache_k, cache_v, layer, page_table, rel_bias, lq1, lk1, lq2, lk2, sub_gain, lam_init):
    _, b, lq, _ = qkv.shape
    n_pages = page_table.shape[1]
    pages = math.gcd(PAGES_PER_STEP, n_pages)
    rows = DA_HEADS * 2 * lq
    assert rows == LANES and lq == SUBLANES
    rbt = jnp.repeat(rel_bias.T, 2 * lq, axis=0)
    vec = lambda x: x.reshape(1, -1)
    vspec = lambda w: pl.BlockSpec((1, w), lambda bi, si, pt: (0, 0))
    new = lambda part: pl.BlockSpec((None, None, lq, DA_HW), lambda bi, si, pt: (part, bi, 0, 0))
    page = lambda r, shape: pl.BlockSpec((None, None) + shape,
                                         lambda bi, si, pt: (layer, pt[bi, si * pages + r], 0, 0))
    k_shape, v_shape = cache_k.shape[2:], cache_v.shape[2:]
    grid_spec = pltpu.PrefetchScalarGridSpec(
        num_scalar_prefetch=1,
        grid=(b, n_pages // pages),
        in_specs=[new(0), new(1), new(2),
                  pl.BlockSpec((rows, REL_BUCKETS), lambda bi, si, pt: (0, 0)),
                  vspec(DA_DH), vspec(DA_DH), vspec(DA_DH), vspec(DA_DH), vspec(DA_VD)]
                 + [page(r, k_shape) for r in range(pages)] + [page(r, v_shape) for r in range(pages)],
        out_specs=pl.BlockSpec((None, lq, DA_HW), lambda bi, si, pt: (bi, 0, 0)),
        scratch_shapes=[pltpu.VMEM((rows, DA_HW), BF16), pltpu.VMEM((rows, LANES), F32),
                        pltpu.VMEM((rows, LANES), F32), pltpu.VMEM((rows, DA_VD), F32),
                        pltpu.VMEM((pages, rows, PAGE_SIZE), F32)],
    )
    return pl.pallas_call(
        functools.partial(_da_sample_kernel, pages=pages, n_pages=n_pages, lq=lq, lam_init=lam_init),
        grid_spec=grid_spec,
        out_shape=jax.ShapeDtypeStruct((b, lq, DA_HW), F32),
        compiler_params=_params("parallel", "arbitrary"),
        name="da_sample",
    )(page_table, qkv, qkv, qkv, rbt, vec(lq1), vec(lk1), vec(lq2), vec(lk2), vec(sub_gain),
      *([cache_k] * pages), *([cache_v] * pages))


def _gdn_layer(y, conv_buf, s0, g_pre, w_main, w_ba, conv_w, a_log, dt_bias, o_gain, w_out, g_post):
    b, seq, d = y.shape
    yf = y.reshape(b * seq, d)
    qkvz, ba = _norm_matmul(yf, g_pre, w_main, D_MODEL, w_narrow=w_ba)
    qkvz = qkvz.reshape(4, b, seq, D_MODEL)
    ba = ba.reshape(b, seq, LANES)
    cb = jnp.transpose(conv_buf.reshape(b, CONV_W - 1, 3, D_MODEL), (0, 2, 1, 3))
    o, s_new = _gdn_core(qkvz, ba, cb, conv_w, a_log, dt_bias, o_gain, s0)
    tail = jnp.concatenate([qkvz[part][:, -(CONV_W - 1):] for part in range(3)], axis=-1)
    pre = jnp.concatenate([conv_buf, tail], axis=1)
    new_buf = pre[:, -(CONV_W - 1):]
    y = _matmul_postnorm(o.reshape(b * seq, GDN_VW), w_out, g_post, yf).reshape(b, seq, d)
    return y, new_buf, s_new


def kernel(x_prompt, x_sample, state_gdn, state_conv, cache_k, cache_v, cache_mem_k, cache_mem_v, page_table,
           mem_prompt, rel_bias, norm_pre, norm_post, gdn_w_in, gdn_conv_w, gdn_a_log, gdn_dt_bias, gdn_o_gain,
           gdn_w_out, da_w_in, da_lq1, da_lk1, da_lq2, da_lk2, da_sub_gain, da_w_out, mem_gain, w_xq, w_xkv, w_xo,
           ffn_w_gu, ffn_w_down):
    depth = norm_pre.shape[0]
    b_p, l_p, d = x_prompt.shape
    b_s, l_s, _ = x_sample.shape
    m_len = mem_prompt.shape[1]
    yp, ys = x_prompt, x_sample
    mem_flat = mem_prompt.reshape(b_p * m_len, d)
    ck = jnp.transpose(cache_k, (0, 1, 3, 4, 5, 2)).reshape(cache_k.shape[:2] + (DA_HW, PAGE_SIZE))
    cv = cache_v.reshape(cache_v.shape[:2] + (PAGE_SIZE * DA_HEADS, DA_VD))
    mem_rows = lambda x: jnp.transpose(
        x.reshape(x.shape[:4] + (X_DH // LANES, LANES)), (0, 1, 2, 4, 3, 5)).reshape(x.shape[:2] + (-1, LANES))
    cmk, cmv = mem_rows(cache_mem_k), mem_rows(cache_mem_v)
    bias_tiles = _bias_tiles(rel_bias, min(ATTN_TILE, l_p))
    gdn_p, conv_p, gdn_s, conv_s = [], [], [], []
    k_p, v_p, k_s, v_s = [], [], [], []
    mk_p, mv_p = [], []
    for i in range(depth):
        j = i // N_MIXERS
        g_pre, g_post = norm_pre[i], norm_post[i]
        if i % N_MIXERS == 0:
            w_in = gdn_w_in[j]
            n_main = CONV_DIM + GDN_VW
            w_main = w_in[:, :n_main].astype(BF16)
            w_ba = jnp.pad(w_in[:, n_main:], ((0, 0), (0, LANES - 2 * GDN_HEADS))).astype(BF16)
            conv_w = jnp.transpose(gdn_conv_w[j].reshape(CONV_W, 3, D_MODEL), (1, 0, 2))
            gw = (g_pre[0], w_main, w_ba, conv_w, gdn_a_log[j], gdn_dt_bias[j], gdn_o_gain[j],
                  gdn_w_out[j].astype(BF16), g_post[0])
            buf0 = jnp.zeros((b_p, CONV_W - 1, CONV_DIM), F32)
            s0 = jnp.zeros((b_p, GDN_HEADS, GDN_DK, GDN_DV), F32)
            yp, cb, st = _gdn_layer(yp, buf0, s0, *gw)
            gdn_p.append(st)
            conv_p.append(cb)
            ys, cb, st = _gdn_layer(ys, state_conv[j], state_gdn[j], *gw)
            gdn_s.append(st)
            conv_s.append(cb)
        else:
            lam_init = 0.8 - 0.6 * math.exp(-0.3 * i)
            w_in = da_w_in[j].astype(BF16)
            w_out = da_w_out[j].astype(BF16)
            lw = (da_lq1[j], da_lk1[j], da_lq2[j], da_lk2[j], da_sub_gain[j], lam_init)
            qkv = _norm_matmul(yp.reshape(b_p * l_p, d), g_pre[0], w_in, DA_HW).reshape(3, b_p, l_p, DA_HW)
            o = _da_prompt(qkv, bias_tiles, *lw)
            yp = _matmul_postnorm(o.reshape(b_p * l_p, DA_HW), w_out, g_post[0],
                                  yp.reshape(b_p * l_p, d)).reshape(b_p, l_p, d)
            k_p.append(qkv[1].reshape(b_p, l_p, DA_HEADS, 2, DA_DH))
            v_p.append(qkv[2].reshape(b_p, l_p, DA_HEADS, DA_VD))
            qkv = _norm_matmul(ys.reshape(b_s * l_s, d), g_pre[0], w_in, DA_HW).reshape(3, b_s, l_s, DA_HW)
            o = _da_sample(qkv, ck, cv, j, page_table, rel_bias, *lw)
            ys = _matmul_postnorm(o.reshape(b_s * l_s, DA_HW), w_out, g_post[0],
                                  ys.reshape(b_s * l_s, d)).reshape(b_s, l_s, d)
            k_s.append(qkv[1].reshape(b_s, l_s, DA_HEADS, 2, DA_DH))
            v_s.append(qkv[2].reshape(b_s, l_s, DA_HEADS, DA_VD))
        mkv = _norm_matmul(mem_flat, mem_gain[i], w_xkv[i].astype(BF16), d).reshape(2, b_p, m_len, d)
        mk_p.append(mkv[0].reshape(b_p, m_len, X_HEADS, X_DH))
        mv_p.append(mkv[1].reshape(b_p, m_len, X_HEADS, X_DH))
        w_q, w_o = w_xq[i].astype(BF16), w_xo[i].astype(BF16)
        yp = _cross_attn(yp, g_pre[1], w_q, mkv, mkv, 0, 1, w_o, g_post[1])
        ys = _cross_attn(ys, g_pre[1], w_q, cmk, cmv, i, i, w_o, g_post[1])
        w_gu, w_down = ffn_w_gu[i].astype(BF16), ffn_w_down[i].astype(BF16)
        yp = _ffn(yp.reshape(b_p * l_p, d), g_pre[2], w_gu, w_down, g_post[2]).reshape(b_p, l_p, d)
        ys = _ffn(ys.reshape(b_s * l_s, d), g_pre[2], w_gu, w_down, g_post[2]).reshape(b_s, l_s, d)
    return (yp, ys,
            jnp.stack(gdn_p), jnp.stack(conv_p), jnp.stack(k_p), jnp.stack(v_p),
            jnp.stack(mk_p), jnp.stack(mv_p),
            jnp.stack(gdn_s), jnp.stack(conv_s), jnp.stack(k_s), jnp.stack(v_s))
```

```python
import functools
import math

import jax
import jax.numpy as jnp
from jax import lax
from jax.experimental import pallas as pl
from jax.experimental.pallas import tpu as pltpu

F32 = jnp.float32
BF16 = jnp.bfloat16

D_MODEL = 1024
N_MIXERS = 2
GDN_HEADS = 8
GDN_DK = 128
GDN_DV = 128
CONV_W = 4
GDN_CHUNK = 64
INV_BLOCK = 8
GDN_GROUP = 8
GDN_MIN_HEADS = 2
SCAN_UNROLL = 4
GDN_QK = GDN_HEADS * GDN_DK
GDN_VW = GDN_HEADS * GDN_DV
CONV_DIM = 2 * GDN_QK + GDN_VW
DA_HEADS = 8
DA_DH = 64
DA_VD = 2 * DA_DH
DA_HW = DA_HEADS * 2 * DA_DH
PAGE_SIZE = 128
REL_BUCKETS = 32
REL_MAX_DIST = 128
X_HEADS = 4
X_DH = D_MODEL // X_HEADS
D_FF = -(-8 * D_MODEL // (3 * 256)) * 256
RMS_EPS = 1e-6
NEG_INF = -1e30

LANES = 128
SUBLANES = 8
ROW_TILE = 512
TAIL_ROW_TILE = 512
MXU_TILE = 256
TAIL_FFN_BOUNDS = (0, -(-D_FF // (2 * MXU_TILE)) * MXU_TILE, D_FF)
ATTN_TILE = 256
PAGES_PER_STEP = 8
VMEM_LIMIT = 56 * 1024 * 1024

_NT = (((1,), (1,)), ((), ()))


def _params(*sem):
    return pltpu.CompilerParams(dimension_semantics=sem, vmem_limit_bytes=VMEM_LIMIT)


def _row_tile(t, rows=ROW_TILE):
    return rows if t % rows == 0 else t


def _rms(x, g):
    return x * lax.rsqrt(jnp.mean(x * x, axis=-1, keepdims=True) + RMS_EPS) * g


def _silu(x):
    return x * jax.nn.sigmoid(x)


def _mm(a, b):
    return jnp.dot(a, b, preferred_element_type=F32)


def _mm_nt(a, b):
    return lax.dot_general(a, b, _NT, preferred_element_type=F32)


def _split(x):
    hi = x.astype(BF16)
    return hi, (x - hi.astype(F32)).astype(BF16)


def _mmb(a, b):
    return _mm(a.astype(BF16), b.astype(BF16))


def _norm_matmul_kernel(x_ref, g_ref, *rest, n_w):
    w_refs, o_refs = rest[:n_w], rest[n_w:]
    h = _rms(x_ref[...], g_ref[...]).astype(BF16)
    tn = o_refs[0].shape[1]
    for j, o_ref in enumerate(o_refs[:-1] if n_w == 2 else o_refs):
        o_ref[...] = _mm(h, w_refs[0][:, j * tn:(j + 1) * tn])
    if n_w == 2:
        o_refs[-1][...] = _mm(h, w_refs[1][...])


def _norm_matmul(x, g, w, tn, w_narrow=None):
    t, d = x.shape
    n = w.shape[1]
    tm = _row_tile(t)
    resident = lambda a: pl.BlockSpec(a.shape, lambda i: (0, 0), pipeline_mode=pl.Buffered(1))
    weights = [w] if w_narrow is None else [w, w_narrow]
    widths = [tn] * (n // tn) + ([] if w_narrow is None else [LANES])
    return pl.pallas_call(
        functools.partial(_norm_matmul_kernel, n_w=len(weights)),
        grid=(t // tm,),
        in_specs=[pl.BlockSpec((tm, d), lambda i: (i, 0)), pl.BlockSpec((1, d), lambda i: (0, 0))]
                 + [resident(a) for a in weights],
        out_specs=[pl.BlockSpec((tm, width), lambda i: (i, 0)) for width in widths],
        out_shape=[jax.ShapeDtypeStruct((t, width), F32) for width in widths],
        compiler_params=_params("parallel"),
        name="norm_matmul",
    )(x, g.reshape(1, d), *weights)


def _matmul_postnorm_kernel(a_ref, w_ref, g_ref, r_ref, o_ref):
    y = _mm(a_ref[...].astype(BF16), w_ref[...])
    o_ref[...] = r_ref[...] + _rms(y, g_ref[...])


def _matmul_postnorm(a, w, g, res):
    t, k = a.shape
    d = w.shape[1]
    tm = _row_tile(t)
    return pl.pallas_call(
        _matmul_postnorm_kernel,
        grid=(t // tm,),
        in_specs=[
            pl.BlockSpec((tm, k), lambda i: (i, 0)),
            pl.BlockSpec((k, d), lambda i: (0, 0)),
            pl.BlockSpec((1, d), lambda i: (0, 0)),
            pl.BlockSpec((tm, d), lambda i: (i, 0)),
        ],
        out_specs=pl.BlockSpec((tm, d), lambda i: (i, 0)),
        out_shape=jax.ShapeDtypeStruct((t, d), F32),
        compiler_params=_params("parallel"),
        name="matmul_postnorm",
    )(a, w, g.reshape(1, d), res)


def _ffn_rows(x, g_pre, wg_ref, wu_ref, wd_ref, g_post, bounds=(0, D_FF)):
    h = _rms(x, g_pre).astype(BF16)
    y = None
    for lo, hi in zip(bounds[:-1], bounds[1:]):
        cols = slice(lo, hi)
        a = (_silu(_mm(h, wg_ref[:, cols])) * _mm(h, wu_ref[:, cols])).astype(BF16)
        part = _mm(a, wd_ref[cols, :])
        y = part if y is None else y + part
    return x + _rms(y, g_post)


def _ffn_kernel(x_ref, gpre_ref, wg_ref, wu_ref, wd_ref, gpost_ref, o_ref):
    o_ref[...] = _ffn_rows(x_ref[...], gpre_ref[...], wg_ref, wu_ref, wd_ref, gpost_ref[...])


def _ffn(x, g_pre, w_gu, w_down, g_post):
    t, d = x.shape
    tm = _row_tile(t)
    resident = lambda shape, col: pl.BlockSpec(shape, lambda i: (0, col), pipeline_mode=pl.Buffered(1))
    return pl.pallas_call(
        _ffn_kernel,
        grid=(t // tm,),
        in_specs=[
            pl.BlockSpec((tm, d), lambda i: (i, 0)),
            pl.BlockSpec((1, d), lambda i: (0, 0)),
            resident((d, D_FF), 0),
            resident((d, D_FF), 1),
            resident((D_FF, d), 0),
            pl.BlockSpec((1, d), lambda i: (0, 0)),
        ],
        out_specs=pl.BlockSpec((tm, d), lambda i: (i, 0)),
        out_shape=jax.ShapeDtypeStruct((t, d), F32),
        compiler_params=_params("parallel"),
        name="ffn",
    )(x, g_pre.reshape(1, d), w_gu, w_gu, w_down, g_post.reshape(1, d))


def _cross_attn_rows(y, g_pre, wq_ref, mk, mv, wo_ref, g_post):
    h = _rms(y, g_pre).astype(BF16)
    q = (_mm(h, wq_ref[...]) * (X_DH ** -0.5)).astype(BF16)
    heads = []
    for hd in range(X_HEADS):
        sl = slice(hd * X_DH, (hd + 1) * X_DH)
        s = _mm_nt(q[:, sl], mk[:, sl])
        p = jnp.exp(s - jnp.max(s, axis=-1, keepdims=True))
        heads.append(_mm(p.astype(BF16), mv[:, sl]) / jnp.sum(p, axis=-1, keepdims=True))
    o = jnp.concatenate(heads, axis=-1).astype(BF16)
    return y + _rms(_mm(o, wo_ref[...]), g_post)


def _cross_attn_kernel(y_ref, gpre_ref, wq_ref, mk_ref, mv_ref, wo_ref, gpost_ref, o_ref, *, tiled_mem):
    if tiled_mem:
        tiles = X_DH // LANES
        m = mk_ref.shape[0] // (tiles * X_HEADS)
        gather = lambda ref: jnp.concatenate(
            [ref[pl.ds(lt * X_HEADS + hd, m, stride=tiles * X_HEADS), :]
             for hd in range(X_HEADS) for lt in range(tiles)], axis=1).astype(BF16)
        mk, mv = gather(mk_ref), gather(mv_ref)
    else:
        mk = mk_ref[...].astype(BF16)
        mv = mv_ref[...].astype(BF16)
    o_ref[...] = _cross_attn_rows(y_ref[...], gpre_ref[...], wq_ref, mk, mv, wo_ref, gpost_ref[...])


def _cross_attn(y, g_pre, w_q, mem_k, mem_v, layer_k, layer_v, w_o, g_post):
    b, l, d = y.shape
    mem_block = mem_k.shape[2:]
    tm = _row_tile(l)
    return pl.pallas_call(
        functools.partial(_cross_attn_kernel, tiled_mem=mem_block[1] != d),
        grid=(b, l // tm),
        in_specs=[
            pl.BlockSpec((None, tm, d), lambda bi, i: (bi, i, 0)),
            pl.BlockSpec((1, d), lambda bi, i: (0, 0)),
            pl.BlockSpec((d, d), lambda bi, i: (0, 0)),
            pl.BlockSpec((None, None) + mem_block, lambda bi, i: (layer_k, bi, 0, 0)),
            pl.BlockSpec((None, None) + mem_block, lambda bi, i: (layer_v, bi, 0, 0)),
            pl.BlockSpec((d, d), lambda bi, i: (0, 0)),
            pl.BlockSpec((1, d), lambda bi, i: (0, 0)),
        ],
        out_specs=pl.BlockSpec((None, tm, d), lambda bi, i: (bi, i, 0)),
        out_shape=jax.ShapeDtypeStruct((b, l, d), F32),
        compiler_params=_params("parallel", "parallel"),
        name="cross_attn",
    )(y, g_pre.reshape(1, d), w_q, mem_k, mem_v, w_o, g_post.reshape(1, d))


def _layer_tail_kernel(a_ref, y_ref, gains_ref, wout_ref, wq_ref, mk_ref, mv_ref, wo_ref, wg_ref, wu_ref, wd_ref,
                       o_ref):
    gain = lambda r: gains_ref[r:r + 1, :]
    y = y_ref[...] + _rms(_mm(a_ref[...].astype(BF16), wout_ref[...]), gain(0))
    y = _cross_attn_rows(y, gain(1), wq_ref, mk_ref[...].astype(BF16), mv_ref[...].astype(BF16), wo_ref, gain(2))
    o_ref[...] = _ffn_rows(y, gain(3), wg_ref, wu_ref, wd_ref, gain(4), bounds=TAIL_FFN_BOUNDS)


def _layer_tail(a, y, gains, w_out, w_q, mem_k, mem_v, w_o, w_gu, w_down):
    b, l, d = y.shape
    k = a.shape[2]
    m = mem_k.shape[1]
    tm = _row_tile(l, TAIL_ROW_TILE)
    rows = lambda width: pl.BlockSpec((None, tm, width), lambda bi, i: (bi, i, 0))
    resident = lambda shape, col=0: pl.BlockSpec(shape, lambda bi, i: (0, col), pipeline_mode=pl.Buffered(1))
    mem = pl.BlockSpec((None, m, d), lambda bi, i: (bi, 0, 0))
    return pl.pallas_call(
        _layer_tail_kernel,
        grid=(b, l // tm),
        in_specs=[
            rows(k), rows(d), resident(gains.shape),
            resident((k, d)), resident((d, d)), mem, mem, resident((d, d)),
            resident((d, D_FF), 0), resident((d, D_FF), 1), resident((D_FF, d)),
        ],
        out_specs=rows(d),
        out_shape=jax.ShapeDtypeStruct((b, l, d), F32),
        compiler_params=_params("parallel", "parallel"),
        name="layer_tail",
    )(a, y, gains, w_out, w_q, mem_k, mem_v, w_o, w_gu, w_gu, w_down)


def _gdn_kernel(alog_ref, dtb_ref, q_ref, k_ref, v_ref, z_ref, ba_ref, cb_ref, cw_ref, og_ref, s0_ref,
                o_ref, sn_ref,
                xq_ref, xk_ref, xv_ref, sm_ref, sb_ref, oq_ref, ob_ref, gl_ref, *, seq, chunk, n_chunks, group, hb):
    c_ = chunk
    padded = n_chunks * c_
    rows = min(seq, c_)
    width = hb * LANES
    heads = [pl.program_id(1) * hb + hh for hh in range(hb)]
    cols = lambda hh: slice(hh * LANES, (hh + 1) * LANES)

    for part, (x_ref, src) in enumerate(((xq_ref, q_ref), (xk_ref, k_ref), (xv_ref, v_ref))):
        x_ref[0:SUBLANES, :] = jnp.zeros((SUBLANES, width), F32)
        x_ref[SUBLANES - (CONV_W - 1):SUBLANES, :] = cb_ref[part]
        x_ref[pl.ds(SUBLANES, seq), :] = src[...]
        if padded > seq:
            x_ref[pl.ds(SUBLANES + seq, padded - seq), :] = jnp.zeros((padded - seq, width), F32)

    alog = [jnp.full((1, 1), alog_ref[h], F32) for h in heads]
    dtb = [jnp.full((1, 1), dtb_ref[h], F32) for h in heads]
    r_i =lax.broadcasted_iota(jnp.int32, (c_, c_), 0)
    c_i = lax.broadcasted_iota(jnp.int32, (c_, c_), 1)
    incl = r_i >= c_i
    strict = r_i > c_i
    tril = jnp.where(incl, 1.0, 0.0).astype(BF16)
    eye = jnp.where(r_i == c_i, 1.0, 0.0).astype(F32)
    blk = lambda size: (r_i // size) == (c_i // size)
    diag_blocks = blk(INV_BLOCK)
    merges = [jnp.logical_and(blk(2 * size), jnp.logical_not(blk(size)))
              for size in (INV_BLOCK << i for i in range(int(math.log2(c_ // INV_BLOCK))))]
    lane = lax.broadcasted_iota(jnp.int32, (c_, LANES), 1)

    def conv(x_ref, part, hh, r0):
        win = x_ref[pl.ds(r0, c_ + SUBLANES), cols(hh)]
        w = cw_ref[part][:, cols(hh)]
        base = SUBLANES - (CONV_W - 1)
        y = win[base:base + c_] * w[0:1]
        for j in range(1, CONV_W):
            y = y + win[base + j:base + j + c_] * w[j:j + 1]
        return _silu(y)

    def pad_rows(x):
        if rows == c_:
            return x
        return jnp.concatenate([x, jnp.zeros((c_ - rows, x.shape[1]), x.dtype)], axis=0)

    def rows_of(c):
        return c * c_ if isinstance(c, int) else pl.multiple_of(c * c_, c_)

    def gates(hh, r0):
        h = heads[hh]
        qc = conv(xq_ref, 0, hh, r0)
        kc = conv(xk_ref, 1, hh, r0)
        v = conv(xv_ref, 2, hh, r0)
        q = qc * lax.rsqrt(jnp.sum(qc * qc, axis=-1, keepdims=True) + RMS_EPS) * (GDN_DK ** -0.5)
        k = kc * lax.rsqrt(jnp.sum(kc * kc, axis=-1, keepdims=True) + RMS_EPS)
        ba = pad_rows(ba_ref[pl.ds(r0, rows), :])
        b_raw = jnp.sum(jnp.where(lane == h, ba, 0.0), axis=-1, keepdims=True)
        a_raw = jnp.sum(jnp.where(lane == h + GDN_HEADS, ba, 0.0), axis=-1, keepdims=True)
        beta = jax.nn.sigmoid(b_raw)
        x = a_raw + dtb[hh]
        g = -jnp.exp(alog[hh]) * (jnp.maximum(x, 0.0) + jnp.log1p(jnp.exp(-jnp.abs(x))))
        if padded > seq:
            valid = (lax.broadcasted_iota(jnp.int32, (c_, 1), 0) + r0) < seq
            k = jnp.where(valid, k, 0.0)
            beta = jnp.where(valid, beta, 0.0)
            g = jnp.where(valid, g, 0.0)
        return q, k, v, beta, jnp.broadcast_to(g, (c_, LANES))

    def prep_group(gi, carry):
        hhs = [hh for hh in range(hb) for _ in range(group)]
        cs = [gi * group + i for _ in range(hb) for i in range(group)]
        r0s = [rows_of(c) for c in cs]
        every = lambda f, *xs: [f(*args) for args in zip(*xs)]
        q, k, v, beta, g = zip(*every(gates, hhs, r0s))
        g_hi, g_lo = zip(*every(_split, g))
        gc = every(lambda hi, lo: _mm(tril, hi) + _mm(tril, lo), g_hi, g_lo)
        kb = every(lambda x: x.astype(BF16), k)
        kk = every(_mm_nt, kb, kb)
        qk = every(lambda x, y: _mm_nt(x.astype(BF16), y), q, kb)
        decay = every(lambda x: jnp.where(incl, jnp.exp(jnp.where(incl, x[:, :c_] - x.T[:c_], 0.0)), 0.0), gc)
        low = every(lambda b_, d_, kk_: jnp.where(strict, b_ * d_ * kk_, 0.0), beta, decay, kk)
        pw = every(lambda x: -jnp.where(diag_blocks, x, 0.0), low)
        t_inv = every(lambda x: eye + x, pw)
        for _ in range(int(math.log2(INV_BLOCK)) - 1):
            pw = every(_mmb, pw, pw)
            t_inv = every(lambda t_, p_: t_ + _mmb(t_, p_), t_inv, pw)
        for merge in merges:
            off = every(lambda x, t_: _mmb(jnp.where(merge, x, 0.0), t_), low, t_inv)
            t_inv = every(lambda t_, x: t_ - _mmb(t_, x), t_inv, off)
        egc = every(jnp.exp, gc)
        rhs = every(lambda b_, e_, k_, v_: jnp.concatenate([(b_ * e_) * k_, b_ * v_], axis=1), beta, egc, k, v)
        w = every(lambda t_, x: _mmb(t_, x).astype(BF16), t_inv, rhs)
        gc_last = every(lambda x: x[c_ - 1:c_, :], gc)
        kdt = every(lambda k_, x, l_: (k_ * jnp.exp(l_ - x)).T.astype(BF16), k, gc, gc_last)
        a = every(lambda x, d_: (x * d_).astype(BF16), qk, decay)
        kw = every(_mm, kdt, w)
        aw = every(_mm, a, w)
        for i, (hh, c, r0) in enumerate(zip(hhs, cs, r0s)):
            sm_ref[hh * n_chunks + c] = kw[i][:, :GDN_DK].astype(BF16)
            sb_ref[hh * n_chunks + c] = kw[i][:, GDN_DK:]
            oq_ref[pl.ds(r0, c_), cols(hh)] = (q[i] * egc[i] - aw[i][:, :GDN_DK]).astype(BF16)
            ob_ref[pl.ds(r0, c_), cols(hh)] = aw[i][:, GDN_DK:]
            gl_ref[pl.ds(hh * n_chunks + c, 1), :] = jnp.exp(gc_last[i])
        return carry

    og = og_ref[...]

    def scan(c, states):
        r0 = rows_of(c)
        at = [hh * n_chunks + c for hh in range(hb)]
        sb = [s.astype(BF16) for s in states]
        upd = [_mm(sm_ref[i], x) for i, x in zip(at, sb)]
        o = [_mm(oq_ref[pl.ds(r0, c_), cols(hh)], sb[hh]) + ob_ref[pl.ds(r0, c_), cols(hh)] for hh in range(hb)]
        for hh in range(hb):
            z = z_ref[pl.ds(r0, rows), cols(hh)]
            o_ref[pl.ds(r0, rows), cols(hh)] = _rms(o[hh][:rows], og) * _silu(z)
        return tuple(gl_ref[pl.ds(i, 1), :] * s - u + sb_ref[i] for i, s, u in zip(at, states, upd))

    lax.fori_loop(0, n_chunks // group, prep_group, 0)
    final = lax.fori_loop(0, n_chunks, scan, tuple(s0_ref[hh] for hh in range(hb)),
                          unroll=math.gcd(SCAN_UNROLL, n_chunks))
    for hh in range(hb):
        sn_ref[hh] = final[hh]


def _gdn_core(qkvz, ba, conv_buf, conv_w, a_log, dt_bias, o_gain, s0):
    b, seq, _ = qkvz[0].shape
    chunk = GDN_CHUNK
    n_chunks = -(-seq // chunk)
    padded = n_chunks * chunk
    group = math.gcd(GDN_GROUP // GDN_MIN_HEADS, n_chunks)
    hb = math.gcd(GDN_GROUP // group, GDN_HEADS)
    width = hb * LANES
    kern = functools.partial(_gdn_kernel, seq=seq, chunk=chunk, n_chunks=n_chunks, group=group, hb=hb)
    head_cols = pl.BlockSpec((None, seq, width), lambda bi, hi: (bi, 0, hi))
    smem = pl.BlockSpec(memory_space=pltpu.SMEM)
    return pl.pallas_call(
        kern,
        grid=(b, GDN_HEADS // hb),
        in_specs=[
            smem, smem,
            head_cols, head_cols, head_cols, head_cols,
            pl.BlockSpec((None, seq, LANES), lambda bi, hi: (bi, 0, 0)),
            pl.BlockSpec((None, 3, CONV_W - 1, width), lambda bi, hi: (bi, 0, 0, hi)),
            pl.BlockSpec((3, CONV_W, width), lambda bi, hi: (0, 0, hi)),
            pl.BlockSpec((1, GDN_DV), lambda bi, hi: (0, 0)),
            pl.BlockSpec((None, hb, GDN_DK, GDN_DV), lambda bi, hi: (bi, hi, 0, 0)),
        ],
        out_specs=[
            pl.BlockSpec((None, seq, width), lambda bi, hi: (bi, 0, hi)),
            pl.BlockSpec((None, hb, GDN_DK, GDN_DV), lambda bi, hi: (bi, hi, 0, 0)),
        ],
        out_shape=[
            jax.ShapeDtypeStruct((b, seq, GDN_VW), F32),
            jax.ShapeDtypeStruct((b, GDN_HEADS, GDN_DK, GDN_DV), F32),
        ],
        scratch_shapes=[
            pltpu.VMEM((padded + SUBLANES, width), F32),
            pltpu.VMEM((padded + SUBLANES, width), F32),
            pltpu.VMEM((padded + SUBLANES, width), F32),
            pltpu.VMEM((hb * n_chunks, GDN_DK, GDN_DK), BF16),
            pltpu.VMEM((hb * n_chunks, GDN_DK, GDN_DV), F32),
            pltpu.VMEM((padded, width), BF16),
            pltpu.VMEM((padded, width), F32),
            pltpu.VMEM((hb * n_chunks, LANES), F32),
        ],
        compiler_params=_params("parallel", "parallel"),
        name="gdn_core",
    )(a_log, dt_bias, *qkvz, ba, conv_buf, conv_w, o_gain.reshape(1, GDN_DV), s0)


def _rel_bucket(dist):
    max_exact = REL_BUCKETS // 2
    n = jnp.maximum(dist, 0)
    large = max_exact + (jnp.log(jnp.maximum(n, 1).astype(F32) / max_exact)
                         / math.log(REL_MAX_DIST / max_exact) * (REL_BUCKETS - max_exact)).astype(jnp.int32)
    large = jnp.minimum(large, REL_BUCKETS - 1)
    return jnp.where(n < max_exact, n, large)


def _bias_tiles_kernel(rb_ref, o_ref, *, t):
    h = pl.program_id(0)
    d = pl.program_id(1)
    r = lax.broadcasted_iota(jnp.int32, (t, t), 0)
    c = lax.broadcasted_iota(jnp.int32, (t, t), 1)
    dist = d * t + r - c
    bucket = _rel_bucket(dist)
    bias = jnp.zeros((t, t), F32)
    for b in range(REL_BUCKETS):
        bias = jnp.where(bucket == b, rb_ref[b, h], bias)
    o_ref[...] = jnp.where(dist >= 0, bias - rb_ref[REL_BUCKETS - 1, h], NEG_INF)


def _bias_tiles(rel_bias, t):
    assert t >= REL_MAX_DIST
    return pl.pallas_call(
        functools.partial(_bias_tiles_kernel, t=t),
        grid=(DA_HEADS, 2),
        in_specs=[pl.BlockSpec(memory_space=pltpu.SMEM)],
        out_specs=pl.BlockSpec((None, None, t, t), lambda h, d: (h, d, 0, 0)),
        out_shape=jax.ShapeDtypeStruct((DA_HEADS, 2, t, t), F32),
        compiler_params=_params("parallel", "parallel"),
        name="rel_bias_tiles",
    )(rel_bias)


def _lambda(lq1_ref, lk1_ref, lq2_ref, lk2_ref, lam_init):
    dot = lambda a, b: jnp.sum(a[...] * b[...], axis=-1, keepdims=True)
    return jnp.exp(dot(lq1_ref, lk1_ref)) - jnp.exp(dot(lq2_ref, lk2_ref)) + lam_init


def _da_prompt_kernel(q_ref, k_ref, v_ref, b_ref, lq1_ref, lk1_ref, lq2_ref, lk2_ref, sg_ref, o_ref,
                      kb_ref, vb_ref, s_ref, *, t, lam_init):
    seq = q_ref.shape[0]
    kb_ref[...] = k_ref[...].astype(BF16)
    vb_ref[:, :DA_VD] = v_ref[...].astype(BF16)
    vb_ref[:, DA_VD:] = jnp.ones((seq, LANES), BF16)
    lam = _lambda(lq1_ref, lk1_ref, lq2_ref, lk2_ref, lam_init)
    sg = sg_ref[...]
    lane = lax.broadcasted_iota(jnp.int32, (t, LANES), 1)
    stack = lambda x: jnp.concatenate([x, x], axis=0)
    for qi in range(seq // t):
        q = q_ref[qi * t:(qi + 1) * t, :] * (DA_DH ** -0.5)
        qq = jnp.concatenate([jnp.where(lane < DA_DH, q, 0.0), jnp.where(lane >= DA_DH, q, 0.0)], axis=0)
        n_k = (qi + 1) * t
        s_ref[:, :n_k] = _mm_nt(qq.astype(BF16), kb_ref[:n_k, :])
        s_ref[:, qi * t:n_k] += stack(b_ref[0])
        if qi >= 1:
            s_ref[:, (qi - 1) * t:qi * t] += stack(b_ref[1])
        s = s_ref[:, :n_k]
        p = jnp.exp(s - jnp.max(s, axis=-1, keepdims=True)).astype(BF16)
        acc = _mm(p, vb_ref[:n_k, :])
        a = acc[:, :DA_VD] / acc[:, DA_VD:]
        o = a[:t] - lam * a[t:]
        o_ref[qi * t:(qi + 1) * t, :] = _rms(o, sg) * (1.0 - lam_init)


def _da_prompt(qkv, bias_tiles, lq1, lk1, lq2, lk2, sub_gain, lam_init):
    b, seq, _ = qkv[0].shape
    t = bias_tiles.shape[-1]
    vec = lambda x: x.reshape(1, -1)
    vspec = lambda w: pl.BlockSpec((1, w), lambda bi, hi: (0, 0))
    head = pl.BlockSpec((None, seq, LANES), lambda bi, hi: (bi, 0, hi))
    return pl.pallas_call(
        functools.partial(_da_prompt_kernel, t=t, lam_init=lam_init),
        grid=(b, DA_HEADS),
        in_specs=[
            head, head, head,
            pl.BlockSpec((None, 2, t, t), lambda bi, hi: (hi, 0, 0, 0)),
            vspec(DA_DH), vspec(DA_DH), vspec(DA_DH), vspec(DA_DH), vspec(DA_VD),
        ],
        out_specs=pl.BlockSpec((None, seq, LANES), lambda bi, hi: (bi, 0, hi)),
        out_shape=jax.ShapeDtypeStruct((b, seq, DA_HEADS * DA_VD), F32),
        scratch_shapes=[pltpu.VMEM((seq, DA_DH * 2), BF16), pltpu.VMEM((seq, DA_VD + LANES), BF16),
                        pltpu.VMEM((2 * t, seq), F32)],
        compiler_params=_params("parallel", "parallel"),
        name="da_prompt",
    )(*qkv, bias_tiles, vec(lq1), vec(lk1), vec(lq2), vec(lk2), vec(sub_gain))


def _da_sample_kernel(pt_ref, q_ref, kn_ref, vn_ref, rbt_ref, lq1_ref, lk1_ref, lq2_ref, lk2_ref, sg_ref, *rest,
                      pages, n_pages, lq, lam_init):
    del pt_ref
    k_refs = rest[:pages]
    v_refs = rest[pages:2 * pages]
    o_ref, qbd_ref, m_ref, l_ref, acc_ref, s_ref = rest[2 * pages:]
    step = pl.program_id(1)
    rows = DA_HEADS * 2 * lq
    past_len = n_pages * PAGE_SIZE

    @pl.when(step == 0)
    def _():
        q = q_ref[...] * (DA_DH ** -0.5)
        qrep = jnp.concatenate([q] * (DA_HEADS * 2), axis=0)
        r_hc = lax.broadcasted_iota(jnp.int32, (rows, DA_HW), 0) // lq
        c_hc = lax.broadcasted_iota(jnp.int32, (rows, DA_HW), 1) // DA_DH
        qbd_ref[...] = jnp.where(r_hc == c_hc, qrep, 0.0).astype(BF16)
        m_ref[...] = jnp.full(m_ref.shape, NEG_INF, F32)
        l_ref[...] = jnp.zeros_like(l_ref)
        acc_ref[...] = jnp.zeros_like(acc_ref)

    def near_bias(k_start, n_valid):
        row = lax.broadcasted_iota(jnp.int32, (rows, PAGE_SIZE), 0)
        col = lax.broadcasted_iota(jnp.int32, (rows, PAGE_SIZE), 1)
        dist = past_len + row % lq - (k_start + col)
        bucket = _rel_bucket(dist)
        bias = jnp.zeros((rows, PAGE_SIZE), F32)
        for b in range(REL_BUCKETS):
            bias = jnp.where(bucket == b, rbt_ref[:, b:b + 1], bias)
        bias = bias - rbt_ref[:, REL_BUCKETS - 1:REL_BUCKETS]
        return jnp.where(jnp.logical_and(dist >= 0, col < n_valid), bias, NEG_INF)

    def attend(s_blocks, v_blocks):
        s_max = functools.reduce(jnp.maximum, s_blocks)
        m_prev = m_ref[...]
        m_new = jnp.maximum(m_prev, jnp.max(s_max, axis=-1, keepdims=True))
        alpha = jnp.exp(m_prev - m_new)
        p = [jnp.exp(s - m_new) for s in s_blocks]
        l_ref[...] = alpha * l_ref[...] + jnp.sum(functools.reduce(jnp.add, p), axis=-1, keepdims=True)
        pb = jnp.concatenate([x.astype(BF16) for x in p], axis=1)
        pv = [_mm(pb[h * 2 * lq:(h + 1) * 2 * lq], jnp.concatenate([v[h] for v in v_blocks], axis=0))
              for h in range(DA_HEADS)]
        acc_ref[...] = alpha * acc_ref[...] + jnp.concatenate(pv, axis=0)
        m_ref[...] = m_new

    for r in range(pages):
        s_ref[r] = _mm(qbd_ref[...], k_refs[r][...].astype(BF16))

    last = step == pl.num_programs(1) - 1

    @pl.when(last)
    def _():
        s_ref[pages - 1] += near_bias(past_len - PAGE_SIZE, PAGE_SIZE)

    attend([s_ref[r] for r in range(pages)],
           [[v_refs[r][pl.ds(h, PAGE_SIZE, stride=DA_HEADS), :].astype(BF16) for h in range(DA_HEADS)]
            for r in range(pages)])

    @pl.when(last)
    def _():
        zeros = jnp.zeros((PAGE_SIZE - lq, DA_HW), F32)
        kn = jnp.concatenate([kn_ref[...], zeros], axis=0).astype(BF16)
        vn = jnp.concatenate([vn_ref[...], zeros], axis=0).astype(BF16)
        attend([_mm_nt(qbd_ref[...], kn) + near_bias(past_len, lq)],
               [[vn[:, h * DA_VD:(h + 1) * DA_VD] for h in range(DA_HEADS)]])
        a = acc_ref[...] / l_ref[...]
        lam = _lambda(lq1_ref, lk1_ref, lq2_ref, lk2_ref, lam_init)
        sg = sg_ref[...]
        heads = []
        for h in range(DA_HEADS):
            o = a[h * 2 * lq:h * 2 * lq + lq] - lam * a[h * 2 * lq + lq:(h + 1) * 2 * lq]
            heads.append(_rms(o, sg) * (1.0 - lam_init))
        o_ref[...] = jnp.concatenate(heads, axis=-1)


def _da_sample(qkv, cache_k, cache_v, layer, page_table, rel_bias, lq1, lk1, lq2, lk2, sub_gain, lam_init):
    b, lq, _ = qkv[0].shape
    assert PAGE_SIZE >= REL_MAX_DIST
    n_pages = page_table.shape[1]
    pages = math.gcd(PAGES_PER_STEP, n_pages)
    rows = DA_HEADS * 2 * lq
    assert rows == LANES and lq == SUBLANES
    rbt = jnp.repeat(rel_bias.T, 2 * lq, axis=0)
    vec = lambda x: x.reshape(1, -1)
    vspec = lambda w: pl.BlockSpec((1, w), lambda bi, si, pt: (0, 0))
    new = pl.BlockSpec((None, lq, DA_HW), lambda bi, si, pt: (bi, 0, 0))
    page = lambda r, shape: pl.BlockSpec((None, None) + shape,
                                         lambda bi, si, pt: (layer, pt[bi, si * pages + r], 0, 0))
    k_shape, v_shape = cache_k.shape[2:], cache_v.shape[2:]
    grid_spec = pltpu.PrefetchScalarGridSpec(
        num_scalar_prefetch=1,
        grid=(b, n_pages // pages),
        in_specs=[new, new, new,
                  pl.BlockSpec((rows, REL_BUCKETS), lambda bi, si, pt: (0, 0)),
                  vspec(DA_DH), vspec(DA_DH), vspec(DA_DH), vspec(DA_DH), vspec(DA_VD)]
                 + [page(r, k_shape) for r in range(pages)] + [page(r, v_shape) for r in range(pages)],
        out_specs=pl.BlockSpec((None, lq, DA_HW), lambda bi, si, pt: (bi, 0, 0)),
        scratch_shapes=[pltpu.VMEM((rows, DA_HW), BF16), pltpu.VMEM((rows, LANES), F32),
                        pltpu.VMEM((rows, LANES), F32), pltpu.VMEM((rows, DA_VD), F32),
                        pltpu.VMEM((pages, rows, PAGE_SIZE), F32)],
    )
    return pl.pallas_call(
        functools.partial(_da_sample_kernel, pages=pages, n_pages=n_pages, lq=lq, lam_init=lam_init),
        grid_spec=grid_spec,
        out_shape=jax.ShapeDtypeStruct((b, lq, DA_HW), F32),
        compiler_params=_params("parallel", "arbitrary"),
        name="da_sample",
    )(page_table, *qkv, rbt, vec(lq1), vec(lk1), vec(lq2), vec(lk2), vec(sub_gain),
      *([cache_k] * pages), *([cache_v] * pages))


def _gdn_layer(y, conv_buf, s0, g_pre, w_main, w_ba, conv_w, a_log, dt_bias, o_gain):
    b, seq, d = y.shape
    yf = y.reshape(b * seq, d)
    *qkvz, ba = _norm_matmul(yf, g_pre, w_main, D_MODEL, w_narrow=w_ba)
    qkvz = [x.reshape(b, seq, D_MODEL) for x in qkvz]
    ba = ba.reshape(b, seq, LANES)
    cb = jnp.transpose(conv_buf.reshape(b, CONV_W - 1, 3, D_MODEL), (0, 2, 1, 3))
    o, s_new = _gdn_core(qkvz, ba, cb, conv_w, a_log, dt_bias, o_gain, s0)
    tail = jnp.concatenate([x[:, -(CONV_W - 1):] for x in qkvz[:3]], axis=-1)
    pre = jnp.concatenate([conv_buf, tail], axis=1)
    new_buf = pre[:, -(CONV_W - 1):]
    return o, new_buf, s_new


def _layer_rest(o, y, g_pre, g_post, w_out, w_q, w_o, w_gu, w_down, mem):
    b, seq, d = y.shape
    mem_k, mem_v, layer = mem
    if layer is None and seq % TAIL_ROW_TILE == 0:
        gains = jnp.stack([g_post[0], g_pre[1], g_post[1], g_pre[2], g_post[2]])
        return _layer_tail(o, y, gains, w_out, w_q, mem_k, mem_v, w_o, w_gu, w_down)
    if layer is None:
        mem_k, mem_v, layer = mem_k[None], mem_v[None], 0
    y = _matmul_postnorm(o.reshape(b * seq, -1), w_out, g_post[0], y.reshape(b * seq, d)).reshape(b, seq, d)
    y = _cross_attn(y, g_pre[1], w_q, mem_k, mem_v, layer, layer, w_o, g_post[1])
    return _ffn(y.reshape(b * seq, d), g_pre[2], w_gu, w_down, g_post[2]).reshape(b, seq, d)


def kernel(x_prompt, x_sample, state_gdn, state_conv, cache_k, cache_v, cache_mem_k, cache_mem_v, page_table,
           mem_prompt, rel_bias, norm_pre, norm_post, gdn_w_in, gdn_conv_w, gdn_a_log, gdn_dt_bias, gdn_o_gain,
           gdn_w_out, da_w_in, da_lq1, da_lk1, da_lq2, da_lk2, da_sub_gain, da_w_out, mem_gain, w_xq, w_xkv, w_xo,
           ffn_w_gu, ffn_w_down):
    depth = norm_pre.shape[0]
    b_p, l_p, d = x_prompt.shape
    b_s, l_s, _ = x_sample.shape
    m_len = mem_prompt.shape[1]
    yp, ys = x_prompt, x_sample
    mem_flat = mem_prompt.reshape(b_p * m_len, d)
    ck = jnp.transpose(cache_k, (0, 1, 3, 4, 5, 2)).reshape(cache_k.shape[:2] + (DA_HW, PAGE_SIZE))
    cv = cache_v.reshape(cache_v.shape[:2] + (PAGE_SIZE * DA_HEADS, DA_VD))
    mem_rows = lambda x: jnp.transpose(
        x.reshape(x.shape[:4] + (X_DH // LANES, LANES)), (0, 1, 2, 4, 3, 5)).reshape(x.shape[:2] + (-1, LANES))
    cmk, cmv = mem_rows(cache_mem_k), mem_rows(cache_mem_v)
    bias_tiles = _bias_tiles(rel_bias, min(ATTN_TILE, l_p))
    gdn_p, conv_p, gdn_s, conv_s = [], [], [], []
    k_p, v_p, k_s, v_s = [], [], [], []
    mk_p, mv_p = [], []
    for i in range(depth):
        j = i // N_MIXERS
        g_pre, g_post = norm_pre[i], norm_post[i]
        if i % N_MIXERS == 0:
            w_in = gdn_w_in[j]
            n_main = CONV_DIM + GDN_VW
            w_main = w_in[:, :n_main].astype(BF16)
            w_ba = jnp.pad(w_in[:, n_main:], ((0, 0), (0, LANES - 2 * GDN_HEADS))).astype(BF16)
            conv_w = jnp.transpose(gdn_conv_w[j].reshape(CONV_W, 3, D_MODEL), (1, 0, 2))
            w_out = gdn_w_out[j].astype(BF16)
            gw = (g_pre[0], w_main, w_ba, conv_w, gdn_a_log[j], gdn_dt_bias[j], gdn_o_gain[j])
            buf0 = jnp.zeros((b_p, CONV_W - 1, CONV_DIM), F32)
            s0 = jnp.zeros((b_p, GDN_HEADS, GDN_DK, GDN_DV), F32)
            op, cb, st = _gdn_layer(yp, buf0, s0, *gw)
            gdn_p.append(st)
            conv_p.append(cb)
            os_, cb, st = _gdn_layer(ys, state_conv[j], state_gdn[j], *gw)
            gdn_s.append(st)
            conv_s.append(cb)
        else:
            lam_init = 0.8 - 0.6 * math.exp(-0.3 * i)
            w_in = da_w_in[j].astype(BF16)
            w_out = da_w_out[j].astype(BF16)
            lw = (da_lq1[j], da_lk1[j], da_lq2[j], da_lk2[j], da_sub_gain[j], lam_init)
            qkv = [x.reshape(b_p, l_p, DA_HW) for x in _norm_matmul(yp.reshape(b_p * l_p, d), g_pre[0], w_in, DA_HW)]
            op = _da_prompt(qkv, bias_tiles, *lw)
            k_p.append(qkv[1].reshape(b_p, l_p, DA_HEADS, 2, DA_DH))
            v_p.append(qkv[2].reshape(b_p, l_p, DA_HEADS, DA_VD))
            qkv = [x.reshape(b_s, l_s, DA_HW) for x in _norm_matmul(ys.reshape(b_s * l_s, d), g_pre[0], w_in, DA_HW)]
            os_ = _da_sample(qkv, ck, cv, j, page_table, rel_bias, *lw)
            k_s.append(qkv[1].reshape(b_s, l_s, DA_HEADS, 2, DA_DH))
            v_s.append(qkv[2].reshape(b_s, l_s, DA_HEADS, DA_VD))
        mk, mv = _norm_matmul(mem_flat, mem_gain[i], w_xkv[i].astype(BF16), d)
        mk_p.append(mk.reshape(b_p, m_len, X_HEADS, X_DH))
        mv_p.append(mv.reshape(b_p, m_len, X_HEADS, X_DH))
        rest = (g_pre, g_post, w_out, w_xq[i].astype(BF16), w_xo[i].astype(BF16),
                ffn_w_gu[i].astype(BF16), ffn_w_down[i].astype(BF16))
        yp = _layer_rest(op, yp, *rest, (mk.reshape(b_p, m_len, d), mv.reshape(b_p, m_len, d), None))
        ys = _layer_rest(os_, ys, *rest, (cmk, cmv, i))
    return (yp, ys,
            jnp.stack(gdn_p), jnp.stack(conv_p), jnp.stack(k_p), jnp.stack(v_p),
            jnp.stack(mk_p), jnp.stack(mv_p),
            jnp.stack(gdn_s), jnp.stack(conv_s), jnp.stack(k_s), jnp.stack(v_s))
```

```python
import functools
import math

import jax
import jax.numpy as jnp
from jax import lax
from jax.experimental import pallas as pl
from jax.experimental.pallas import tpu as pltpu

F32 = jnp.float32
BF16 = jnp.bfloat16

D_MODEL = 1024
N_MIXERS = 2
GDN_HEADS = 8
GDN_DK = 128
GDN_DV = 128
CONV_W = 4
GDN_CHUNK = 64
INV_BLOCK = 8
GDN_GROUP = 8
GDN_MIN_HEADS = 2
SCAN_UNROLL = 4
GDN_QK = GDN_HEADS * GDN_DK
GDN_VW = GDN_HEADS * GDN_DV
CONV_DIM = 2 * GDN_QK + GDN_VW
DA_HEADS = 8
DA_DH = 64
DA_VD = 2 * DA_DH
DA_HW = DA_HEADS * 2 * DA_DH
PAGE_SIZE = 128
REL_BUCKETS = 32
REL_MAX_DIST = 128
X_HEADS = 4
X_DH = D_MODEL // X_HEADS
D_FF = -(-8 * D_MODEL // (3 * 256)) * 256
RMS_EPS = 1e-6
NEG_INF = -1e30

LANES = 128
SUBLANES = 8
ROW_TILE = 512
TAIL_ROW_TILE = 512
MXU_TILE = 256
TAIL_FFN_BOUNDS = (0, -(-D_FF // (2 * MXU_TILE)) * MXU_TILE, D_FF)
ATTN_TILE = 256
PAGES_PER_STEP = 8
VMEM_LIMIT = 56 * 1024 * 1024

_NT = (((1,), (1,)), ((), ()))


def _params(*sem):
    return pltpu.CompilerParams(dimension_semantics=sem, vmem_limit_bytes=VMEM_LIMIT)


def _row_tile(t, rows=ROW_TILE):
    return rows if t % rows == 0 else t


def _rms(x, g):
    return x * lax.rsqrt(jnp.mean(x * x, axis=-1, keepdims=True) + RMS_EPS) * g


def _silu(x):
    return x * jax.nn.sigmoid(x)


def _mm(a, b):
    return jnp.dot(a, b, preferred_element_type=F32)


def _mm_nt(a, b):
    return lax.dot_general(a, b, _NT, preferred_element_type=F32)


def _split(x):
    hi = x.astype(BF16)
    return hi, (x - hi.astype(F32)).astype(BF16)


def _mmb(a, b):
    return _mm(a.astype(BF16), b.astype(BF16))


def _norm_matmul_kernel(x_ref, g_ref, *rest, n_w):
    w_refs, o_refs = rest[:n_w], rest[n_w:]
    h = _rms(x_ref[...], g_ref[...]).astype(BF16)
    tn = o_refs[0].shape[1]
    for j, o_ref in enumerate(o_refs[:-1] if n_w == 2 else o_refs):
        o_ref[...] = _mm(h, w_refs[0][:, j * tn:(j + 1) * tn])
    if n_w == 2:
        o_refs[-1][...] = _mm(h, w_refs[1][...])


def _norm_matmul(x, g, w, tn, w_narrow=None):
    t, d = x.shape
    n = w.shape[1]
    tm = _row_tile(t)
    resident = lambda a: pl.BlockSpec(a.shape, lambda i: (0, 0), pipeline_mode=pl.Buffered(1))
    weights = [w] if w_narrow is None else [w, w_narrow]
    widths = [tn] * (n // tn) + ([] if w_narrow is None else [LANES])
    return pl.pallas_call(
        functools.partial(_norm_matmul_kernel, n_w=len(weights)),
        grid=(t // tm,),
        in_specs=[pl.BlockSpec((tm, d), lambda i: (i, 0)), pl.BlockSpec((1, d), lambda i: (0, 0))]
                 + [resident(a) for a in weights],
        out_specs=[pl.BlockSpec((tm, width), lambda i: (i, 0)) for width in widths],
        out_shape=[jax.ShapeDtypeStruct((t, width), F32) for width in widths],
        compiler_params=_params("parallel"),
        name="norm_matmul",
    )(x, g.reshape(1, d), *weights)


def _matmul_postnorm_kernel(a_ref, w_ref, g_ref, r_ref, o_ref):
    y = _mm(a_ref[...].astype(BF16), w_ref[...])
    o_ref[...] = r_ref[...] + _rms(y, g_ref[...])


def _matmul_postnorm(a, w, g, res):
    t, k = a.shape
    d = w.shape[1]
    tm = _row_tile(t)
    return pl.pallas_call(
        _matmul_postnorm_kernel,
        grid=(t // tm,),
        in_specs=[
            pl.BlockSpec((tm, k), lambda i: (i, 0)),
            pl.BlockSpec((k, d), lambda i: (0, 0)),
            pl.BlockSpec((1, d), lambda i: (0, 0)),
            pl.BlockSpec((tm, d), lambda i: (i, 0)),
        ],
        out_specs=pl.BlockSpec((tm, d), lambda i: (i, 0)),
        out_shape=jax.ShapeDtypeStruct((t, d), F32),
        compiler_params=_params("parallel"),
        name="matmul_postnorm",
    )(a, w, g.reshape(1, d), res)


def _ffn_rows(x, g_pre, wg_ref, wu_ref, wd_ref, g_post, bounds=(0, D_FF)):
    h = _rms(x, g_pre).astype(BF16)
    y = None
    for lo, hi in zip(bounds[:-1], bounds[1:]):
        cols = slice(lo, hi)
        a = (_silu(_mm(h, wg_ref[:, cols])) * _mm(h, wu_ref[:, cols])).astype(BF16)
        part = _mm(a, wd_ref[cols, :])
        y = part if y is None else y + part
    return x + _rms(y, g_post)


def _ffn_kernel(x_ref, gpre_ref, wg_ref, wu_ref, wd_ref, gpost_ref, o_ref):
    o_ref[...] = _ffn_rows(x_ref[...], gpre_ref[...], wg_ref, wu_ref, wd_ref, gpost_ref[...])


def _ffn(x, g_pre, w_gu, w_down, g_post):
    t, d = x.shape
    tm = _row_tile(t)
    resident = lambda shape, col: pl.BlockSpec(shape, lambda i: (0, col), pipeline_mode=pl.Buffered(1))
    return pl.pallas_call(
        _ffn_kernel,
        grid=(t // tm,),
        in_specs=[
            pl.BlockSpec((tm, d), lambda i: (i, 0)),
            pl.BlockSpec((1, d), lambda i: (0, 0)),
            resident((d, D_FF), 0),
            resident((d, D_FF), 1),
            resident((D_FF, d), 0),
            pl.BlockSpec((1, d), lambda i: (0, 0)),
        ],
        out_specs=pl.BlockSpec((tm, d), lambda i: (i, 0)),
        out_shape=jax.ShapeDtypeStruct((t, d), F32),
        compiler_params=_params("parallel"),
        name="ffn",
    )(x, g_pre.reshape(1, d), w_gu, w_gu, w_down, g_post.reshape(1, d))


def _cross_attn_rows(y, g_pre, wq_ref, mk, mv, wo_ref, g_post):
    h = _rms(y, g_pre).astype(BF16)
    q = (_mm(h, wq_ref[...]) * (X_DH ** -0.5)).astype(BF16)
    heads = []
    for hd in range(X_HEADS):
        sl = slice(hd * X_DH, (hd + 1) * X_DH)
        s = _mm_nt(q[:, sl], mk[:, sl])
        p = jnp.exp(s - jnp.max(s, axis=-1, keepdims=True))
        heads.append(_mm(p.astype(BF16), mv[:, sl]) / jnp.sum(p, axis=-1, keepdims=True))
    o = jnp.concatenate(heads, axis=-1).astype(BF16)
    return y + _rms(_mm(o, wo_ref[...]), g_post)


def _cross_attn_kernel(y_ref, gpre_ref, wq_ref, mk_ref, mv_ref, wo_ref, gpost_ref, o_ref, *, tiled_mem):
    if tiled_mem:
        tiles = X_DH // LANES
        m = mk_ref.shape[0] // (tiles * X_HEADS)
        gather = lambda ref: jnp.concatenate(
            [ref[pl.ds(lt * X_HEADS + hd, m, stride=tiles * X_HEADS), :]
             for hd in range(X_HEADS) for lt in range(tiles)], axis=1).astype(BF16)
        mk, mv = gather(mk_ref), gather(mv_ref)
    else:
        mk = mk_ref[...].astype(BF16)
        mv = mv_ref[...].astype(BF16)
    o_ref[...] = _cross_attn_rows(y_ref[...], gpre_ref[...], wq_ref, mk, mv, wo_ref, gpost_ref[...])


def _cross_attn(y, g_pre, w_q, mem_k, mem_v, layer_k, layer_v, w_o, g_post):
    b, l, d = y.shape
    mem_block = mem_k.shape[2:]
    tm = _row_tile(l)
    return pl.pallas_call(
        functools.partial(_cross_attn_kernel, tiled_mem=mem_block[1] != d),
        grid=(b, l // tm),
        in_specs=[
            pl.BlockSpec((None, tm, d), lambda bi, i: (bi, i, 0)),
            pl.BlockSpec((1, d), lambda bi, i: (0, 0)),
            pl.BlockSpec((d, d), lambda bi, i: (0, 0)),
            pl.BlockSpec((None, None) + mem_block, lambda bi, i: (layer_k, bi, 0, 0)),
            pl.BlockSpec((None, None) + mem_block, lambda bi, i: (layer_v, bi, 0, 0)),
            pl.BlockSpec((d, d), lambda bi, i: (0, 0)),
            pl.BlockSpec((1, d), lambda bi, i: (0, 0)),
        ],
        out_specs=pl.BlockSpec((None, tm, d), lambda bi, i: (bi, i, 0)),
        out_shape=jax.ShapeDtypeStruct((b, l, d), F32),
        compiler_params=_params("parallel", "parallel"),
        name="cross_attn",
    )(y, g_pre.reshape(1, d), w_q, mem_k, mem_v, w_o, g_post.reshape(1, d))


def _layer_tail_kernel(a_ref, y_ref, gains_ref, wout_ref, wq_ref, mk_ref, mv_ref, wo_ref, wg_ref, wu_ref, wd_ref,
                       o_ref):
    gain = lambda r: gains_ref[r:r + 1, :]
    y = y_ref[...] + _rms(_mm(a_ref[...].astype(BF16), wout_ref[...]), gain(0))
    y = _cross_attn_rows(y, gain(1), wq_ref, mk_ref[...].astype(BF16), mv_ref[...].astype(BF16), wo_ref, gain(2))
    o_ref[...] = _ffn_rows(y, gain(3), wg_ref, wu_ref, wd_ref, gain(4), bounds=TAIL_FFN_BOUNDS)


def _layer_tail(a, y, gains, w_out, w_q, mem_k, mem_v, w_o, w_gu, w_down):
    b, l, d = y.shape
    k = a.shape[2]
    m = mem_k.shape[1]
    tm = _row_tile(l, TAIL_ROW_TILE)
    rows = lambda width: pl.BlockSpec((None, tm, width), lambda bi, i: (bi, i, 0))
    resident = lambda shape, col=0: pl.BlockSpec(shape, lambda bi, i: (0, col), pipeline_mode=pl.Buffered(1))
    mem = pl.BlockSpec((None, m, d), lambda bi, i: (bi, 0, 0))
    return pl.pallas_call(
        _layer_tail_kernel,
        grid=(b, l // tm),
        in_specs=[
            rows(k), rows(d), resident(gains.shape),
            resident((k, d)), resident((d, d)), mem, mem, resident((d, d)),
            resident((d, D_FF), 0), resident((d, D_FF), 1), resident((D_FF, d)),
        ],
        out_specs=rows(d),
        out_shape=jax.ShapeDtypeStruct((b, l, d), F32),
        compiler_params=_params("parallel", "parallel"),
        name="layer_tail",
    )(a, y, gains, w_out, w_q, mem_k, mem_v, w_o, w_gu, w_gu, w_down)


def _gdn_kernel(alog_ref, dtb_ref, q_ref, k_ref, v_ref, z_ref, ba_ref, cb_ref, cw_ref, og_ref, s0_ref,
                o_ref, sn_ref,
                xq_ref, xk_ref, xv_ref, slot0_ref, slot1_ref, sm_ref, sb_ref, oq_ref, ob_ref, gl_ref,
                *, seq, chunk, n_chunks, group, hb):
    c_ = chunk
    padded = n_chunks * c_
    rows = min(seq, c_)
    width = hb * LANES
    heads = [pl.program_id(1) * hb + hh for hh in range(hb)]
    cols = lambda hh: slice(hh * LANES, (hh + 1) * LANES)

    for part, (x_ref, src) in enumerate(((xq_ref, q_ref), (xk_ref, k_ref), (xv_ref, v_ref))):
        x_ref[0:SUBLANES, :] = jnp.zeros((SUBLANES, width), F32)
        x_ref[SUBLANES - (CONV_W - 1):SUBLANES, :] = cb_ref[part]
        x_ref[pl.ds(SUBLANES, seq), :] = src[...]
        if padded > seq:
            x_ref[pl.ds(SUBLANES + seq, padded - seq), :] = jnp.zeros((padded - seq, width), F32)

    alog = [jnp.full((1, 1), alog_ref[h], F32) for h in heads]
    dtb = [jnp.full((1, 1), dtb_ref[h], F32) for h in heads]
    r_i =lax.broadcasted_iota(jnp.int32, (c_, c_), 0)
    c_i = lax.broadcasted_iota(jnp.int32, (c_, c_), 1)
    incl = r_i >= c_i
    strict = r_i > c_i
    tril = jnp.where(incl, 1.0, 0.0).astype(BF16)
    eye = jnp.where(r_i == c_i, 1.0, 0.0).astype(F32)
    blk = lambda size: (r_i // size) == (c_i // size)
    diag_blocks = blk(INV_BLOCK)
    merges = [jnp.logical_and(blk(2 * size), jnp.logical_not(blk(size)))
              for size in (INV_BLOCK << i for i in range(int(math.log2(c_ // INV_BLOCK))))]
    lane = lax.broadcasted_iota(jnp.int32, (c_, LANES), 1)

    def conv(win, part, hh):
        w = cw_ref[part][:, cols(hh)]
        base = SUBLANES - (CONV_W - 1)
        y = win[base:base + c_] * w[0:1]
        for j in range(1, CONV_W):
            y = y + win[base + j:base + j + c_] * w[j:j + 1]
        return _silu(y)

    def pad_rows(x):
        if rows == c_:
            return x
        return jnp.concatenate([x, jnp.zeros((c_ - rows, x.shape[1]), x.dtype)], axis=0)

    def rows_of(c):
        return c * c_ if isinstance(c, int) else pl.multiple_of(c * c_, c_)

    def gates(hh, r0, wins):
        h = heads[hh]
        qc = conv(wins[0], 0, hh)
        kc = conv(wins[1], 1, hh)
        v = conv(wins[2], 2, hh)
        q = qc * lax.rsqrt(jnp.sum(qc * qc, axis=-1, keepdims=True) + RMS_EPS) * (GDN_DK ** -0.5)
        k = kc * lax.rsqrt(jnp.sum(kc * kc, axis=-1, keepdims=True) + RMS_EPS)
        ba = pad_rows(ba_ref[pl.ds(r0, rows), :])
        b_raw = jnp.sum(jnp.where(lane == h, ba, 0.0), axis=-1, keepdims=True)
        a_raw = jnp.sum(jnp.where(lane == h + GDN_HEADS, ba, 0.0), axis=-1, keepdims=True)
        beta = jax.nn.sigmoid(b_raw)
        x = a_raw + dtb[hh]
        g = -jnp.exp(alog[hh]) * (jnp.maximum(x, 0.0) + jnp.log1p(jnp.exp(-jnp.abs(x))))
        if padded > seq:
            valid = (lax.broadcasted_iota(jnp.int32, (c_, 1), 0) + r0) < seq
            k = jnp.where(valid, k, 0.0)
            beta = jnp.where(valid, beta, 0.0)
            g = jnp.where(valid, g, 0.0)
        return q, k, v, jnp.broadcast_to(beta, (c_, LANES)), jnp.broadcast_to(g, (c_, LANES))

    def gates_group(gi, slot_ref):
        for i in range(group):
            r0 = rows_of(gi * group + i)
            for hh in range(hb):
                wins = [ref[pl.ds(r0, c_ + SUBLANES), cols(hh)] for ref in (xq_ref, xk_ref, xv_ref)]
                for n, x in enumerate(gates(hh, r0, wins)):
                    slot_ref[n, i * c_:(i + 1) * c_, cols(hh)] = x

    def prep_group(gi, slot_ref):
        hhs = [hh for hh in range(hb) for _ in range(group)]
        local = [i for _ in range(hb) for i in range(group)]
        cs = [gi * group + i for i in local]
        r0s = [rows_of(c) for c in cs]
        every = lambda f, *xs: [f(*args) for args in zip(*xs)]
        slot = lambda n: [lambda hh=hh, i=i: slot_ref[n, i * c_:(i + 1) * c_, cols(hh)] for hh, i in zip(hhs, local)]
        q, k, v, beta = slot(0), slot(1), slot(2), slot(3)
        g_hi, g_lo = zip(*every(lambda get: _split(get()), slot(4)))
        gc = every(lambda hi, lo: _mm(tril, hi) + _mm(tril, lo), g_hi, g_lo)
        kb = every(lambda get: get().astype(BF16), k)
        kk = every(_mm_nt, kb, kb)
        qk = every(lambda get, y: _mm_nt(get().astype(BF16), y), q, kb)
        decay = every(lambda x: jnp.where(incl, jnp.exp(jnp.where(incl, x[:, :c_] - x.T[:c_], 0.0)), 0.0), gc)
        low = every(lambda b_, d_, kk_: jnp.where(strict, b_()[:, :c_] * d_ * kk_, 0.0), beta, decay, kk)
        pw = every(lambda x: -jnp.where(diag_blocks, x, 0.0), low)
        t_inv = every(lambda x: eye + x, pw)
        for _ in range(int(math.log2(INV_BLOCK)) - 1):
            pw = every(_mmb, pw, pw)
            t_inv = every(lambda t_, p_: t_ + _mmb(t_, p_), t_inv, pw)
        for merge in merges:
            off = every(lambda x, t_: _mmb(jnp.where(merge, x, 0.0), t_), low, t_inv)
            t_inv = every(lambda t_, x: t_ - _mmb(t_, x), t_inv, off)
        egc = every(jnp.exp, gc)
        rhs = every(lambda b_, e_, k_, v_: jnp.concatenate([(b_() * e_) * k_(), b_() * v_()], axis=1),
                    beta, egc, k, v)
        w = every(lambda t_, x: _mmb(t_, x).astype(BF16), t_inv, rhs)
        gc_last = every(lambda x: x[c_ - 1:c_, :], gc)
        kdt = every(lambda k_, x, l_: (k_() * jnp.exp(l_ - x)).T.astype(BF16), k, gc, gc_last)
        a = every(lambda x, d_: (x * d_).astype(BF16), qk, decay)
        kw = every(_mm, kdt, w)
        aw = every(_mm, a, w)
        for i, (hh, c, r0) in enumerate(zip(hhs, cs, r0s)):
            sm_ref[hh * n_chunks + c] = kw[i][:, :GDN_DK].astype(BF16)
            sb_ref[hh * n_chunks + c] = kw[i][:, GDN_DK:]
            oq_ref[pl.ds(r0, c_), cols(hh)] = (q[i]() * egc[i] - aw[i][:, :GDN_DK]).astype(BF16)
            ob_ref[pl.ds(r0, c_), cols(hh)] = aw[i][:, GDN_DK:]
            gl_ref[pl.ds(hh * n_chunks + c, 1), :] = jnp.exp(gc_last[i])

    og = og_ref[...]

    def scan(c, states):
        r0 = rows_of(c)
        at = [hh * n_chunks + c for hh in range(hb)]
        sb = [s.astype(BF16) for s in states]
        upd = [_mm(sm_ref[i], x) for i, x in zip(at, sb)]
        o = [_mm(oq_ref[pl.ds(r0, c_), cols(hh)], sb[hh]) + ob_ref[pl.ds(r0, c_), cols(hh)] for hh in range(hb)]
        for hh in range(hb):
            z = z_ref[pl.ds(r0, rows), cols(hh)]
            o_ref[pl.ds(r0, rows), cols(hh)] = _rms(o[hh][:rows], og) * _silu(z)
        return tuple(gl_ref[pl.ds(i, 1), :] * s - u + sb_ref[i] for i, s, u in zip(at, states, upd))

    def pipelined(g2, carry):
        prep_group(2 * g2, slot0_ref)
        gates_group(2 * g2 + 1, slot1_ref)
        prep_group(2 * g2 + 1, slot1_ref)
        gates_group(2 * g2 + 2, slot0_ref)
        return carry

    n_groups = n_chunks // group
    pairs = (n_groups - 1) // 2
    gates_group(0, slot0_ref)
    lax.fori_loop(0, pairs, pipelined, 0)
    prep_group(2 * pairs, slot0_ref)
    if n_groups - 2 * pairs == 2:
        gates_group(2 * pairs + 1, slot1_ref)
        prep_group(2 * pairs + 1, slot1_ref)
    final = lax.fori_loop(0, n_chunks, scan, tuple(s0_ref[hh] for hh in range(hb)),
                          unroll=math.gcd(SCAN_UNROLL, n_chunks))
    for hh in range(hb):
        sn_ref[hh] = final[hh]


def _gdn_core(qkvz, ba, conv_buf, conv_w, a_log, dt_bias, o_gain, s0):
    b, seq, _ = qkvz[0].shape
    chunk = GDN_CHUNK
    n_chunks = -(-seq // chunk)
    padded = n_chunks * chunk
    group = math.gcd(GDN_GROUP // GDN_MIN_HEADS, n_chunks)
    hb = math.gcd(GDN_GROUP // group, GDN_HEADS)
    width = hb * LANES
    kern = functools.partial(_gdn_kernel, seq=seq, chunk=chunk, n_chunks=n_chunks, group=group, hb=hb)
    head_cols = pl.BlockSpec((None, seq, width), lambda bi, hi: (bi, 0, hi))
    smem = pl.BlockSpec(memory_space=pltpu.SMEM)
    return pl.pallas_call(
        kern,
        grid=(b, GDN_HEADS // hb),
        in_specs=[
            smem, smem,
            head_cols, head_cols, head_cols, head_cols,
            pl.BlockSpec((None, seq, LANES), lambda bi, hi: (bi, 0, 0)),
            pl.BlockSpec((None, 3, CONV_W - 1, width), lambda bi, hi: (bi, 0, 0, hi)),
            pl.BlockSpec((3, CONV_W, width), lambda bi, hi: (0, 0, hi)),
            pl.BlockSpec((1, GDN_DV), lambda bi, hi: (0, 0)),
            pl.BlockSpec((None, hb, GDN_DK, GDN_DV), lambda bi, hi: (bi, hi, 0, 0)),
        ],
        out_specs=[
            pl.BlockSpec((None, seq, width), lambda bi, hi: (bi, 0, hi)),
            pl.BlockSpec((None, hb, GDN_DK, GDN_DV), lambda bi, hi: (bi, hi, 0, 0)),
        ],
        out_shape=[
            jax.ShapeDtypeStruct((b, seq, GDN_VW), F32),
            jax.ShapeDtypeStruct((b, GDN_HEADS, GDN_DK, GDN_DV), F32),
        ],
        scratch_shapes=[
            pltpu.VMEM((padded + SUBLANES, width), F32),
            pltpu.VMEM((padded + SUBLANES, width), F32),
            pltpu.VMEM((padded + SUBLANES, width), F32),
            pltpu.VMEM((5, group * chunk, width), F32),
            pltpu.VMEM((5, group * chunk, width), F32),
            pltpu.VMEM((hb * n_chunks, GDN_DK, GDN_DK), BF16),
            pltpu.VMEM((hb * n_chunks, GDN_DK, GDN_DV), F32),
            pltpu.VMEM((padded, width), BF16),
            pltpu.VMEM((padded, width), F32),
            pltpu.VMEM((hb * n_chunks, LANES), F32),
        ],
        compiler_params=_params("parallel", "parallel"),
        name="gdn_core",
    )(a_log, dt_bias, *qkvz, ba, conv_buf, conv_w, o_gain.reshape(1, GDN_DV), s0)


def _rel_bucket(dist):
    max_exact = REL_BUCKETS // 2
    n = jnp.maximum(dist, 0)
    large = max_exact + (jnp.log(jnp.maximum(n, 1).astype(F32) / max_exact)
                         / math.log(REL_MAX_DIST / max_exact) * (REL_BUCKETS - max_exact)).astype(jnp.int32)
    large = jnp.minimum(large, REL_BUCKETS - 1)
    return jnp.where(n < max_exact, n, large)


def _bias_tiles_kernel(rb_ref, o_ref, *, t):
    h = pl.program_id(0)
    d = pl.program_id(1)
    r = lax.broadcasted_iota(jnp.int32, (t, t), 0)
    c = lax.broadcasted_iota(jnp.int32, (t, t), 1)
    dist = d * t + r - c
    bucket = _rel_bucket(dist)
    bias = jnp.zeros((t, t), F32)
    for b in range(REL_BUCKETS):
        bias = jnp.where(bucket == b, rb_ref[b, h], bias)
    o_ref[...] = jnp.where(dist >= 0, bias - rb_ref[REL_BUCKETS - 1, h], NEG_INF)


def _bias_tiles(rel_bias, t):
    assert t >= REL_MAX_DIST
    return pl.pallas_call(
        functools.partial(_bias_tiles_kernel, t=t),
        grid=(DA_HEADS, 2),
        in_specs=[pl.BlockSpec(memory_space=pltpu.SMEM)],
        out_specs=pl.BlockSpec((None, None, t, t), lambda h, d: (h, d, 0, 0)),
        out_shape=jax.ShapeDtypeStruct((DA_HEADS, 2, t, t), F32),
        compiler_params=_params("parallel", "parallel"),
        name="rel_bias_tiles",
    )(rel_bias)


def _lambda(lq1_ref, lk1_ref, lq2_ref, lk2_ref, lam_init):
    dot = lambda a, b: jnp.sum(a[...] * b[...], axis=-1, keepdims=True)
    return jnp.exp(dot(lq1_ref, lk1_ref)) - jnp.exp(dot(lq2_ref, lk2_ref)) + lam_init


def _da_prompt_kernel(q_ref, k_ref, v_ref, b_ref, lq1_ref, lk1_ref, lq2_ref, lk2_ref, sg_ref, o_ref,
                      kb_ref, vb_ref, s_ref, *, t, lam_init):
    seq = q_ref.shape[0]
    kb_ref[...] = k_ref[...].astype(BF16)
    vb_ref[:, :DA_VD] = v_ref[...].astype(BF16)
    vb_ref[:, DA_VD:] = jnp.ones((seq, LANES), BF16)
    lam = _lambda(lq1_ref, lk1_ref, lq2_ref, lk2_ref, lam_init)
    sg = sg_ref[...]
    lane = lax.broadcasted_iota(jnp.int32, (t, LANES), 1)
    stack = lambda x: jnp.concatenate([x, x], axis=0)
    for qi in range(seq // t):
        q = q_ref[qi * t:(qi + 1) * t, :] * (DA_DH ** -0.5)
        qq = jnp.concatenate([jnp.where(lane < DA_DH, q, 0.0), jnp.where(lane >= DA_DH, q, 0.0)], axis=0)
        n_k = (qi + 1) * t
        s_ref[:, :n_k] = _mm_nt(qq.astype(BF16), kb_ref[:n_k, :])
        s_ref[:, qi * t:n_k] += stack(b_ref[0])
        if qi >= 1:
            s_ref[:, (qi - 1) * t:qi * t] += stack(b_ref[1])
        s = s_ref[:, :n_k]
        p = jnp.exp(s - jnp.max(s, axis=-1, keepdims=True)).astype(BF16)
        acc = _mm(p, vb_ref[:n_k, :])
        a = acc[:, :DA_VD] / acc[:, DA_VD:]
        o = a[:t] - lam * a[t:]
        o_ref[qi * t:(qi + 1) * t, :] = _rms(o, sg) * (1.0 - lam_init)


def _da_prompt(qkv, bias_tiles, lq1, lk1, lq2, lk2, sub_gain, lam_init):
    b, seq, _ = qkv[0].shape
    t = bias_tiles.shape[-1]
    vec = lambda x: x.reshape(1, -1)
    vspec = lambda w: pl.BlockSpec((1, w), lambda bi, hi: (0, 0))
    head = pl.BlockSpec((None, seq, LANES), lambda bi, hi: (bi, 0, hi))
    return pl.pallas_call(
        functools.partial(_da_prompt_kernel, t=t, lam_init=lam_init),
        grid=(b, DA_HEADS),
        in_specs=[
            head, head, head,
            pl.BlockSpec((None, 2, t, t), lambda bi, hi: (hi, 0, 0, 0)),
            vspec(DA_DH), vspec(DA_DH), vspec(DA_DH), vspec(DA_DH), vspec(DA_VD),
        ],
        out_specs=pl.BlockSpec((None, seq, LANES), lambda bi, hi: (bi, 0, hi)),
        out_shape=jax.ShapeDtypeStruct((b, seq, DA_HEADS * DA_VD), F32),
        scratch_shapes=[pltpu.VMEM((seq, DA_DH * 2), BF16), pltpu.VMEM((seq, DA_VD + LANES), BF16),
                        pltpu.VMEM((2 * t, seq), F32)],
        compiler_params=_params("parallel", "parallel"),
        name="da_prompt",
    )(*qkv, bias_tiles, vec(lq1), vec(lk1), vec(lq2), vec(lk2), vec(sub_gain))


def _da_sample_kernel(pt_ref, q_ref, kn_ref, vn_ref, rbt_ref, lq1_ref, lk1_ref, lq2_ref, lk2_ref, sg_ref, *rest,
                      pages, n_pages, lq, lam_init):
    del pt_ref
    k_refs = rest[:pages]
    v_refs = rest[pages:2 * pages]
    o_ref, qbd_ref, m_ref, l_ref, acc_ref, s_ref = rest[2 * pages:]
    step = pl.program_id(1)
    rows = DA_HEADS * 2 * lq
    past_len = n_pages * PAGE_SIZE

    @pl.when(step == 0)
    def _():
        q = q_ref[...] * (DA_DH ** -0.5)
        qrep = jnp.concatenate([q] * (DA_HEADS * 2), axis=0)
        r_hc = lax.broadcasted_iota(jnp.int32, (rows, DA_HW), 0) // lq
        c_hc = lax.broadcasted_iota(jnp.int32, (rows, DA_HW), 1) // DA_DH
        qbd_ref[...] = jnp.where(r_hc == c_hc, qrep, 0.0).astype(BF16)
        m_ref[...] = jnp.full(m_ref.shape, NEG_INF, F32)
        l_ref[...] = jnp.zeros_like(l_ref)
        acc_ref[...] = jnp.zeros_like(acc_ref)

    def near_bias(k_start, n_valid):
        row = lax.broadcasted_iota(jnp.int32, (rows, PAGE_SIZE), 0)
        col = lax.broadcasted_iota(jnp.int32, (rows, PAGE_SIZE), 1)
        dist = past_len + row % lq - (k_start + col)
        bucket = _rel_bucket(dist)
        bias = jnp.zeros((rows, PAGE_SIZE), F32)
        for b in range(REL_BUCKETS):
            bias = jnp.where(bucket == b, rbt_ref[:, b:b + 1], bias)
        bias = bias - rbt_ref[:, REL_BUCKETS - 1:REL_BUCKETS]
        return jnp.where(jnp.logical_and(dist >= 0, col < n_valid), bias, NEG_INF)

    def attend(s_blocks, v_blocks):
        s_max = functools.reduce(jnp.maximum, s_blocks)
        m_prev = m_ref[...]
        m_new = jnp.maximum(m_prev, jnp.max(s_max, axis=-1, keepdims=True))
        alpha = jnp.exp(m_prev - m_new)
        p = [jnp.exp(s - m_new) for s in s_blocks]
        l_ref[...] = alpha * l_ref[...] + jnp.sum(functools.reduce(jnp.add, p), axis=-1, keepdims=True)
        pb = jnp.concatenate([x.astype(BF16) for x in p], axis=1)
        pv = [_mm(pb[h * 2 * lq:(h + 1) * 2 * lq], jnp.concatenate([v[h] for v in v_blocks], axis=0))
              for h in range(DA_HEADS)]
        acc_ref[...] = alpha * acc_ref[...] + jnp.concatenate(pv, axis=0)
        m_ref[...] = m_new

    n_tiles = DA_HW // MXU_TILE
    tile_rows = rows // n_tiles
    q_tiles = [qbd_ref[j * tile_rows:(j + 1) * tile_rows, j * MXU_TILE:(j + 1) * MXU_TILE] for j in range(n_tiles)]
    for r in range(pages):
        kt = k_refs[r][...].astype(BF16)
        s_ref[r] = jnp.concatenate(
            [_mm(q_tiles[j], kt[j * MXU_TILE:(j + 1) * MXU_TILE, :]) for j in range(n_tiles)], axis=0)

    last = step == pl.num_programs(1) - 1

    @pl.when(last)
    def _():
        s_ref[pages - 1] += near_bias(past_len - PAGE_SIZE, PAGE_SIZE)

    attend([s_ref[r] for r in range(pages)],
           [[v_refs[r][pl.ds(h, PAGE_SIZE, stride=DA_HEADS), :].astype(BF16) for h in range(DA_HEADS)]
            for r in range(pages)])

    @pl.when(last)
    def _():
        zeros = jnp.zeros((PAGE_SIZE - lq, DA_HW), F32)
        kn = jnp.concatenate([kn_ref[...], zeros], axis=0).astype(BF16)
        vn = jnp.concatenate([vn_ref[...], zeros], axis=0).astype(BF16)
        attend([_mm_nt(qbd_ref[...], kn) + near_bias(past_len, lq)],
               [[vn[:, h * DA_VD:(h + 1) * DA_VD] for h in range(DA_HEADS)]])
        a = acc_ref[...] / l_ref[...]
        lam = _lambda(lq1_ref, lk1_ref, lq2_ref, lk2_ref, lam_init)
        sg = sg_ref[...]
        heads = []
        for h in range(DA_HEADS):
            o = a[h * 2 * lq:h * 2 * lq + lq] - lam * a[h * 2 * lq + lq:(h + 1) * 2 * lq]
            heads.append(_rms(o, sg) * (1.0 - lam_init))
        o_ref[...] = jnp.concatenate(heads, axis=-1)


def _da_sample(qkv, cache_k, cache_v, layer, page_table, rel_bias, lq1, lk1, lq2, lk2, sub_gain, lam_init):
    b, lq, _ = qkv[0].shape
    assert PAGE_SIZE >= REL_MAX_DIST
    n_pages = page_table.shape[1]
    pages = math.gcd(PAGES_PER_STEP, n_pages)
    rows = DA_HEADS * 2 * lq
    assert rows == LANES and lq == SUBLANES
    rbt = jnp.repeat(rel_bias.T, 2 * lq, axis=0)
    vec = lambda x: x.reshape(1, -1)
    vspec = lambda w: pl.BlockSpec((1, w), lambda bi, si, pt: (0, 0))
    new = pl.BlockSpec((None, lq, DA_HW), lambda bi, si, pt: (bi, 0, 0))
    page = lambda r, shape: pl.BlockSpec((None, None) + shape,
                                         lambda bi, si, pt: (layer, pt[bi, si * pages + r], 0, 0))
    k_shape, v_shape = cache_k.shape[2:], cache_v.shape[2:]
    grid_spec = pltpu.PrefetchScalarGridSpec(
        num_scalar_prefetch=1,
        grid=(b, n_pages // pages),
        in_specs=[new, new, new,
                  pl.BlockSpec((rows, REL_BUCKETS), lambda bi, si, pt: (0, 0)),
                  vspec(DA_DH), vspec(DA_DH), vspec(DA_DH), vspec(DA_DH), vspec(DA_VD)]
                 + [page(r, k_shape) for r in range(pages)] + [page(r, v_shape) for r in range(pages)],
        out_specs=pl.BlockSpec((None, lq, DA_HW), lambda bi, si, pt: (bi, 0, 0)),
        scratch_shapes=[pltpu.VMEM((rows, DA_HW), BF16), pltpu.VMEM((rows, LANES), F32),
                        pltpu.VMEM((rows, LANES), F32), pltpu.VMEM((rows, DA_VD), F32),
                        pltpu.VMEM((pages, rows, PAGE_SIZE), F32)],
    )
    return pl.pallas_call(
        functools.partial(_da_sample_kernel, pages=pages, n_pages=n_pages, lq=lq, lam_init=lam_init),
        grid_spec=grid_spec,
        out_shape=jax.ShapeDtypeStruct((b, lq, DA_HW), F32),
        compiler_params=_params("parallel", "arbitrary"),
        name="da_sample",
    )(page_table, *qkv, rbt, vec(lq1), vec(lk1), vec(lq2), vec(lk2), vec(sub_gain),
      *([cache_k] * pages), *([cache_v] * pages))


def _gdn_layer(y, conv_buf, s0, g_pre, w_main, w_ba, conv_w, a_log, dt_bias, o_gain):
    b, seq, d = y.shape
    yf = y.reshape(b * seq, d)
    *qkvz, ba = _norm_matmul(yf, g_pre, w_main, D_MODEL, w_narrow=w_ba)
    qkvz = [x.reshape(b, seq, D_MODEL) for x in qkvz]
    ba = ba.reshape(b, seq, LANES)
    cb = jnp.transpose(conv_buf.reshape(b, CONV_W - 1, 3, D_MODEL), (0, 2, 1, 3))
    o, s_new = _gdn_core(qkvz, ba, cb, conv_w, a_log, dt_bias, o_gain, s0)
    tail = jnp.concatenate([x[:, -(CONV_W - 1):] for x in qkvz[:3]], axis=-1)
    pre = jnp.concatenate([conv_buf, tail], axis=1)
    new_buf = pre[:, -(CONV_W - 1):]
    return o, new_buf, s_new


def _layer_rest(o, y, g_pre, g_post, w_out, w_q, w_o, w_gu, w_down, mem):
    b, seq, d = y.shape
    mem_k, mem_v, layer = mem
    if layer is None and seq % TAIL_ROW_TILE == 0:
        gains = jnp.stack([g_post[0], g_pre[1], g_post[1], g_pre[2], g_post[2]])
        return _layer_tail(o, y, gains, w_out, w_q, mem_k, mem_v, w_o, w_gu, w_down)
    if layer is None:
        mem_k, mem_v, layer = mem_k[None], mem_v[None], 0
    y = _matmul_postnorm(o.reshape(b * seq, -1), w_out, g_post[0], y.reshape(b * seq, d)).reshape(b, seq, d)
    y = _cross_attn(y, g_pre[1], w_q, mem_k, mem_v, layer, layer, w_o, g_post[1])
    return _ffn(y.reshape(b * seq, d), g_pre[2], w_gu, w_down, g_post[2]).reshape(b, seq, d)


def kernel(x_prompt, x_sample, state_gdn, state_conv, cache_k, cache_v, cache_mem_k, cache_mem_v, page_table,
           mem_prompt, rel_bias, norm_pre, norm_post, gdn_w_in, gdn_conv_w, gdn_a_log, gdn_dt_bias, gdn_o_gain,
           gdn_w_out, da_w_in, da_lq1, da_lk1, da_lq2, da_lk2, da_sub_gain, da_w_out, mem_gain, w_xq, w_xkv, w_xo,
           ffn_w_gu, ffn_w_down):
    depth = norm_pre.shape[0]
    b_p, l_p, d = x_prompt.shape
    b_s, l_s, _ = x_sample.shape
    m_len = mem_prompt.shape[1]
    yp, ys = x_prompt, x_sample
    mem_flat = mem_prompt.reshape(b_p * m_len, d)
    ck = jnp.transpose(cache_k, (0, 1, 3, 4, 5, 2)).reshape(cache_k.shape[:2] + (DA_HW, PAGE_SIZE))
    cv = cache_v.reshape(cache_v.shape[:2] + (PAGE_SIZE * DA_HEADS, DA_VD))
    mem_rows = lambda x: jnp.transpose(
        x.reshape(x.shape[:4] + (X_DH // LANES, LANES)), (0, 1, 2, 4, 3, 5)).reshape(x.shape[:2] + (-1, LANES))
    cmk, cmv = mem_rows(cache_mem_k), mem_rows(cache_mem_v)
    bias_tiles = _bias_tiles(rel_bias, min(ATTN_TILE, l_p))
    gdn_p, conv_p, gdn_s, conv_s = [], [], [], []
    k_p, v_p, k_s, v_s = [], [], [], []
    mk_p, mv_p = [], []
    for i in range(depth):
        j = i // N_MIXERS
        g_pre, g_post = norm_pre[i], norm_post[i]
        if i % N_MIXERS == 0:
            w_in = gdn_w_in[j]
            n_main = CONV_DIM + GDN_VW
            w_main = w_in[:, :n_main].astype(BF16)
            w_ba = jnp.pad(w_in[:, n_main:], ((0, 0), (0, LANES - 2 * GDN_HEADS))).astype(BF16)
            conv_w = jnp.transpose(gdn_conv_w[j].reshape(CONV_W, 3, D_MODEL), (1, 0, 2))
            w_out = gdn_w_out[j].astype(BF16)
            gw = (g_pre[0], w_main, w_ba, conv_w, gdn_a_log[j], gdn_dt_bias[j], gdn_o_gain[j])
            buf0 = jnp.zeros((b_p, CONV_W - 1, CONV_DIM), F32)
            s0 = jnp.zeros((b_p, GDN_HEADS, GDN_DK, GDN_DV), F32)
            op, cb, st = _gdn_layer(yp, buf0, s0, *gw)
            gdn_p.append(st)
            conv_p.append(cb)
            os_, cb, st = _gdn_layer(ys, state_conv[j], state_gdn[j], *gw)
            gdn_s.append(st)
            conv_s.append(cb)
        else:
            lam_init = 0.8 - 0.6 * math.exp(-0.3 * i)
            w_in = da_w_in[j].astype(BF16)
            w_out = da_w_out[j].astype(BF16)
            lw = (da_lq1[j], da_lk1[j], da_lq2[j], da_lk2[j], da_sub_gain[j], lam_init)
            qkv = [x.reshape(b_p, l_p, DA_HW) for x in _norm_matmul(yp.reshape(b_p * l_p, d), g_pre[0], w_in, DA_HW)]
            op = _da_prompt(qkv, bias_tiles, *lw)
            k_p.append(qkv[1].reshape(b_p, l_p, DA_HEADS, 2, DA_DH))
            v_p.append(qkv[2].reshape(b_p, l_p, DA_HEADS, DA_VD))
            qkv = [x.reshape(b_s, l_s, DA_HW) for x in _norm_matmul(ys.reshape(b_s * l_s, d), g_pre[0], w_in, DA_HW)]
            os_ = _da_sample(qkv, ck, cv, j, page_table, rel_bias, *lw)
            k_s.append(qkv[1].reshape(b_s, l_s, DA_HEADS, 2, DA_DH))
            v_s.append(qkv[2].reshape(b_s, l_s, DA_HEADS, DA_VD))
        mk, mv = _norm_matmul(mem_flat, mem_gain[i], w_xkv[i].astype(BF16), d)
        mk_p.append(mk.reshape(b_p, m_len, X_HEADS, X_DH))
        mv_p.append(mv.reshape(b_p, m_len, X_HEADS, X_DH))
        rest = (g_pre, g_post, w_out, w_xq[i].astype(BF16), w_xo[i].astype(BF16),
                ffn_w_gu[i].astype(BF16), ffn_w_down[i].astype(BF16))
        yp = _layer_rest(op, yp, *rest, (mk.reshape(b_p, m_len, d), mv.reshape(b_p, m_len, d), None))
        ys = _layer_rest(os_, ys, *rest, (cmk, cmv, i))
    return (yp, ys,
            jnp.stack(gdn_p), jnp.stack(conv_p), jnp.stack(k_p), jnp.stack(v_p),
            jnp.stack(mk_p), jnp.stack(mv_p),
            jnp.stack(gdn_s), jnp.stack(conv_s), jnp.stack(k_s), jnp.stack(v_s))
```

```python
import functools
import math

import jax
import jax.numpy as jnp
from jax import lax
from jax.experimental import pallas as pl
from jax.experimental.pallas import tpu as pltpu

F32 = jnp.float32
BF16 = jnp.bfloat16

D_MODEL = 1024
N_MIXERS = 2
GDN_HEADS = 8
GDN_DK = 128
GDN_DV = 128
CONV_W = 4
GDN_CHUNK = 64
INV_BLOCK = 8
GDN_GROUP = 8
GDN_MIN_HEADS = 2
SCAN_UNROLL = 4
GDN_QK = GDN_HEADS * GDN_DK
GDN_VW = GDN_HEADS * GDN_DV
CONV_DIM = 2 * GDN_QK + GDN_VW
DA_HEADS = 8
DA_DH = 64
DA_VD = 2 * DA_DH
DA_HW = DA_HEADS * 2 * DA_DH
PAGE_SIZE = 128
REL_BUCKETS = 32
REL_MAX_DIST = 128
X_HEADS = 4
X_DH = D_MODEL // X_HEADS
D_FF = -(-8 * D_MODEL // (3 * 256)) * 256
RMS_EPS = 1e-6
NEG_INF = -1e30

LANES = 128
SUBLANES = 8
ROW_TILE = 512
TAIL_ROW_TILE = 512
MXU_TILE = 256
TAIL_FFN_BOUNDS = (0, -(-D_FF // (2 * MXU_TILE)) * MXU_TILE, D_FF)
ATTN_TILE = 256
PAGES_PER_STEP = 16
XATTN_ROWS = 64
VMEM_LIMIT = 56 * 1024 * 1024

_NT = (((1,), (1,)), ((), ()))


def _params(*sem):
    return pltpu.CompilerParams(dimension_semantics=sem, vmem_limit_bytes=VMEM_LIMIT)


def _row_tile(t, rows=ROW_TILE):
    return rows if t % rows == 0 else t


def _rms(x, g):
    return x * lax.rsqrt(jnp.mean(x * x, axis=-1, keepdims=True) + RMS_EPS) * g


def _silu(x):
    return x * jax.nn.sigmoid(x)


def _mm(a, b):
    return jnp.dot(a, b, preferred_element_type=F32)


def _mm_nt(a, b):
    return lax.dot_general(a, b, _NT, preferred_element_type=F32)


def _split(x):
    hi = x.astype(BF16)
    return hi, (x - hi.astype(F32)).astype(BF16)


def _mmb(a, b):
    return _mm(a.astype(BF16), b.astype(BF16))


def _norm_matmul_kernel(x_ref, g_ref, *rest, n_w):
    w_refs, o_refs = rest[:n_w], rest[n_w:]
    h = _rms(x_ref[...], g_ref[...]).astype(BF16)
    tn = o_refs[0].shape[1]
    for j, o_ref in enumerate(o_refs[:-1] if n_w == 2 else o_refs):
        o_ref[...] = _mm(h, w_refs[0][:, j * tn:(j + 1) * tn])
    if n_w == 2:
        o_refs[-1][...] = _mm(h, w_refs[1][...])


def _norm_matmul(x, g, w, tn, w_narrow=None):
    t, d = x.shape
    n = w.shape[1]
    tm = _row_tile(t)
    resident = lambda a: pl.BlockSpec(a.shape, lambda i: (0, 0), pipeline_mode=pl.Buffered(1))
    weights = [w] if w_narrow is None else [w, w_narrow]
    widths = [tn] * (n // tn) + ([] if w_narrow is None else [LANES])
    return pl.pallas_call(
        functools.partial(_norm_matmul_kernel, n_w=len(weights)),
        grid=(t // tm,),
        in_specs=[pl.BlockSpec((tm, d), lambda i: (i, 0)), pl.BlockSpec((1, d), lambda i: (0, 0))]
                 + [resident(a) for a in weights],
        out_specs=[pl.BlockSpec((tm, width), lambda i: (i, 0)) for width in widths],
        out_shape=[jax.ShapeDtypeStruct((t, width), F32) for width in widths],
        compiler_params=_params("parallel"),
        name="norm_matmul",
    )(x, g.reshape(1, d), *weights)


def _matmul_postnorm_kernel(a_ref, w_ref, g_ref, r_ref, o_ref):
    y = _mm(a_ref[...].astype(BF16), w_ref[...])
    o_ref[...] = r_ref[...] + _rms(y, g_ref[...])


def _matmul_postnorm(a, w, g, res):
    t, k = a.shape
    d = w.shape[1]
    tm = _row_tile(t)
    return pl.pallas_call(
        _matmul_postnorm_kernel,
        grid=(t // tm,),
        in_specs=[
            pl.BlockSpec((tm, k), lambda i: (i, 0)),
            pl.BlockSpec((k, d), lambda i: (0, 0)),
            pl.BlockSpec((1, d), lambda i: (0, 0)),
            pl.BlockSpec((tm, d), lambda i: (i, 0)),
        ],
        out_specs=pl.BlockSpec((tm, d), lambda i: (i, 0)),
        out_shape=jax.ShapeDtypeStruct((t, d), F32),
        compiler_params=_params("parallel"),
        name="matmul_postnorm",
    )(a, w, g.reshape(1, d), res)


def _ffn_rows(x, g_pre, wg_ref, wu_ref, wd_ref, g_post, bounds=(0, D_FF)):
    h = _rms(x, g_pre).astype(BF16)
    y = None
    for lo, hi in zip(bounds[:-1], bounds[1:]):
        cols = slice(lo, hi)
        a = (_silu(_mm(h, wg_ref[:, cols])) * _mm(h, wu_ref[:, cols])).astype(BF16)
        part = _mm(a, wd_ref[cols, :])
        y = part if y is None else y + part
    return x + _rms(y, g_post)


def _ffn_kernel(x_ref, gpre_ref, wg_ref, wu_ref, wd_ref, gpost_ref, o_ref):
    o_ref[...] = _ffn_rows(x_ref[...], gpre_ref[...], wg_ref, wu_ref, wd_ref, gpost_ref[...])


def _ffn(x, g_pre, w_gu, w_down, g_post):
    t, d = x.shape
    tm = _row_tile(t)
    resident = lambda shape, col: pl.BlockSpec(shape, lambda i: (0, col), pipeline_mode=pl.Buffered(1))
    return pl.pallas_call(
        _ffn_kernel,
        grid=(t // tm,),
        in_specs=[
            pl.BlockSpec((tm, d), lambda i: (i, 0)),
            pl.BlockSpec((1, d), lambda i: (0, 0)),
            resident((d, D_FF), 0),
            resident((d, D_FF), 1),
            resident((D_FF, d), 0),
            pl.BlockSpec((1, d), lambda i: (0, 0)),
        ],
        out_specs=pl.BlockSpec((tm, d), lambda i: (i, 0)),
        out_shape=jax.ShapeDtypeStruct((t, d), F32),
        compiler_params=_params("parallel"),
        name="ffn",
    )(x, g_pre.reshape(1, d), w_gu, w_gu, w_down, g_post.reshape(1, d))


def _mem_attention(q, mk, mv):
    heads = []
    for hd in range(X_HEADS):
        sl = slice(hd * X_DH, (hd + 1) * X_DH)
        s = _mm_nt(q[:, sl], mk[:, sl])
        p = jnp.exp(s - jnp.max(s, axis=-1, keepdims=True))
        heads.append(_mm(p.astype(BF16), mv[:, sl]) / jnp.sum(p, axis=-1, keepdims=True))
    return jnp.concatenate(heads, axis=-1)


def _cross_attn_rows(y, g_pre, wq_ref, mems, wo_ref, g_post):
    h = _rms(y, g_pre).astype(BF16)
    q = (_mm(h, wq_ref[...]) * (X_DH ** -0.5)).astype(BF16)
    rows = y.shape[0] // len(mems)
    o = [_mem_attention(q[e * rows:(e + 1) * rows], mk, mv) for e, (mk, mv) in enumerate(mems)]
    o = (o[0] if len(o) == 1 else jnp.concatenate(o, axis=0)).astype(BF16)
    return y + _rms(_mm(o, wo_ref[...]), g_post)


def _cross_attn_kernel(y_ref, gpre_ref, wq_ref, mk_ref, mv_ref, wo_ref, gpost_ref, o_ref, *, tiled_mem):
    gb, tm, d = y_ref.shape
    if tiled_mem:
        tiles = X_DH // LANES
        m = mk_ref.shape[1] // (tiles * X_HEADS)
        load = lambda ref, e: jnp.concatenate(
            [ref[e, pl.ds(lt * X_HEADS + hd, m, stride=tiles * X_HEADS), :]
             for hd in range(X_HEADS) for lt in range(tiles)], axis=1).astype(BF16)
    else:
        load = lambda ref, e: ref[e].astype(BF16)
    mems = [(load(mk_ref, e), load(mv_ref, e)) for e in range(gb)]
    y = _cross_attn_rows(y_ref[...].reshape(gb * tm, d), gpre_ref[...], wq_ref, mems, wo_ref, gpost_ref[...])
    o_ref[...] = y.reshape(gb, tm, d)


def _cross_attn(y, g_pre, w_q, mem_k, mem_v, layer_k, layer_v, w_o, g_post):
    b, l, d = y.shape
    mem_block = mem_k.shape[2:]
    tm = _row_tile(l)
    gb = math.gcd(b, max(1, XATTN_ROWS // tm))
    resident = pl.BlockSpec((d, d), lambda bi, i: (0, 0), pipeline_mode=pl.Buffered(1))
    return pl.pallas_call(
        functools.partial(_cross_attn_kernel, tiled_mem=mem_block[1] != d),
        grid=(b // gb, l // tm),
        in_specs=[
            pl.BlockSpec((gb, tm, d), lambda bi, i: (bi, i, 0)),
            pl.BlockSpec((1, d), lambda bi, i: (0, 0)),
            resident,
            pl.BlockSpec((None, gb) + mem_block, lambda bi, i: (layer_k, bi, 0, 0)),
            pl.BlockSpec((None, gb) + mem_block, lambda bi, i: (layer_v, bi, 0, 0)),
            resident,
            pl.BlockSpec((1, d), lambda bi, i: (0, 0)),
        ],
        out_specs=pl.BlockSpec((gb, tm, d), lambda bi, i: (bi, i, 0)),
        out_shape=jax.ShapeDtypeStruct((b, l, d), F32),
        compiler_params=_params("parallel", "parallel"),
        name="cross_attn",
    )(y, g_pre.reshape(1, d), w_q, mem_k, mem_v, w_o, g_post.reshape(1, d))


def _layer_tail_kernel(a_ref, y_ref, gains_ref, wout_ref, wq_ref, mk_ref, mv_ref, wo_ref, wg_ref, wu_ref, wd_ref,
                       o_ref):
    gain = lambda r: gains_ref[r:r + 1, :]
    y = y_ref[...] + _rms(_mm(a_ref[...].astype(BF16), wout_ref[...]), gain(0))
    mems = [(mk_ref[...].astype(BF16), mv_ref[...].astype(BF16))]
    y = _cross_attn_rows(y, gain(1), wq_ref, mems, wo_ref, gain(2))
    o_ref[...] = _ffn_rows(y, gain(3), wg_ref, wu_ref, wd_ref, gain(4), bounds=TAIL_FFN_BOUNDS)


def _layer_tail(a, y, gains, w_out, w_q, mem_k, mem_v, w_o, w_gu, w_down):
    b, l, d = y.shape
    k = a.shape[2]
    m = mem_k.shape[1]
    tm = _row_tile(l, TAIL_ROW_TILE)
    rows = lambda width: pl.BlockSpec((None, tm, width), lambda bi, i: (bi, i, 0))
    resident = lambda shape, col=0: pl.BlockSpec(shape, lambda bi, i: (0, col), pipeline_mode=pl.Buffered(1))
    mem = pl.BlockSpec((None, m, d), lambda bi, i: (bi, 0, 0))
    return pl.pallas_call(
        _layer_tail_kernel,
        grid=(b, l // tm),
        in_specs=[
            rows(k), rows(d), resident(gains.shape),
            resident((k, d)), resident((d, d)), mem, mem, resident((d, d)),
            resident((d, D_FF), 0), resident((d, D_FF), 1), resident((D_FF, d)),
        ],
        out_specs=rows(d),
        out_shape=jax.ShapeDtypeStruct((b, l, d), F32),
        compiler_params=_params("parallel", "parallel"),
        name="layer_tail",
    )(a, y, gains, w_out, w_q, mem_k, mem_v, w_o, w_gu, w_gu, w_down)


def _gdn_kernel(alog_ref, dtb_ref, q_ref, k_ref, v_ref, z_ref, ba_ref, cb_ref, cw_ref, og_ref, s0_ref,
                o_ref, sn_ref,
                xq_ref, xk_ref, xv_ref, slot0_ref, slot1_ref, sm_ref, sb_ref, oq_ref, ob_ref, gl_ref,
                *, seq, chunk, n_chunks, group, hb):
    c_ = chunk
    padded = n_chunks * c_
    rows = min(seq, c_)
    width = hb * LANES
    heads = [pl.program_id(1) * hb + hh for hh in range(hb)]
    cols = lambda hh: slice(hh * LANES, (hh + 1) * LANES)

    for part, (x_ref, src) in enumerate(((xq_ref, q_ref), (xk_ref, k_ref), (xv_ref, v_ref))):
        x_ref[0:SUBLANES, :] = jnp.zeros((SUBLANES, width), F32)
        x_ref[SUBLANES - (CONV_W - 1):SUBLANES, :] = cb_ref[part]
        x_ref[pl.ds(SUBLANES, seq), :] = src[...]
        if padded > seq:
            x_ref[pl.ds(SUBLANES + seq, padded - seq), :] = jnp.zeros((padded - seq, width), F32)

    alog = [jnp.full((1, 1), alog_ref[h], F32) for h in heads]
    dtb = [jnp.full((1, 1), dtb_ref[h], F32) for h in heads]
    r_i =lax.broadcasted_iota(jnp.int32, (c_, c_), 0)
    c_i = lax.broadcasted_iota(jnp.int32, (c_, c_), 1)
    incl = r_i >= c_i
    strict = r_i > c_i
    tril = jnp.where(incl, 1.0, 0.0).astype(BF16)
    eye = jnp.where(r_i == c_i, 1.0, 0.0).astype(F32)
    blk = lambda size: (r_i // size) == (c_i // size)
    diag_blocks = blk(INV_BLOCK)
    merges = [jnp.logical_and(blk(2 * size), jnp.logical_not(blk(size)))
              for size in (INV_BLOCK << i for i in range(int(math.log2(c_ // INV_BLOCK))))]
    lane = lax.broadcasted_iota(jnp.int32, (c_, LANES), 1)

    def conv(win, part, hh):
        w = cw_ref[part][:, cols(hh)]
        base = SUBLANES - (CONV_W - 1)
        y = win[base:base + c_] * w[0:1]
        for j in range(1, CONV_W):
            y = y + win[base + j:base + j + c_] * w[j:j + 1]
        return _silu(y)

    def pad_rows(x):
        if rows == c_:
            return x
        return jnp.concatenate([x, jnp.zeros((c_ - rows, x.shape[1]), x.dtype)], axis=0)

    def rows_of(c):
        return c * c_ if isinstance(c, int) else pl.multiple_of(c * c_, c_)

    def gates(hh, r0, wins):
        h = heads[hh]
        qc = conv(wins[0], 0, hh)
        kc = conv(wins[1], 1, hh)
        v = conv(wins[2], 2, hh)
        q = qc * lax.rsqrt(jnp.sum(qc * qc, axis=-1, keepdims=True) + RMS_EPS) * (GDN_DK ** -0.5)
        k = kc * lax.rsqrt(jnp.sum(kc * kc, axis=-1, keepdims=True) + RMS_EPS)
        ba = pad_rows(ba_ref[pl.ds(r0, rows), :])
        b_raw = jnp.sum(jnp.where(lane == h, ba, 0.0), axis=-1, keepdims=True)
        a_raw = jnp.sum(jnp.where(lane == h + GDN_HEADS, ba, 0.0), axis=-1, keepdims=True)
        beta = jax.nn.sigmoid(b_raw)
        x = a_raw + dtb[hh]
        g = -jnp.exp(alog[hh]) * (jnp.maximum(x, 0.0) + jnp.log1p(jnp.exp(-jnp.abs(x))))
        if padded > seq:
            valid = (lax.broadcasted_iota(jnp.int32, (c_, 1), 0) + r0) < seq
            k = jnp.where(valid, k, 0.0)
            beta = jnp.where(valid, beta, 0.0)
            g = jnp.where(valid, g, 0.0)
        return q, k, v, jnp.broadcast_to(beta, (c_, LANES)), jnp.broadcast_to(g, (c_, LANES))

    def gates_group(gi, slot_ref):
        for i in range(group):
            r0 = rows_of(gi * group + i)
            for hh in range(hb):
                wins = [ref[pl.ds(r0, c_ + SUBLANES), cols(hh)] for ref in (xq_ref, xk_ref, xv_ref)]
                for n, x in enumerate(gates(hh, r0, wins)):
                    slot_ref[n, i * c_:(i + 1) * c_, cols(hh)] = x

    def prep_group(gi, slot_ref):
        hhs = [hh for hh in range(hb) for _ in range(group)]
        local = [i for _ in range(hb) for i in range(group)]
        cs = [gi * group + i for i in local]
        r0s = [rows_of(c) for c in cs]
        every = lambda f, *xs: [f(*args) for args in zip(*xs)]
        slot = lambda n: [lambda hh=hh, i=i: slot_ref[n, i * c_:(i + 1) * c_, cols(hh)] for hh, i in zip(hhs, local)]
        q, k, v, beta = slot(0), slot(1), slot(2), slot(3)
        g_hi, g_lo = zip(*every(lambda get: _split(get()), slot(4)))
        gc = every(lambda hi, lo: _mm(tril, hi) + _mm(tril, lo), g_hi, g_lo)
        kb = every(lambda get: get().astype(BF16), k)
        kk = every(_mm_nt, kb, kb)
        qk = every(lambda get, y: _mm_nt(get().astype(BF16), y), q, kb)
        decay = every(lambda x: jnp.where(incl, jnp.exp(jnp.where(incl, x[:, :c_] - x.T[:c_], 0.0)), 0.0), gc)
        low = every(lambda b_, d_, kk_: jnp.where(strict, b_()[:, :c_] * d_ * kk_, 0.0), beta, decay, kk)
        pw = every(lambda x: -jnp.where(diag_blocks, x, 0.0), low)
        t_inv = every(lambda x: eye + x, pw)
        for _ in range(int(math.log2(INV_BLOCK)) - 1):
            pw = every(_mmb, pw, pw)
            t_inv = every(lambda t_, p_: t_ + _mmb(t_, p_), t_inv, pw)
        for merge in merges:
            off = every(lambda x, t_: _mmb(jnp.where(merge, x, 0.0), t_), low, t_inv)
            t_inv = every(lambda t_, x: t_ - _mmb(t_, x), t_inv, off)
        egc = every(jnp.exp, gc)
        rhs = every(lambda b_, e_, k_, v_: jnp.concatenate([(b_() * e_) * k_(), b_() * v_()], axis=1),
                    beta, egc, k, v)
        w = every(lambda t_, x: _mmb(t_, x).astype(BF16), t_inv, rhs)
        gc_last = every(lambda x: x[c_ - 1:c_, :], gc)
        kdt = every(lambda k_, x, l_: (k_() * jnp.exp(l_ - x)).T.astype(BF16), k, gc, gc_last)
        a = every(lambda x, d_: (x * d_).astype(BF16), qk, decay)
        kw = every(_mm, kdt, w)
        aw = every(_mm, a, w)
        for i, (hh, c, r0) in enumerate(zip(hhs, cs, r0s)):
            sm_ref[hh * n_chunks + c] = kw[i][:, :GDN_DK].astype(BF16)
            sb_ref[hh * n_chunks + c] = kw[i][:, GDN_DK:]
            oq_ref[pl.ds(r0, c_), cols(hh)] = (q[i]() * egc[i] - aw[i][:, :GDN_DK]).astype(BF16)
            ob_ref[pl.ds(r0, c_), cols(hh)] = aw[i][:, GDN_DK:]
            gl_ref[pl.ds(hh * n_chunks + c, 1), :] = jnp.exp(gc_last[i])

    og = og_ref[...]

    def scan(c, states):
        r0 = rows_of(c)
        at = [hh * n_chunks + c for hh in range(hb)]
        sb = [s.astype(BF16) for s in states]
        upd = [_mm(sm_ref[i], x) for i, x in zip(at, sb)]
        o = [_mm(oq_ref[pl.ds(r0, c_), cols(hh)], sb[hh]) + ob_ref[pl.ds(r0, c_), cols(hh)] for hh in range(hb)]
        for hh in range(hb):
            z = z_ref[pl.ds(r0, rows), cols(hh)]
            o_ref[pl.ds(r0, rows), cols(hh)] = _rms(o[hh][:rows], og) * _silu(z)
        return tuple(gl_ref[pl.ds(i, 1), :] * s - u + sb_ref[i] for i, s, u in zip(at, states, upd))

    def pipelined(g2, carry):
        prep_group(2 * g2, slot0_ref)
        gates_group(2 * g2 + 1, slot1_ref)
        prep_group(2 * g2 + 1, slot1_ref)
        gates_group(2 * g2 + 2, slot0_ref)
        return carry

    n_groups = n_chunks // group
    pairs = (n_groups - 1) // 2
    gates_group(0, slot0_ref)
    lax.fori_loop(0, pairs, pipelined, 0)
    prep_group(2 * pairs, slot0_ref)
    if n_groups - 2 * pairs == 2:
        gates_group(2 * pairs + 1, slot1_ref)
        prep_group(2 * pairs + 1, slot1_ref)
    final = lax.fori_loop(0, n_chunks, scan, tuple(s0_ref[hh] for hh in range(hb)),
                          unroll=math.gcd(SCAN_UNROLL, n_chunks))
    for hh in range(hb):
        sn_ref[hh] = final[hh]


def _gdn_core(qkvz, ba, conv_buf, conv_w, a_log, dt_bias, o_gain, s0):
    b, seq, _ = qkvz[0].shape
    chunk = GDN_CHUNK
    n_chunks = -(-seq // chunk)
    padded = n_chunks * chunk
    group = math.gcd(GDN_GROUP // GDN_MIN_HEADS, n_chunks)
    hb = math.gcd(GDN_GROUP // group, GDN_HEADS)
    width = hb * LANES
    kern = functools.partial(_gdn_kernel, seq=seq, chunk=chunk, n_chunks=n_chunks, group=group, hb=hb)
    head_cols = pl.BlockSpec((None, seq, width), lambda bi, hi: (bi, 0, hi))
    smem = pl.BlockSpec(memory_space=pltpu.SMEM)
    return pl.pallas_call(
        kern,
        grid=(b, GDN_HEADS // hb),
        in_specs=[
            smem, smem,
            head_cols, head_cols, head_cols, head_cols,
            pl.BlockSpec((None, seq, LANES), lambda bi, hi: (bi, 0, 0)),
            pl.BlockSpec((None, 3, CONV_W - 1, width), lambda bi, hi: (bi, 0, 0, hi)),
            pl.BlockSpec((3, CONV_W, width), lambda bi, hi: (0, 0, hi)),
            pl.BlockSpec((1, GDN_DV), lambda bi, hi: (0, 0)),
            pl.BlockSpec((None, hb, GDN_DK, GDN_DV), lambda bi, hi: (bi, hi, 0, 0)),
        ],
        out_specs=[
            pl.BlockSpec((None, seq, width), lambda bi, hi: (bi, 0, hi)),
            pl.BlockSpec((None, hb, GDN_DK, GDN_DV), lambda bi, hi: (bi, hi, 0, 0)),
        ],
        out_shape=[
            jax.ShapeDtypeStruct((b, seq, GDN_VW), F32),
            jax.ShapeDtypeStruct((b, GDN_HEADS, GDN_DK, GDN_DV), F32),
        ],
        scratch_shapes=[
            pltpu.VMEM((padded + SUBLANES, width), F32),
            pltpu.VMEM((padded + SUBLANES, width), F32),
            pltpu.VMEM((padded + SUBLANES, width), F32),
            pltpu.VMEM((5, group * chunk, width), F32),
            pltpu.VMEM((5, group * chunk, width), F32),
            pltpu.VMEM((hb * n_chunks, GDN_DK, GDN_DK), BF16),
            pltpu.VMEM((hb * n_chunks, GDN_DK, GDN_DV), F32),
            pltpu.VMEM((padded, width), BF16),
            pltpu.VMEM((padded, width), F32),
            pltpu.VMEM((hb * n_chunks, LANES), F32),
        ],
        compiler_params=_params("parallel", "parallel"),
        name="gdn_core",
    )(a_log, dt_bias, *qkvz, ba, conv_buf, conv_w, o_gain.reshape(1, GDN_DV), s0)


def _rel_bucket(dist):
    max_exact = REL_BUCKETS // 2
    n = jnp.maximum(dist, 0)
    large = max_exact + (jnp.log(jnp.maximum(n, 1).astype(F32) / max_exact)
                         / math.log(REL_MAX_DIST / max_exact) * (REL_BUCKETS - max_exact)).astype(jnp.int32)
    large = jnp.minimum(large, REL_BUCKETS - 1)
    return jnp.where(n < max_exact, n, large)


def _bias_tiles_kernel(rb_ref, o_ref, *, t):
    h = pl.program_id(0)
    d = pl.program_id(1)
    r = lax.broadcasted_iota(jnp.int32, (t, t), 0)
    c = lax.broadcasted_iota(jnp.int32, (t, t), 1)
    dist = d * t + r - c
    bucket = _rel_bucket(dist)
    bias = jnp.zeros((t, t), F32)
    for b in range(REL_BUCKETS):
        bias = jnp.where(bucket == b, rb_ref[b, h], bias)
    o_ref[...] = jnp.where(dist >= 0, bias - rb_ref[REL_BUCKETS - 1, h], NEG_INF)


def _bias_tiles(rel_bias, t):
    assert t >= REL_MAX_DIST
    return pl.pallas_call(
        functools.partial(_bias_tiles_kernel, t=t),
        grid=(DA_HEADS, 2),
        in_specs=[pl.BlockSpec(memory_space=pltpu.SMEM)],
        out_specs=pl.BlockSpec((None, None, t, t), lambda h, d: (h, d, 0, 0)),
        out_shape=jax.ShapeDtypeStruct((DA_HEADS, 2, t, t), F32),
        compiler_params=_params("parallel", "parallel"),
        name="rel_bias_tiles",
    )(rel_bias)


def _lambda(lq1_ref, lk1_ref, lq2_ref, lk2_ref, lam_init):
    dot = lambda a, b: jnp.sum(a[...] * b[...], axis=-1, keepdims=True)
    return jnp.exp(dot(lq1_ref, lk1_ref)) - jnp.exp(dot(lq2_ref, lk2_ref)) + lam_init


def _da_prompt_kernel(q_ref, k_ref, v_ref, b_ref, lq1_ref, lk1_ref, lq2_ref, lk2_ref, sg_ref, o_ref,
                      kb_ref, vb_ref, s_ref, *, t, lam_init):
    seq = q_ref.shape[0]
    kb_ref[...] = k_ref[...].astype(BF16)
    vb_ref[:, :DA_VD] = v_ref[...].astype(BF16)
    vb_ref[:, DA_VD:] = jnp.ones((seq, LANES), BF16)
    lam = _lambda(lq1_ref, lk1_ref, lq2_ref, lk2_ref, lam_init)
    sg = sg_ref[...]
    lane = lax.broadcasted_iota(jnp.int32, (t, LANES), 1)
    stack = lambda x: jnp.concatenate([x, x], axis=0)
    for qi in range(seq // t):
        q = q_ref[qi * t:(qi + 1) * t, :] * (DA_DH ** -0.5)
        qq = jnp.concatenate([jnp.where(lane < DA_DH, q, 0.0), jnp.where(lane >= DA_DH, q, 0.0)], axis=0)
        n_k = (qi + 1) * t
        s_ref[:, :n_k] = _mm_nt(qq.astype(BF16), kb_ref[:n_k, :])
        s_ref[:, qi * t:n_k] += stack(b_ref[0])
        if qi >= 1:
            s_ref[:, (qi - 1) * t:qi * t] += stack(b_ref[1])
        s = s_ref[:, :n_k]
        p = jnp.exp(s - jnp.max(s, axis=-1, keepdims=True)).astype(BF16)
        acc = _mm(p, vb_ref[:n_k, :])
        a = acc[:, :DA_VD] / acc[:, DA_VD:]
        o = a[:t] - lam * a[t:]
        o_ref[qi * t:(qi + 1) * t, :] = _rms(o, sg) * (1.0 - lam_init)


def _da_prompt(qkv, bias_tiles, lq1, lk1, lq2, lk2, sub_gain, lam_init):
    b, seq, _ = qkv[0].shape
    t = bias_tiles.shape[-1]
    vec = lambda x: x.reshape(1, -1)
    vspec = lambda w: pl.BlockSpec((1, w), lambda bi, hi: (0, 0))
    head = pl.BlockSpec((None, seq, LANES), lambda bi, hi: (bi, 0, hi))
    return pl.pallas_call(
        functools.partial(_da_prompt_kernel, t=t, lam_init=lam_init),
        grid=(b, DA_HEADS),
        in_specs=[
            head, head, head,
            pl.BlockSpec((None, 2, t, t), lambda bi, hi: (hi, 0, 0, 0)),
            vspec(DA_DH), vspec(DA_DH), vspec(DA_DH), vspec(DA_DH), vspec(DA_VD),
        ],
        out_specs=pl.BlockSpec((None, seq, LANES), lambda bi, hi: (bi, 0, hi)),
        out_shape=jax.ShapeDtypeStruct((b, seq, DA_HEADS * DA_VD), F32),
        scratch_shapes=[pltpu.VMEM((seq, DA_DH * 2), BF16), pltpu.VMEM((seq, DA_VD + LANES), BF16),
                        pltpu.VMEM((2 * t, seq), F32)],
        compiler_params=_params("parallel", "parallel"),
        name="da_prompt",
    )(*qkv, bias_tiles, vec(lq1), vec(lk1), vec(lq2), vec(lk2), vec(sub_gain))


def _da_sample_kernel(pt_ref, q_ref, kn_ref, vn_ref, rbt_ref, lq1_ref, lk1_ref, lq2_ref, lk2_ref, sg_ref, *rest,
                      pages, n_pages, lq, lam_init):
    del pt_ref
    k_refs = rest[:pages]
    v_refs = rest[pages:2 * pages]
    o_ref, qbd_ref, m_ref, l_ref, acc_ref, s_ref = rest[2 * pages:]
    step = pl.program_id(1)
    rows = DA_HEADS * 2 * lq
    past_len = n_pages * PAGE_SIZE

    @pl.when(step == 0)
    def _():
        q = q_ref[...] * (DA_DH ** -0.5)
        qrep = jnp.concatenate([q] * (DA_HEADS * 2), axis=0)
        r_hc = lax.broadcasted_iota(jnp.int32, (rows, DA_HW), 0) // lq
        c_hc = lax.broadcasted_iota(jnp.int32, (rows, DA_HW), 1) // DA_DH
        qbd_ref[...] = jnp.where(r_hc == c_hc, qrep, 0.0).astype(BF16)
        m_ref[...] = jnp.full(m_ref.shape, NEG_INF, F32)
        l_ref[...] = jnp.zeros_like(l_ref)
        acc_ref[...] = jnp.zeros_like(acc_ref)

    def near_bias(k_start, n_valid):
        row = lax.broadcasted_iota(jnp.int32, (rows, PAGE_SIZE), 0)
        col = lax.broadcasted_iota(jnp.int32, (rows, PAGE_SIZE), 1)
        dist = past_len + row % lq - (k_start + col)
        bucket = _rel_bucket(dist)
        bias = jnp.zeros((rows, PAGE_SIZE), F32)
        for b in range(REL_BUCKETS):
            bias = jnp.where(bucket == b, rbt_ref[:, b:b + 1], bias)
        bias = bias - rbt_ref[:, REL_BUCKETS - 1:REL_BUCKETS]
        return jnp.where(jnp.logical_and(dist >= 0, col < n_valid), bias, NEG_INF)

    def attend(s_blocks, v_blocks):
        s_max = functools.reduce(jnp.maximum, s_blocks)
        m_prev = m_ref[...]
        m_new = jnp.maximum(m_prev, jnp.max(s_max, axis=-1, keepdims=True))
        alpha = jnp.exp(m_prev - m_new)
        p = [jnp.exp(s - m_new) for s in s_blocks]
        l_ref[...] = alpha * l_ref[...] + jnp.sum(functools.reduce(jnp.add, p), axis=-1, keepdims=True)
        pb = jnp.concatenate([x.astype(BF16) for x in p], axis=1)
        pv = [_mm(pb[h * 2 * lq:(h + 1) * 2 * lq], jnp.concatenate([v[h] for v in v_blocks], axis=0))
              for h in range(DA_HEADS)]
        acc_ref[...] = alpha * acc_ref[...] + jnp.concatenate(pv, axis=0)
        m_ref[...] = m_new

    n_tiles = DA_HW // MXU_TILE
    tile_rows = rows // n_tiles
    q_tiles = [qbd_ref[j * tile_rows:(j + 1) * tile_rows, j * MXU_TILE:(j + 1) * MXU_TILE] for j in range(n_tiles)]
    for r in range(pages):
        kt = k_refs[r][...].astype(BF16)
        s_ref[r] = jnp.concatenate(
            [_mm(q_tiles[j], kt[j * MXU_TILE:(j + 1) * MXU_TILE, :]) for j in range(n_tiles)], axis=0)

    last = step == pl.num_programs(1) - 1

    @pl.when(last)
    def _():
        s_ref[pages - 1] += near_bias(past_len - PAGE_SIZE, PAGE_SIZE)

    attend([s_ref[r] for r in range(pages)],
           [[v_refs[r][pl.ds(h, PAGE_SIZE, stride=DA_HEADS), :].astype(BF16) for h in range(DA_HEADS)]
            for r in range(pages)])

    @pl.when(last)
    def _():
        zeros = jnp.zeros((PAGE_SIZE - lq, DA_HW), F32)
        kn = jnp.concatenate([kn_ref[...], zeros], axis=0).astype(BF16)
        vn = jnp.concatenate([vn_ref[...], zeros], axis=0).astype(BF16)
        attend([_mm_nt(qbd_ref[...], kn) + near_bias(past_len, lq)],
               [[vn[:, h * DA_VD:(h + 1) * DA_VD] for h in range(DA_HEADS)]])
        a = acc_ref[...] / l_ref[...]
        lam = _lambda(lq1_ref, lk1_ref, lq2_ref, lk2_ref, lam_init)
        sg = sg_ref[...]
        heads = []
        for h in range(DA_HEADS):
            o = a[h * 2 * lq:h * 2 * lq + lq] - lam * a[h * 2 * lq + lq:(h + 1) * 2 * lq]
            heads.append(_rms(o, sg) * (1.0 - lam_init))
        o_ref[...] = jnp.concatenate(heads, axis=-1)


def _da_sample(qkv, cache_k, cache_v, layer, page_table, rel_bias, lq1, lk1, lq2, lk2, sub_gain, lam_init):
    b, lq, _ = qkv[0].shape
    assert PAGE_SIZE >= REL_MAX_DIST
    n_pages = page_table.shape[1]
    pages = math.gcd(PAGES_PER_STEP, n_pages)
    rows = DA_HEADS * 2 * lq
    assert rows == LANES and lq == SUBLANES
    rbt = jnp.repeat(rel_bias.T, 2 * lq, axis=0)
    vec = lambda x: x.reshape(1, -1)
    vspec = lambda w: pl.BlockSpec((1, w), lambda bi, si, pt: (0, 0))
    new = pl.BlockSpec((None, lq, DA_HW), lambda bi, si, pt: (bi, 0, 0))
    page = lambda r, shape: pl.BlockSpec((None, None) + shape,
                                         lambda bi, si, pt: (layer, pt[bi, si * pages + r], 0, 0))
    k_shape, v_shape = cache_k.shape[2:], cache_v.shape[2:]
    grid_spec = pltpu.PrefetchScalarGridSpec(
        num_scalar_prefetch=1,
        grid=(b, n_pages // pages),
        in_specs=[new, new, new,
                  pl.BlockSpec((rows, REL_BUCKETS), lambda bi, si, pt: (0, 0)),
                  vspec(DA_DH), vspec(DA_DH), vspec(DA_DH), vspec(DA_DH), vspec(DA_VD)]
                 + [page(r, k_shape) for r in range(pages)] + [page(r, v_shape) for r in range(pages)],
        out_specs=pl.BlockSpec((None, lq, DA_HW), lambda bi, si, pt: (bi, 0, 0)),
        scratch_shapes=[pltpu.VMEM((rows, DA_HW), BF16), pltpu.VMEM((rows, LANES), F32),
                        pltpu.VMEM((rows, LANES), F32), pltpu.VMEM((rows, DA_VD), F32),
                        pltpu.VMEM((pages, rows, PAGE_SIZE), F32)],
    )
    return pl.pallas_call(
        functools.partial(_da_sample_kernel, pages=pages, n_pages=n_pages, lq=lq, lam_init=lam_init),
        grid_spec=grid_spec,
        out_shape=jax.ShapeDtypeStruct((b, lq, DA_HW), F32),
        compiler_params=_params("parallel", "arbitrary"),
        name="da_sample",
    )(page_table, *qkv, rbt, vec(lq1), vec(lk1), vec(lq2), vec(lk2), vec(sub_gain),
      *([cache_k] * pages), *([cache_v] * pages))


def _gdn_layer(y, conv_buf, s0, g_pre, w_main, w_ba, conv_w, a_log, dt_bias, o_gain):
    b, seq, d = y.shape
    yf = y.reshape(b * seq, d)
    *qkvz, ba = _norm_matmul(yf, g_pre, w_main, D_MODEL, w_narrow=w_ba)
    qkvz = [x.reshape(b, seq, D_MODEL) for x in qkvz]
    ba = ba.reshape(b, seq, LANES)
    cb = jnp.transpose(conv_buf.reshape(b, CONV_W - 1, 3, D_MODEL), (0, 2, 1, 3))
    o, s_new = _gdn_core(qkvz, ba, cb, conv_w, a_log, dt_bias, o_gain, s0)
    tail = jnp.concatenate([x[:, -(CONV_W - 1):] for x in qkvz[:3]], axis=-1)
    pre = jnp.concatenate([conv_buf, tail], axis=1)
    new_buf = pre[:, -(CONV_W - 1):]
    return o, new_buf, s_new


def _layer_rest(o, y, g_pre, g_post, w_out, w_q, w_o, w_gu, w_down, mem):
    b, seq, d = y.shape
    mem_k, mem_v, layer = mem
    if layer is None and seq % TAIL_ROW_TILE == 0:
        gains = jnp.stack([g_post[0], g_pre[1], g_post[1], g_pre[2], g_post[2]])
        return _layer_tail(o, y, gains, w_out, w_q, mem_k, mem_v, w_o, w_gu, w_down)
    if layer is None:
        mem_k, mem_v, layer = mem_k[None], mem_v[None], 0
    y = _matmul_postnorm(o.reshape(b * seq, -1), w_out, g_post[0], y.reshape(b * seq, d)).reshape(b, seq, d)
    y = _cross_attn(y, g_pre[1], w_q, mem_k, mem_v, layer, layer, w_o, g_post[1])
    return _ffn(y.reshape(b * seq, d), g_pre[2], w_gu, w_down, g_post[2]).reshape(b, seq, d)


def kernel(x_prompt, x_sample, state_gdn, state_conv, cache_k, cache_v, cache_mem_k, cache_mem_v, page_table,
           mem_prompt, rel_bias, norm_pre, norm_post, gdn_w_in, gdn_conv_w, gdn_a_log, gdn_dt_bias, gdn_o_gain,
           gdn_w_out, da_w_in, da_lq1, da_lk1, da_lq2, da_lk2, da_sub_gain, da_w_out, mem_gain, w_xq, w_xkv, w_xo,
           ffn_w_gu, ffn_w_down):
    depth = norm_pre.shape[0]
    b_p, l_p, d = x_prompt.shape
    b_s, l_s, _ = x_sample.shape
    m_len = mem_prompt.shape[1]
    yp, ys = x_prompt, x_sample
    mem_flat = mem_prompt.reshape(b_p * m_len, d)
    ck = jnp.transpose(cache_k, (0, 1, 3, 4, 5, 2)).reshape(cache_k.shape[:2] + (DA_HW, PAGE_SIZE))
    cv = cache_v.reshape(cache_v.shape[:2] + (PAGE_SIZE * DA_HEADS, DA_VD))
    mem_rows = lambda x: jnp.transpose(
        x.reshape(x.shape[:4] + (X_DH // LANES, LANES)), (0, 1, 2, 4, 3, 5)).reshape(x.shape[:2] + (-1, LANES))
    cmk, cmv = mem_rows(cache_mem_k), mem_rows(cache_mem_v)
    bias_tiles = _bias_tiles(rel_bias, min(ATTN_TILE, l_p))
    gdn_p, conv_p, gdn_s, conv_s = [], [], [], []
    k_p, v_p, k_s, v_s = [], [], [], []
    mk_p, mv_p = [], []
    for i in range(depth):
        j = i // N_MIXERS
        g_pre, g_post = norm_pre[i], norm_post[i]
        if i % N_MIXERS == 0:
            w_in = gdn_w_in[j]
            n_main = CONV_DIM + GDN_VW
            w_main = w_in[:, :n_main].astype(BF16)
            w_ba = jnp.pad(w_in[:, n_main:], ((0, 0), (0, LANES - 2 * GDN_HEADS))).astype(BF16)
            conv_w = jnp.transpose(gdn_conv_w[j].reshape(CONV_W, 3, D_MODEL), (1, 0, 2))
            w_out = gdn_w_out[j].astype(BF16)
            gw = (g_pre[0], w_main, w_ba, conv_w, gdn_a_log[j], gdn_dt_bias[j], gdn_o_gain[j])
            buf0 = jnp.zeros((b_p, CONV_W - 1, CONV_DIM), F32)
            s0 = jnp.zeros((b_p, GDN_HEADS, GDN_DK, GDN_DV), F32)
            op, cb, st = _gdn_layer(yp, buf0, s0, *gw)
            gdn_p.append(st)
            conv_p.append(cb)
            os_, cb, st = _gdn_layer(ys, state_conv[j], state_gdn[j], *gw)
            gdn_s.append(st)
            conv_s.append(cb)
        else:
            lam_init = 0.8 - 0.6 * math.exp(-0.3 * i)
            w_in = da_w_in[j].astype(BF16)
            w_out = da_w_out[j].astype(BF16)
            lw = (da_lq1[j], da_lk1[j], da_lq2[j], da_lk2[j], da_sub_gain[j], lam_init)
            qkv = [x.reshape(b_p, l_p, DA_HW) for x in _norm_matmul(yp.reshape(b_p * l_p, d), g_pre[0], w_in, DA_HW)]
            op = _da_prompt(qkv, bias_tiles, *lw)
            k_p.append(qkv[1].reshape(b_p, l_p, DA_HEADS, 2, DA_DH))
            v_p.append(qkv[2].reshape(b_p, l_p, DA_HEADS, DA_VD))
            qkv = [x.reshape(b_s, l_s, DA_HW) for x in _norm_matmul(ys.reshape(b_s * l_s, d), g_pre[0], w_in, DA_HW)]
            os_ = _da_sample(qkv, ck, cv, j, page_table, rel_bias, *lw)
            k_s.append(qkv[1].reshape(b_s, l_s, DA_HEADS, 2, DA_DH))
            v_s.append(qkv[2].reshape(b_s, l_s, DA_HEADS, DA_VD))
        mk, mv = _norm_matmul(mem_flat, mem_gain[i], w_xkv[i].astype(BF16), d)
        mk_p.append(mk.reshape(b_p, m_len, X_HEADS, X_DH))
        mv_p.append(mv.reshape(b_p, m_len, X_HEADS, X_DH))
        rest = (g_pre, g_post, w_out, w_xq[i].astype(BF16), w_xo[i].astype(BF16),
                ffn_w_gu[i].astype(BF16), ffn_w_down[i].astype(BF16))
        yp = _layer_rest(op, yp, *rest, (mk.reshape(b_p, m_len, d), mv.reshape(b_p, m_len, d), None))
        ys = _layer_rest(os_, ys, *rest, (cmk, cmv, i))
    return (yp, ys,
            jnp.stack(gdn_p), jnp.stack(conv_p), jnp.stack(k_p), jnp.stack(v_p),
            jnp.stack(mk_p), jnp.stack(mv_p),
            jnp.stack(gdn_s), jnp.stack(conv_s), jnp.stack(k_s), jnp.stack(v_s))
```

```python
import functools
import math

import jax
import jax.numpy as jnp
from jax import lax
from jax.experimental import pallas as pl
from jax.experimental.pallas import tpu as pltpu

F32 = jnp.float32
BF16 = jnp.bfloat16

D_MODEL = 1024
N_MIXERS = 2
GDN_HEADS = 8
GDN_DK = 128
GDN_DV = 128
CONV_W = 4
GDN_CHUNK = 64
INV_BLOCK = 8
GDN_GROUP = 8
GDN_MIN_HEADS = 2
SCAN_UNROLL = 4
GDN_QK = GDN_HEADS * GDN_DK
GDN_VW = GDN_HEADS * GDN_DV
CONV_DIM = 2 * GDN_QK + GDN_VW
DA_HEADS = 8
DA_DH = 64
DA_VD = 2 * DA_DH
DA_HW = DA_HEADS * 2 * DA_DH
PAGE_SIZE = 128
REL_BUCKETS = 32
REL_MAX_DIST = 128
X_HEADS = 4
X_DH = D_MODEL // X_HEADS
D_FF = -(-8 * D_MODEL // (3 * 256)) * 256
RMS_EPS = 1e-6
NEG_INF = -1e30

LANES = 128
SUBLANES = 8
ROW_TILE = 512
TAIL_ROW_TILE = 512
MXU_TILE = 256
TAIL_FFN_BOUNDS = (0, -(-D_FF // (2 * MXU_TILE)) * MXU_TILE, D_FF)
ATTN_TILE = 256
PAGES_PER_STEP = 16
XATTN_ROWS = 64
VMEM_LIMIT = 56 * 1024 * 1024

_NT = (((1,), (1,)), ((), ()))


def _params(*sem):
    return pltpu.CompilerParams(dimension_semantics=sem, vmem_limit_bytes=VMEM_LIMIT)


def _row_tile(t, rows=ROW_TILE):
    return rows if t % rows == 0 else t


def _rms(x, g):
    return x * lax.rsqrt(jnp.mean(x * x, axis=-1, keepdims=True) + RMS_EPS) * g


def _silu(x):
    return x * jax.nn.sigmoid(x)


def _mm(a, b):
    return jnp.dot(a, b, preferred_element_type=F32)


def _mm_nt(a, b):
    return lax.dot_general(a, b, _NT, preferred_element_type=F32)


def _split(x):
    hi = x.astype(BF16)
    return hi, (x - hi.astype(F32)).astype(BF16)


def _mmb(a, b):
    return _mm(a.astype(BF16), b.astype(BF16))


def _norm_matmul_kernel(x_ref, g_ref, *rest, n_w):
    w_refs, o_refs = rest[:n_w], rest[n_w:]
    h = _rms(x_ref[...], g_ref[...]).astype(BF16)
    tn = o_refs[0].shape[1]
    for j, o_ref in enumerate(o_refs[:-1] if n_w == 2 else o_refs):
        o_ref[...] = _mm(h, w_refs[0][:, j * tn:(j + 1) * tn])
    if n_w == 2:
        o_refs[-1][...] = _mm(h, w_refs[1][...])


def _norm_matmul(x, g, w, tn, w_narrow=None):
    t, d = x.shape
    n = w.shape[1]
    tm = _row_tile(t)
    resident = lambda a: pl.BlockSpec(a.shape, lambda i: (0, 0), pipeline_mode=pl.Buffered(1))
    weights = [w] if w_narrow is None else [w, w_narrow]
    widths = [tn] * (n // tn) + ([] if w_narrow is None else [LANES])
    return pl.pallas_call(
        functools.partial(_norm_matmul_kernel, n_w=len(weights)),
        grid=(t // tm,),
        in_specs=[pl.BlockSpec((tm, d), lambda i: (i, 0)), pl.BlockSpec((1, d), lambda i: (0, 0))]
                 + [resident(a) for a in weights],
        out_specs=[pl.BlockSpec((tm, width), lambda i: (i, 0)) for width in widths],
        out_shape=[jax.ShapeDtypeStruct((t, width), F32) for width in widths],
        compiler_params=_params("parallel"),
        name="norm_matmul",
    )(x, g.reshape(1, d), *weights)


def _memory_kv_kernel(x_ref, g_ref, w_ref, k_ref, v_ref, kt_ref, vt_ref):
    h = _rms(x_ref[...], g_ref[...]).astype(BF16)
    d = k_ref.shape[1]
    tiles = X_DH // LANES
    for col, rows_ref, tiled_ref in ((0, k_ref, kt_ref), (d, v_ref, vt_ref)):
        y = _mm(h, w_ref[:, col:col + d])
        rows_ref[...] = y
        for hd in range(X_HEADS):
            for lt in range(tiles):
                at = (hd * tiles + lt) * LANES
                tiled_ref[pl.ds(lt * X_HEADS + hd, y.shape[0], stride=tiles * X_HEADS), :] = y[:, at:at + LANES]


def _memory_kv(mem, gains, w_kv):
    t, d = mem.shape
    n = gains.shape[0]
    tm = _row_tile(t)
    rows_per = d // LANES
    plain = pl.BlockSpec((None, tm, d), lambda li, i: (li, i, 0))
    tiled = pl.BlockSpec((None, tm * rows_per, LANES), lambda li, i: (li, i, 0))
    return pl.pallas_call(
        _memory_kv_kernel,
        grid=(n, t // tm),
        in_specs=[
            pl.BlockSpec((tm, d), lambda li, i: (i, 0)),
            pl.BlockSpec((None, 1, d), lambda li, i: (li, 0, 0)),
            pl.BlockSpec((None, d, 2 * d), lambda li, i: (li, 0, 0)),
        ],
        out_specs=[plain, plain, tiled, tiled],
        out_shape=[jax.ShapeDtypeStruct((n, t, d), F32)] * 2
                  + [jax.ShapeDtypeStruct((n, t * rows_per, LANES), F32)] * 2,
        compiler_params=_params("parallel", "parallel"),
        name="memory_kv",
    )(mem, gains.reshape(n, 1, d), w_kv)


def _matmul_postnorm_kernel(a_ref, w_ref, g_ref, r_ref, o_ref):
    y = _mm(a_ref[...].astype(BF16), w_ref[...])
    o_ref[...] = r_ref[...] + _rms(y, g_ref[...])


def _matmul_postnorm(a, w, g, res):
    t, k = a.shape
    d = w.shape[1]
    tm = _row_tile(t)
    return pl.pallas_call(
        _matmul_postnorm_kernel,
        grid=(t // tm,),
        in_specs=[
            pl.BlockSpec((tm, k), lambda i: (i, 0)),
            pl.BlockSpec((k, d), lambda i: (0, 0)),
            pl.BlockSpec((1, d), lambda i: (0, 0)),
            pl.BlockSpec((tm, d), lambda i: (i, 0)),
        ],
        out_specs=pl.BlockSpec((tm, d), lambda i: (i, 0)),
        out_shape=jax.ShapeDtypeStruct((t, d), F32),
        compiler_params=_params("parallel"),
        name="matmul_postnorm",
    )(a, w, g.reshape(1, d), res)


def _ffn_rows(x, g_pre, wg_ref, wu_ref, wd_ref, g_post, bounds=(0, D_FF)):
    h = _rms(x, g_pre).astype(BF16)
    y = None
    for lo, hi in zip(bounds[:-1], bounds[1:]):
        cols = slice(lo, hi)
        a = (_silu(_mm(h, wg_ref[:, cols])) * _mm(h, wu_ref[:, cols])).astype(BF16)
        part = _mm(a, wd_ref[cols, :])
        y = part if y is None else y + part
    return x + _rms(y, g_post)


def _ffn_kernel(x_ref, gpre_ref, wg_ref, wu_ref, wd_ref, gpost_ref, o_ref):
    o_ref[...] = _ffn_rows(x_ref[...], gpre_ref[...], wg_ref, wu_ref, wd_ref, gpost_ref[...])


def _ffn(x, g_pre, w_gu, w_down, g_post):
    t, d = x.shape
    tm = _row_tile(t)
    resident = lambda shape, col: pl.BlockSpec(shape, lambda i: (0, col), pipeline_mode=pl.Buffered(1))
    return pl.pallas_call(
        _ffn_kernel,
        grid=(t // tm,),
        in_specs=[
            pl.BlockSpec((tm, d), lambda i: (i, 0)),
            pl.BlockSpec((1, d), lambda i: (0, 0)),
            resident((d, D_FF), 0),
            resident((d, D_FF), 1),
            resident((D_FF, d), 0),
            pl.BlockSpec((1, d), lambda i: (0, 0)),
        ],
        out_specs=pl.BlockSpec((tm, d), lambda i: (i, 0)),
        out_shape=jax.ShapeDtypeStruct((t, d), F32),
        compiler_params=_params("parallel"),
        name="ffn",
    )(x, g_pre.reshape(1, d), w_gu, w_gu, w_down, g_post.reshape(1, d))


def _mem_attention(q, mk, mv):
    heads = []
    for hd in range(X_HEADS):
        sl = slice(hd * X_DH, (hd + 1) * X_DH)
        s = _mm_nt(q[:, sl], mk[:, sl])
        p = jnp.exp(s - jnp.max(s, axis=-1, keepdims=True))
        heads.append(_mm(p.astype(BF16), mv[:, sl]) / jnp.sum(p, axis=-1, keepdims=True))
    return jnp.concatenate(heads, axis=-1)


def _cross_attn_rows(y, g_pre, wq_ref, mems, wo_ref, g_post):
    h = _rms(y, g_pre).astype(BF16)
    q = (_mm(h, wq_ref[...]) * (X_DH ** -0.5)).astype(BF16)
    rows = y.shape[0] // len(mems)
    o = [_mem_attention(q[e * rows:(e + 1) * rows], mk, mv) for e, (mk, mv) in enumerate(mems)]
    o = (o[0] if len(o) == 1 else jnp.concatenate(o, axis=0)).astype(BF16)
    return y + _rms(_mm(o, wo_ref[...]), g_post)


def _cross_attn_kernel(y_ref, gpre_ref, wq_ref, mk_ref, mv_ref, wo_ref, gpost_ref, o_ref, *, tiled_mem):
    gb, tm, d = y_ref.shape
    if tiled_mem:
        tiles = X_DH // LANES
        m = mk_ref.shape[1] // (tiles * X_HEADS)
        load = lambda ref, e: jnp.concatenate(
            [ref[e, pl.ds(lt * X_HEADS + hd, m, stride=tiles * X_HEADS), :]
             for hd in range(X_HEADS) for lt in range(tiles)], axis=1).astype(BF16)
    else:
        load = lambda ref, e: ref[e].astype(BF16)
    mems = [(load(mk_ref, e), load(mv_ref, e)) for e in range(gb)]
    y = _cross_attn_rows(y_ref[...].reshape(gb * tm, d), gpre_ref[...], wq_ref, mems, wo_ref, gpost_ref[...])
    o_ref[...] = y.reshape(gb, tm, d)


def _cross_attn(y, g_pre, w_q, mem_k, mem_v, layer_k, layer_v, w_o, g_post):
    b, l, d = y.shape
    mem_block = mem_k.shape[2:]
    tm = _row_tile(l)
    gb = math.gcd(b, max(1, XATTN_ROWS // tm))
    resident = pl.BlockSpec((d, d), lambda bi, i: (0, 0), pipeline_mode=pl.Buffered(1))
    return pl.pallas_call(
        functools.partial(_cross_attn_kernel, tiled_mem=mem_block[1] != d),
        grid=(b // gb, l // tm),
        in_specs=[
            pl.BlockSpec((gb, tm, d), lambda bi, i: (bi, i, 0)),
            pl.BlockSpec((1, d), lambda bi, i: (0, 0)),
            resident,
            pl.BlockSpec((None, gb) + mem_block, lambda bi, i: (layer_k, bi, 0, 0)),
            pl.BlockSpec((None, gb) + mem_block, lambda bi, i: (layer_v, bi, 0, 0)),
            resident,
            pl.BlockSpec((1, d), lambda bi, i: (0, 0)),
        ],
        out_specs=pl.BlockSpec((gb, tm, d), lambda bi, i: (bi, i, 0)),
        out_shape=jax.ShapeDtypeStruct((b, l, d), F32),
        compiler_params=_params("parallel", "parallel"),
        name="cross_attn",
    )(y, g_pre.reshape(1, d), w_q, mem_k, mem_v, w_o, g_post.reshape(1, d))


def _layer_tail_kernel(a_ref, y_ref, gains_ref, wout_ref, wq_ref, mk_ref, mv_ref, wo_ref, wg_ref, wu_ref, wd_ref,
                       o_ref):
    gain = lambda r: gains_ref[r:r + 1, :]
    y = y_ref[...] + _rms(_mm(a_ref[...].astype(BF16), wout_ref[...]), gain(0))
    mems = [(mk_ref[...].astype(BF16), mv_ref[...].astype(BF16))]
    y = _cross_attn_rows(y, gain(1), wq_ref, mems, wo_ref, gain(2))
    o_ref[...] = _ffn_rows(y, gain(3), wg_ref, wu_ref, wd_ref, gain(4), bounds=TAIL_FFN_BOUNDS)


def _layer_tail(a, y, gains, w_out, w_q, mem_k, mem_v, layer, w_o, w_gu, w_down):
    b, l, d = y.shape
    k = a.shape[2]
    m = mem_k.shape[2]
    tm = _row_tile(l, TAIL_ROW_TILE)
    rows = lambda width: pl.BlockSpec((None, tm, width), lambda bi, i: (bi, i, 0))
    resident = lambda shape, col=0: pl.BlockSpec(shape, lambda bi, i: (0, col), pipeline_mode=pl.Buffered(1))
    mem = pl.BlockSpec((None, None, m, d), lambda bi, i: (layer, bi, 0, 0))
    return pl.pallas_call(
        _layer_tail_kernel,
        grid=(b, l // tm),
        in_specs=[
            rows(k), rows(d), resident(gains.shape),
            resident((k, d)), resident((d, d)), mem, mem, resident((d, d)),
            resident((d, D_FF), 0), resident((d, D_FF), 1), resident((D_FF, d)),
        ],
        out_specs=rows(d),
        out_shape=jax.ShapeDtypeStruct((b, l, d), F32),
        compiler_params=_params("parallel", "parallel"),
        name="layer_tail",
    )(a, y, gains, w_out, w_q, mem_k, mem_v, w_o, w_gu, w_gu, w_down)


def _gdn_kernel(alog_ref, dtb_ref, q_ref, k_ref, v_ref, z_ref, ba_ref, cb_ref, cw_ref, og_ref, s0_ref,
                o_ref, sn_ref,
                xq_ref, xk_ref, xv_ref, slot0_ref, slot1_ref, sm_ref, sb_ref, oq_ref, ob_ref, gl_ref,
                *, seq, chunk, n_chunks, group, hb):
    c_ = chunk
    padded = n_chunks * c_
    rows = min(seq, c_)
    width = hb * LANES
    heads = [pl.program_id(1) * hb + hh for hh in range(hb)]
    cols = lambda hh: slice(hh * LANES, (hh + 1) * LANES)

    for part, (x_ref, src) in enumerate(((xq_ref, q_ref), (xk_ref, k_ref), (xv_ref, v_ref))):
        x_ref[0:SUBLANES, :] = jnp.zeros((SUBLANES, width), F32)
        x_ref[SUBLANES - (CONV_W - 1):SUBLANES, :] = cb_ref[part]
        x_ref[pl.ds(SUBLANES, seq), :] = src[...]
        if padded > seq:
            x_ref[pl.ds(SUBLANES + seq, padded - seq), :] = jnp.zeros((padded - seq, width), F32)

    alog = [jnp.full((1, 1), alog_ref[h], F32) for h in heads]
    dtb = [jnp.full((1, 1), dtb_ref[h], F32) for h in heads]
    r_i =lax.broadcasted_iota(jnp.int32, (c_, c_), 0)
    c_i = lax.broadcasted_iota(jnp.int32, (c_, c_), 1)
    incl = r_i >= c_i
    strict = r_i > c_i
    tril = jnp.where(incl, 1.0, 0.0).astype(BF16)
    eye = jnp.where(r_i == c_i, 1.0, 0.0).astype(F32)
    blk = lambda size: (r_i // size) == (c_i // size)
    diag_blocks = blk(INV_BLOCK)
    merges = [jnp.logical_and(blk(2 * size), jnp.logical_not(blk(size)))
              for size in (INV_BLOCK << i for i in range(int(math.log2(c_ // INV_BLOCK))))]
    lane = lax.broadcasted_iota(jnp.int32, (c_, LANES), 1)

    def conv(win, part, hh):
        w = cw_ref[part][:, cols(hh)]
        base = SUBLANES - (CONV_W - 1)
        y = win[base:base + c_] * w[0:1]
        for j in range(1, CONV_W):
            y = y + win[base + j:base + j + c_] * w[j:j + 1]
        return _silu(y)

    def pad_rows(x):
        if rows == c_:
            return x
        return jnp.concatenate([x, jnp.zeros((c_ - rows, x.shape[1]), x.dtype)], axis=0)

    def rows_of(c):
        return c * c_ if isinstance(c, int) else pl.multiple_of(c * c_, c_)

    def gates(hh, r0, wins):
        h = heads[hh]
        qc = conv(wins[0], 0, hh)
        kc = conv(wins[1], 1, hh)
        v = conv(wins[2], 2, hh)
        q = qc * lax.rsqrt(jnp.sum(qc * qc, axis=-1, keepdims=True) + RMS_EPS) * (GDN_DK ** -0.5)
        k = kc * lax.rsqrt(jnp.sum(kc * kc, axis=-1, keepdims=True) + RMS_EPS)
        ba = pad_rows(ba_ref[pl.ds(r0, rows), :])
        b_raw = jnp.sum(jnp.where(lane == h, ba, 0.0), axis=-1, keepdims=True)
        a_raw = jnp.sum(jnp.where(lane == h + GDN_HEADS, ba, 0.0), axis=-1, keepdims=True)
        beta = jax.nn.sigmoid(b_raw)
        x = a_raw + dtb[hh]
        g = -jnp.exp(alog[hh]) * (jnp.maximum(x, 0.0) + jnp.log1p(jnp.exp(-jnp.abs(x))))
        if padded > seq:
            valid = (lax.broadcasted_iota(jnp.int32, (c_, 1), 0) + r0) < seq
            k = jnp.where(valid, k, 0.0)
            beta = jnp.where(valid, beta, 0.0)
            g = jnp.where(valid, g, 0.0)
        return q, k, v, jnp.broadcast_to(beta, (c_, LANES)), jnp.broadcast_to(g, (c_, LANES))

    def gates_group(gi, slot_ref):
        for i in range(group):
            r0 = rows_of(gi * group + i)
            for hh in range(hb):
                wins = [ref[pl.ds(r0, c_ + SUBLANES), cols(hh)] for ref in (xq_ref, xk_ref, xv_ref)]
                for n, x in enumerate(gates(hh, r0, wins)):
                    slot_ref[n, i * c_:(i + 1) * c_, cols(hh)] = x

    def prep_group(gi, slot_ref):
        hhs = [hh for hh in range(hb) for _ in range(group)]
        local = [i for _ in range(hb) for i in range(group)]
        cs = [gi * group + i for i in local]
        r0s = [rows_of(c) for c in cs]
        every = lambda f, *xs: [f(*args) for args in zip(*xs)]
        slot = lambda n: [lambda hh=hh, i=i: slot_ref[n, i * c_:(i + 1) * c_, cols(hh)] for hh, i in zip(hhs, local)]
        q, k, v, beta = slot(0), slot(1), slot(2), slot(3)
        g_hi, g_lo = zip(*every(lambda get: _split(get()), slot(4)))
        gc = every(lambda hi, lo: _mm(tril, hi) + _mm(tril, lo), g_hi, g_lo)
        kb = every(lambda get: get().astype(BF16), k)
        kk = every(_mm_nt, kb, kb)
        qk = every(lambda get, y: _mm_nt(get().astype(BF16), y), q, kb)
        decay = every(lambda x: jnp.where(incl, jnp.exp(jnp.where(incl, x[:, :c_] - x.T[:c_], 0.0)), 0.0), gc)
        low = every(lambda b_, d_, kk_: jnp.where(strict, b_()[:, :c_] * d_ * kk_, 0.0), beta, decay, kk)
        pw = every(lambda x: -jnp.where(diag_blocks, x, 0.0), low)
        t_inv = every(lambda x: eye + x, pw)
        for _ in range(int(math.log2(INV_BLOCK)) - 1):
            pw = every(_mmb, pw, pw)
            t_inv = every(lambda t_, p_: t_ + _mmb(t_, p_), t_inv, pw)
        for merge in merges:
            off = every(lambda x, t_: _mmb(jnp.where(merge, x, 0.0), t_), low, t_inv)
            t_inv = every(lambda t_, x: t_ - _mmb(t_, x), t_inv, off)
        egc = every(jnp.exp, gc)
        rhs = every(lambda b_, e_, k_, v_: jnp.concatenate([(b_() * e_) * k_(), b_() * v_()], axis=1),
                    beta, egc, k, v)
        w = every(lambda t_, x: _mmb(t_, x).astype(BF16), t_inv, rhs)
        gc_last = every(lambda x: x[c_ - 1:c_, :], gc)
        kdt = every(lambda k_, x, l_: (k_() * jnp.exp(l_ - x)).T.astype(BF16), k, gc, gc_last)
        a = every(lambda x, d_: (x * d_).astype(BF16), qk, decay)
        kw = every(_mm, kdt, w)
        aw = every(_mm, a, w)
        for i, (hh, c, r0) in enumerate(zip(hhs, cs, r0s)):
            sm_ref[hh * n_chunks + c] = kw[i][:, :GDN_DK].astype(BF16)
            sb_ref[hh * n_chunks + c] = kw[i][:, GDN_DK:]
            oq_ref[pl.ds(r0, c_), cols(hh)] = (q[i]() * egc[i] - aw[i][:, :GDN_DK]).astype(BF16)
            ob_ref[pl.ds(r0, c_), cols(hh)] = aw[i][:, GDN_DK:]
            gl_ref[pl.ds(hh * n_chunks + c, 1), :] = jnp.exp(gc_last[i])

    og = og_ref[...]

    def scan(c, states):
        r0 = rows_of(c)
        at = [hh * n_chunks + c for hh in range(hb)]
        sb = [s.astype(BF16) for s in states]
        upd = [_mm(sm_ref[i], x) for i, x in zip(at, sb)]
        o = [_mm(oq_ref[pl.ds(r0, c_), cols(hh)], sb[hh]) + ob_ref[pl.ds(r0, c_), cols(hh)] for hh in range(hb)]
        for hh in range(hb):
            z = z_ref[pl.ds(r0, rows), cols(hh)]
            o_ref[pl.ds(r0, rows), cols(hh)] = _rms(o[hh][:rows], og) * _silu(z)
        return tuple(gl_ref[pl.ds(i, 1), :] * s - u + sb_ref[i] for i, s, u in zip(at, states, upd))

    def pipelined(g2, carry):
        prep_group(2 * g2, slot0_ref)
        gates_group(2 * g2 + 1, slot1_ref)
        prep_group(2 * g2 + 1, slot1_ref)
        gates_group(2 * g2 + 2, slot0_ref)
        return carry

    n_groups = n_chunks // group
    pairs = (n_groups - 1) // 2
    gates_group(0, slot0_ref)
    lax.fori_loop(0, pairs, pipelined, 0)
    prep_group(2 * pairs, slot0_ref)
    if n_groups - 2 * pairs == 2:
        gates_group(2 * pairs + 1, slot1_ref)
        prep_group(2 * pairs + 1, slot1_ref)
    final = lax.fori_loop(0, n_chunks, scan, tuple(s0_ref[hh] for hh in range(hb)),
                          unroll=math.gcd(SCAN_UNROLL, n_chunks))
    for hh in range(hb):
        sn_ref[hh] = final[hh]


def _gdn_core(qkvz, ba, conv_buf, conv_w, a_log, dt_bias, o_gain, s0):
    b, seq, _ = qkvz[0].shape
    chunk = GDN_CHUNK
    n_chunks = -(-seq // chunk)
    padded = n_chunks * chunk
    group = math.gcd(GDN_GROUP // GDN_MIN_HEADS, n_chunks)
    hb = math.gcd(GDN_GROUP // group, GDN_HEADS)
    width = hb * LANES
    kern = functools.partial(_gdn_kernel, seq=seq, chunk=chunk, n_chunks=n_chunks, group=group, hb=hb)
    head_cols = pl.BlockSpec((None, seq, width), lambda bi, hi: (bi, 0, hi))
    smem = pl.BlockSpec(memory_space=pltpu.SMEM)
    return pl.pallas_call(
        kern,
        grid=(b, GDN_HEADS // hb),
        in_specs=[
            smem, smem,
            head_cols, head_cols, head_cols, head_cols,
            pl.BlockSpec((None, seq, LANES), lambda bi, hi: (bi, 0, 0)),
            pl.BlockSpec((None, 3, CONV_W - 1, width), lambda bi, hi: (bi, 0, 0, hi)),
            pl.BlockSpec((3, CONV_W, width), lambda bi, hi: (0, 0, hi)),
            pl.BlockSpec((1, GDN_DV), lambda bi, hi: (0, 0)),
            pl.BlockSpec((None, hb, GDN_DK, GDN_DV), lambda bi, hi: (bi, hi, 0, 0)),
        ],
        out_specs=[
            pl.BlockSpec((None, seq, width), lambda bi, hi: (bi, 0, hi)),
            pl.BlockSpec((None, hb, GDN_DK, GDN_DV), lambda bi, hi: (bi, hi, 0, 0)),
        ],
        out_shape=[
            jax.ShapeDtypeStruct((b, seq, GDN_VW), F32),
            jax.ShapeDtypeStruct((b, GDN_HEADS, GDN_DK, GDN_DV), F32),
        ],
        scratch_shapes=[
            pltpu.VMEM((padded + SUBLANES, width), F32),
            pltpu.VMEM((padded + SUBLANES, width), F32),
            pltpu.VMEM((padded + SUBLANES, width), F32),
            pltpu.VMEM((5, group * chunk, width), F32),
            pltpu.VMEM((5, group * chunk, width), F32),
            pltpu.VMEM((hb * n_chunks, GDN_DK, GDN_DK), BF16),
            pltpu.VMEM((hb * n_chunks, GDN_DK, GDN_DV), F32),
            pltpu.VMEM((padded, width), BF16),
            pltpu.VMEM((padded, width), F32),
            pltpu.VMEM((hb * n_chunks, LANES), F32),
        ],
        compiler_params=_params("parallel", "parallel"),
        name="gdn_core",
    )(a_log, dt_bias, *qkvz, ba, conv_buf, conv_w, o_gain.reshape(1, GDN_DV), s0)


def _rel_bucket(dist):
    max_exact = REL_BUCKETS // 2
    n = jnp.maximum(dist, 0)
    large = max_exact + (jnp.log(jnp.maximum(n, 1).astype(F32) / max_exact)
                         / math.log(REL_MAX_DIST / max_exact) * (REL_BUCKETS - max_exact)).astype(jnp.int32)
    large = jnp.minimum(large, REL_BUCKETS - 1)
    return jnp.where(n < max_exact, n, large)


def _bias_tiles_kernel(rb_ref, o_ref, *, t):
    h = pl.program_id(0)
    d = pl.program_id(1)
    r = lax.broadcasted_iota(jnp.int32, (t, t), 0)
    c = lax.broadcasted_iota(jnp.int32, (t, t), 1)
    dist = d * t + r - c
    bucket = _rel_bucket(dist)
    bias = jnp.zeros((t, t), F32)
    for b in range(REL_BUCKETS):
        bias = jnp.where(bucket == b, rb_ref[b, h], bias)
    o_ref[...] = jnp.where(dist >= 0, bias - rb_ref[REL_BUCKETS - 1, h], NEG_INF)


def _bias_tiles(rel_bias, t):
    assert t >= REL_MAX_DIST
    return pl.pallas_call(
        functools.partial(_bias_tiles_kernel, t=t),
        grid=(DA_HEADS, 2),
        in_specs=[pl.BlockSpec(memory_space=pltpu.SMEM)],
        out_specs=pl.BlockSpec((None, None, t, t), lambda h, d: (h, d, 0, 0)),
        out_shape=jax.ShapeDtypeStruct((DA_HEADS, 2, t, t), F32),
        compiler_params=_params("parallel", "parallel"),
        name="rel_bias_tiles",
    )(rel_bias)


def _lambda(lq1_ref, lk1_ref, lq2_ref, lk2_ref, lam_init):
    dot = lambda a, b: jnp.sum(a[...] * b[...], axis=-1, keepdims=True)
    return jnp.exp(dot(lq1_ref, lk1_ref)) - jnp.exp(dot(lq2_ref, lk2_ref)) + lam_init


def _da_prompt_kernel(q_ref, k_ref, v_ref, b_ref, lq1_ref, lk1_ref, lq2_ref, lk2_ref, sg_ref, o_ref,
                      kb_ref, vb_ref, s_ref, *, t, lam_init):
    seq = q_ref.shape[0]
    kb_ref[...] = k_ref[...].astype(BF16)
    vb_ref[:, :DA_VD] = v_ref[...].astype(BF16)
    vb_ref[:, DA_VD:] = jnp.ones((seq, LANES), BF16)
    lam = _lambda(lq1_ref, lk1_ref, lq2_ref, lk2_ref, lam_init)
    sg = sg_ref[...]
    lane = lax.broadcasted_iota(jnp.int32, (t, LANES), 1)
    stack = lambda x: jnp.concatenate([x, x], axis=0)
    for qi in range(seq // t):
        q = q_ref[qi * t:(qi + 1) * t, :] * (DA_DH ** -0.5)
        qq = jnp.concatenate([jnp.where(lane < DA_DH, q, 0.0), jnp.where(lane >= DA_DH, q, 0.0)], axis=0)
        n_k = (qi + 1) * t
        s_ref[:, :n_k] = _mm_nt(qq.astype(BF16), kb_ref[:n_k, :])
        s_ref[:, qi * t:n_k] += stack(b_ref[0])
        if qi >= 1:
            s_ref[:, (qi - 1) * t:qi * t] += stack(b_ref[1])
        s = s_ref[:, :n_k]
        p = jnp.exp(s - jnp.max(s, axis=-1, keepdims=True)).astype(BF16)
        acc = _mm(p, vb_ref[:n_k, :])
        a = acc[:, :DA_VD] / acc[:, DA_VD:]
        o = a[:t] - lam * a[t:]
        o_ref[qi * t:(qi + 1) * t, :] = _rms(o, sg) * (1.0 - lam_init)


def _da_prompt(qkv, bias_tiles, lq1, lk1, lq2, lk2, sub_gain, lam_init):
    b, seq, _ = qkv[0].shape
    t = bias_tiles.shape[-1]
    vec = lambda x: x.reshape(1, -1)
    vspec = lambda w: pl.BlockSpec((1, w), lambda bi, hi: (0, 0))
    head = pl.BlockSpec((None, seq, LANES), lambda bi, hi: (bi, 0, hi))
    return pl.pallas_call(
        functools.partial(_da_prompt_kernel, t=t, lam_init=lam_init),
        grid=(b, DA_HEADS),
        in_specs=[
            head, head, head,
            pl.BlockSpec((None, 2, t, t), lambda bi, hi: (hi, 0, 0, 0)),
            vspec(DA_DH), vspec(DA_DH), vspec(DA_DH), vspec(DA_DH), vspec(DA_VD),
        ],
        out_specs=pl.BlockSpec((None, seq, LANES), lambda bi, hi: (bi, 0, hi)),
        out_shape=jax.ShapeDtypeStruct((b, seq, DA_HEADS * DA_VD), F32),
        scratch_shapes=[pltpu.VMEM((seq, DA_DH * 2), BF16), pltpu.VMEM((seq, DA_VD + LANES), BF16),
                        pltpu.VMEM((2 * t, seq), F32)],
        compiler_params=_params("parallel", "parallel"),
        name="da_prompt",
    )(*qkv, bias_tiles, vec(lq1), vec(lk1), vec(lq2), vec(lk2), vec(sub_gain))


def _da_sample_kernel(pt_ref, q_ref, kn_ref, vn_ref, rbt_ref, lq1_ref, lk1_ref, lq2_ref, lk2_ref, sg_ref, *rest,
                      pages, n_pages, lq, lam_init):
    del pt_ref
    k_refs = rest[:pages]
    v_refs = rest[pages:2 * pages]
    o_ref, qbd_ref, m_ref, l_ref, acc_ref, s_ref = rest[2 * pages:]
    step = pl.program_id(1)
    rows = DA_HEADS * 2 * lq
    past_len = n_pages * PAGE_SIZE

    @pl.when(step == 0)
    def _():
        q = q_ref[...] * (DA_DH ** -0.5)
        qrep = jnp.concatenate([q] * (DA_HEADS * 2), axis=0)
        r_hc = lax.broadcasted_iota(jnp.int32, (rows, DA_HW), 0) // lq
        c_hc = lax.broadcasted_iota(jnp.int32, (rows, DA_HW), 1) // DA_DH
        qbd_ref[...] = jnp.where(r_hc == c_hc, qrep, 0.0).astype(BF16)
        m_ref[...] = jnp.full(m_ref.shape, NEG_INF, F32)
        l_ref[...] = jnp.zeros_like(l_ref)
        acc_ref[...] = jnp.zeros_like(acc_ref)

    def near_bias(k_start, n_valid):
        row = lax.broadcasted_iota(jnp.int32, (rows, PAGE_SIZE), 0)
        col = lax.broadcasted_iota(jnp.int32, (rows, PAGE_SIZE), 1)
        dist = past_len + row % lq - (k_start + col)
        bucket = _rel_bucket(dist)
        bias = jnp.zeros((rows, PAGE_SIZE), F32)
        for b in range(REL_BUCKETS):
            bias = jnp.where(bucket == b, rbt_ref[:, b:b + 1], bias)
        bias = bias - rbt_ref[:, REL_BUCKETS - 1:REL_BUCKETS]
        return jnp.where(jnp.logical_and(dist >= 0, col < n_valid), bias, NEG_INF)

    def attend(s_blocks, v_blocks):
        s_max = functools.reduce(jnp.maximum, s_blocks)
        m_prev = m_ref[...]
        m_new = jnp.maximum(m_prev, jnp.max(s_max, axis=-1, keepdims=True))
        alpha = jnp.exp(m_prev - m_new)
        p = [jnp.exp(s - m_new) for s in s_blocks]
        l_ref[...] = alpha * l_ref[...] + jnp.sum(functools.reduce(jnp.add, p), axis=-1, keepdims=True)
        pb = jnp.concatenate([x.astype(BF16) for x in p], axis=1)
        pv = [_mm(pb[h * 2 * lq:(h + 1) * 2 * lq], jnp.concatenate([v[h] for v in v_blocks], axis=0))
              for h in range(DA_HEADS)]
        acc_ref[...] = alpha * acc_ref[...] + jnp.concatenate(pv, axis=0)
        m_ref[...] = m_new

    n_tiles = DA_HW // MXU_TILE
    tile_rows = rows // n_tiles
    q_tiles = [qbd_ref[j * tile_rows:(j + 1) * tile_rows, j * MXU_TILE:(j + 1) * MXU_TILE] for j in range(n_tiles)]
    for r in range(pages):
        kt = k_refs[r][...].astype(BF16)
        s_ref[r] = jnp.concatenate(
            [_mm(q_tiles[j], kt[j * MXU_TILE:(j + 1) * MXU_TILE, :]) for j in range(n_tiles)], axis=0)

    last = step == pl.num_programs(1) - 1

    @pl.when(last)
    def _():
        s_ref[pages - 1] += near_bias(past_len - PAGE_SIZE, PAGE_SIZE)

    attend([s_ref[r] for r in range(pages)],
           [[v_refs[r][pl.ds(h, PAGE_SIZE, stride=DA_HEADS), :].astype(BF16) for h in range(DA_HEADS)]
            for r in range(pages)])

    @pl.when(last)
    def _():
        zeros = jnp.zeros((PAGE_SIZE - lq, DA_HW), F32)
        kn = jnp.concatenate([kn_ref[...], zeros], axis=0).astype(BF16)
        vn = jnp.concatenate([vn_ref[...], zeros], axis=0).astype(BF16)
        attend([_mm_nt(qbd_ref[...], kn) + near_bias(past_len, lq)],
               [[vn[:, h * DA_VD:(h + 1) * DA_VD] for h in range(DA_HEADS)]])
        a = acc_ref[...] / l_ref[...]
        lam = _lambda(lq1_ref, lk1_ref, lq2_ref, lk2_ref, lam_init)
        sg = sg_ref[...]
        heads = []
        for h in range(DA_HEADS):
            o = a[h * 2 * lq:h * 2 * lq + lq] - lam * a[h * 2 * lq + lq:(h + 1) * 2 * lq]
            heads.append(_rms(o, sg) * (1.0 - lam_init))
        o_ref[...] = jnp.concatenate(heads, axis=-1)


def _da_sample(qkv, cache_k, cache_v, layer, page_table, rel_bias, lq1, lk1, lq2, lk2, sub_gain, lam_init):
    b, lq, _ = qkv[0].shape
    assert PAGE_SIZE >= REL_MAX_DIST
    n_pages = page_table.shape[1]
    pages = math.gcd(PAGES_PER_STEP, n_pages)
    rows = DA_HEADS * 2 * lq
    assert rows == LANES and lq == SUBLANES
    rbt = jnp.repeat(rel_bias.T, 2 * lq, axis=0)
    vec = lambda x: x.reshape(1, -1)
    vspec = lambda w: pl.BlockSpec((1, w), lambda bi, si, pt: (0, 0))
    new = pl.BlockSpec((None, lq, DA_HW), lambda bi, si, pt: (bi, 0, 0))
    page = lambda r, shape: pl.BlockSpec((None, None) + shape,
                                         lambda bi, si, pt: (layer, pt[bi, si * pages + r], 0, 0))
    k_shape, v_shape = cache_k.shape[2:], cache_v.shape[2:]
    grid_spec = pltpu.PrefetchScalarGridSpec(
        num_scalar_prefetch=1,
        grid=(b, n_pages // pages),
        in_specs=[new, new, new,
                  pl.BlockSpec((rows, REL_BUCKETS), lambda bi, si, pt: (0, 0)),
                  vspec(DA_DH), vspec(DA_DH), vspec(DA_DH), vspec(DA_DH), vspec(DA_VD)]
                 + [page(r, k_shape) for r in range(pages)] + [page(r, v_shape) for r in range(pages)],
        out_specs=pl.BlockSpec((None, lq, DA_HW), lambda bi, si, pt: (bi, 0, 0)),
        scratch_shapes=[pltpu.VMEM((rows, DA_HW), BF16), pltpu.VMEM((rows, LANES), F32),
                        pltpu.VMEM((rows, LANES), F32), pltpu.VMEM((rows, DA_VD), F32),
                        pltpu.VMEM((pages, rows, PAGE_SIZE), F32)],
    )
    return pl.pallas_call(
        functools.partial(_da_sample_kernel, pages=pages, n_pages=n_pages, lq=lq, lam_init=lam_init),
        grid_spec=grid_spec,
        out_shape=jax.ShapeDtypeStruct((b, lq, DA_HW), F32),
        compiler_params=_params("parallel", "arbitrary"),
        name="da_sample",
    )(page_table, *qkv, rbt, vec(lq1), vec(lk1), vec(lq2), vec(lk2), vec(sub_gain),
      *([cache_k] * pages), *([cache_v] * pages))


def _gdn_layer(y, conv_buf, s0, g_pre, w_main, w_ba, conv_w, a_log, dt_bias, o_gain):
    b, seq, d = y.shape
    yf = y.reshape(b * seq, d)
    *qkvz, ba = _norm_matmul(yf, g_pre, w_main, D_MODEL, w_narrow=w_ba)
    qkvz = [x.reshape(b, seq, D_MODEL) for x in qkvz]
    ba = ba.reshape(b, seq, LANES)
    cb = jnp.transpose(conv_buf.reshape(b, CONV_W - 1, 3, D_MODEL), (0, 2, 1, 3))
    o, s_new = _gdn_core(qkvz, ba, cb, conv_w, a_log, dt_bias, o_gain, s0)
    tail = jnp.concatenate([x[:, -(CONV_W - 1):] for x in qkvz[:3]], axis=-1)
    pre = jnp.concatenate([conv_buf, tail], axis=1)
    new_buf = pre[:, -(CONV_W - 1):]
    return o, new_buf, s_new


def _layer_rest(o, y, g_pre, g_post, w_out, w_q, w_o, w_gu, w_down, mem):
    b, seq, d = y.shape
    mem_k, mem_v, layer = mem
    if mem_k.shape[-1] == d and seq % TAIL_ROW_TILE == 0:
        gains = jnp.stack([g_post[0], g_pre[1], g_post[1], g_pre[2], g_post[2]])
        return _layer_tail(o, y, gains, w_out, w_q, mem_k, mem_v, layer, w_o, w_gu, w_down)
    y = _matmul_postnorm(o.reshape(b * seq, -1), w_out, g_post[0], y.reshape(b * seq, d)).reshape(b, seq, d)
    y = _cross_attn(y, g_pre[1], w_q, mem_k, mem_v, layer, layer, w_o, g_post[1])
    return _ffn(y.reshape(b * seq, d), g_pre[2], w_gu, w_down, g_post[2]).reshape(b, seq, d)


def kernel(x_prompt, x_sample, state_gdn, state_conv, cache_k, cache_v, cache_mem_k, cache_mem_v, page_table,
           mem_prompt, rel_bias, norm_pre, norm_post, gdn_w_in, gdn_conv_w, gdn_a_log, gdn_dt_bias, gdn_o_gain,
           gdn_w_out, da_w_in, da_lq1, da_lk1, da_lq2, da_lk2, da_sub_gain, da_w_out, mem_gain, w_xq, w_xkv, w_xo,
           ffn_w_gu, ffn_w_down):
    depth = norm_pre.shape[0]
    b_p, l_p, d = x_prompt.shape
    b_s, l_s, _ = x_sample.shape
    m_len = mem_prompt.shape[1]
    yp, ys = x_prompt, x_sample
    mem_flat = mem_prompt.reshape(b_p * m_len, d)
    ck = jnp.transpose(cache_k, (0, 1, 3, 4, 5, 2)).reshape(cache_k.shape[:2] + (DA_HW, PAGE_SIZE))
    cv = cache_v.reshape(cache_v.shape[:2] + (PAGE_SIZE * DA_HEADS, DA_VD))
    mem_rows = lambda x: jnp.transpose(
        x.reshape(x.shape[:4] + (X_DH // LANES, LANES)), (0, 1, 2, 4, 3, 5)).reshape(x.shape[:2] + (-1, LANES))
    cmk, cmv = mem_rows(cache_mem_k), mem_rows(cache_mem_v)
    bias_tiles = _bias_tiles(rel_bias, min(ATTN_TILE, l_p))
    mk_rows, mv_rows, mk_tiled, mv_tiled = _memory_kv(mem_flat, mem_gain, w_xkv.astype(BF16))
    mk_rows, mv_rows = (x.reshape(depth, b_p, m_len, d) for x in (mk_rows, mv_rows))
    mem_out = lambda x: jnp.transpose(
        x.reshape(depth, b_p, m_len, X_DH // LANES, X_HEADS, LANES), (0, 1, 2, 4, 3, 5)
    ).reshape(depth, b_p, m_len, X_HEADS, X_DH)
    gdn_p, conv_p, gdn_s, conv_s = [], [], [], []
    k_p, v_p, k_s, v_s = [], [], [], []
    for i in range(depth):
        j = i // N_MIXERS
        g_pre, g_post = norm_pre[i], norm_post[i]
        if i % N_MIXERS == 0:
            w_in = gdn_w_in[j]
            n_main = CONV_DIM + GDN_VW
            w_main = w_in[:, :n_main].astype(BF16)
            w_ba = jnp.pad(w_in[:, n_main:], ((0, 0), (0, LANES - 2 * GDN_HEADS))).astype(BF16)
            conv_w = jnp.transpose(gdn_conv_w[j].reshape(CONV_W, 3, D_MODEL), (1, 0, 2))
            w_out = gdn_w_out[j].astype(BF16)
            gw = (g_pre[0], w_main, w_ba, conv_w, gdn_a_log[j], gdn_dt_bias[j], gdn_o_gain[j])
            buf0 = jnp.zeros((b_p, CONV_W - 1, CONV_DIM), F32)
            s0 = jnp.zeros((b_p, GDN_HEADS, GDN_DK, GDN_DV), F32)
            op, cb, st = _gdn_layer(yp, buf0, s0, *gw)
            gdn_p.append(st)
            conv_p.append(cb)
            os_, cb, st = _gdn_layer(ys, state_conv[j], state_gdn[j], *gw)
            gdn_s.append(st)
            conv_s.append(cb)
        else:
            lam_init = 0.8 - 0.6 * math.exp(-0.3 * i)
            w_in = da_w_in[j].astype(BF16)
            w_out = da_w_out[j].astype(BF16)
            lw = (da_lq1[j], da_lk1[j], da_lq2[j], da_lk2[j], da_sub_gain[j], lam_init)
            qkv = [x.reshape(b_p, l_p, DA_HW) for x in _norm_matmul(yp.reshape(b_p * l_p, d), g_pre[0], w_in, DA_HW)]
            op = _da_prompt(qkv, bias_tiles, *lw)
            k_p.append(qkv[1].reshape(b_p, l_p, DA_HEADS, 2, DA_DH))
            v_p.append(qkv[2].reshape(b_p, l_p, DA_HEADS, DA_VD))
            qkv = [x.reshape(b_s, l_s, DA_HW) for x in _norm_matmul(ys.reshape(b_s * l_s, d), g_pre[0], w_in, DA_HW)]
            os_ = _da_sample(qkv, ck, cv, j, page_table, rel_bias, *lw)
            k_s.append(qkv[1].reshape(b_s, l_s, DA_HEADS, 2, DA_DH))
            v_s.append(qkv[2].reshape(b_s, l_s, DA_HEADS, DA_VD))
        rest = (g_pre, g_post, w_out, w_xq[i].astype(BF16), w_xo[i].astype(BF16),
                ffn_w_gu[i].astype(BF16), ffn_w_down[i].astype(BF16))
        yp = _layer_rest(op, yp, *rest, (mk_rows, mv_rows, i))
        ys = _layer_rest(os_, ys, *rest, (cmk, cmv, i))
    return (yp, ys,
            jnp.stack(gdn_p), jnp.stack(conv_p), jnp.stack(k_p), jnp.stack(v_p),
            mem_out(mk_tiled), mem_out(mv_tiled),
            jnp.stack(gdn_s), jnp.stack(conv_s), jnp.stack(k_s), jnp.stack(v_s))
```

```python
import functools
import math

import jax
import jax.numpy as jnp
from jax import lax
from jax.experimental import pallas as pl
from jax.experimental.pallas import tpu as pltpu

F32 = jnp.float32
BF16 = jnp.bfloat16

D_MODEL = 1024
N_MIXERS = 2
GDN_HEADS = 8
GDN_DK = 128
GDN_DV = 128
CONV_W = 4
GDN_CHUNK = 64
INV_BLOCK = 8
GDN_GROUP = 8
GDN_MIN_HEADS = 2
SCAN_UNROLL = 4
GDN_QK = GDN_HEADS * GDN_DK
GDN_VW = GDN_HEADS * GDN_DV
CONV_DIM = 2 * GDN_QK + GDN_VW
DA_HEADS = 8
DA_DH = 64
DA_VD = 2 * DA_DH
DA_HW = DA_HEADS * 2 * DA_DH
PAGE_SIZE = 128
REL_BUCKETS = 32
REL_MAX_DIST = 128
X_HEADS = 4
X_DH = D_MODEL // X_HEADS
D_FF = -(-8 * D_MODEL // (3 * 256)) * 256
RMS_EPS = 1e-6
NEG_INF = -1e30

LANES = 128
SUBLANES = 8
ROW_TILE = 512
TAIL_ROW_TILE = 512
MXU_TILE = 256
TAIL_FFN_BOUNDS = (0, -(-D_FF // (2 * MXU_TILE)) * MXU_TILE, D_FF)
ATTN_TILE = 256
PAGES_PER_STEP = 16
XATTN_ROWS = 64
VMEM_LIMIT = 56 * 1024 * 1024

_NT = (((1,), (1,)), ((), ()))


def _params(*sem):
    return pltpu.CompilerParams(dimension_semantics=sem, vmem_limit_bytes=VMEM_LIMIT)


def _row_tile(t, rows=ROW_TILE):
    return rows if t % rows == 0 else t


def _mixer_dtype(seq):
    return BF16 if seq % (2 * SUBLANES) == 0 else F32


def _rms(x, g):
    return x * lax.rsqrt(jnp.mean(x * x, axis=-1, keepdims=True) + RMS_EPS) * g


def _silu(x):
    return x * jax.nn.sigmoid(x)


def _mm(a, b):
    return jnp.dot(a, b, preferred_element_type=F32)


def _mm_nt(a, b):
    return lax.dot_general(a, b, _NT, preferred_element_type=F32)


def _split(x):
    hi = x.astype(BF16)
    return hi, (x - hi.astype(F32)).astype(BF16)


def _mmb(a, b):
    return _mm(a.astype(BF16), b.astype(BF16))


def _norm_matmul_kernel(x_ref, g_ref, *rest, n_w):
    w_refs, o_refs = rest[:n_w], rest[n_w:]
    h = _rms(x_ref[...], g_ref[...]).astype(BF16)
    tn = o_refs[0].shape[1]
    for j, o_ref in enumerate(o_refs[:-1] if n_w == 2 else o_refs):
        o_ref[...] = _mm(h, w_refs[0][:, j * tn:(j + 1) * tn])
    if n_w == 2:
        o_refs[-1][...] = _mm(h, w_refs[1][...])


def _norm_matmul(x, g, w, tn, w_narrow=None):
    t, d = x.shape
    n = w.shape[1]
    tm = _row_tile(t)
    resident = lambda a: pl.BlockSpec(a.shape, lambda i: (0, 0), pipeline_mode=pl.Buffered(1))
    weights = [w] if w_narrow is None else [w, w_narrow]
    widths = [tn] * (n // tn) + ([] if w_narrow is None else [LANES])
    return pl.pallas_call(
        functools.partial(_norm_matmul_kernel, n_w=len(weights)),
        grid=(t // tm,),
        in_specs=[pl.BlockSpec((tm, d), lambda i: (i, 0)), pl.BlockSpec((1, d), lambda i: (0, 0))]
                 + [resident(a) for a in weights],
        out_specs=[pl.BlockSpec((tm, width), lambda i: (i, 0)) for width in widths],
        out_shape=[jax.ShapeDtypeStruct((t, width), F32) for width in widths],
        compiler_params=_params("parallel"),
        name="norm_matmul",
    )(x, g.reshape(1, d), *weights)


def _memory_kv_kernel(x_ref, g_ref, w_ref, k_ref, v_ref, kt_ref, vt_ref):
    h = _rms(x_ref[...], g_ref[...]).astype(BF16)
    d = k_ref.shape[1]
    tiles = X_DH // LANES
    for col, rows_ref, tiled_ref in ((0, k_ref, kt_ref), (d, v_ref, vt_ref)):
        y = _mm(h, w_ref[:, col:col + d])
        rows_ref[...] = y
        for hd in range(X_HEADS):
            for lt in range(tiles):
                at = (hd * tiles + lt) * LANES
                tiled_ref[pl.ds(lt * X_HEADS + hd, y.shape[0], stride=tiles * X_HEADS), :] = y[:, at:at + LANES]


def _memory_kv(mem, gains, w_kv):
    t, d = mem.shape
    n = gains.shape[0]
    tm = _row_tile(t)
    rows_per = d // LANES
    plain = pl.BlockSpec((None, tm, d), lambda li, i: (li, i, 0))
    tiled = pl.BlockSpec((None, tm * rows_per, LANES), lambda li, i: (li, i, 0))
    return pl.pallas_call(
        _memory_kv_kernel,
        grid=(n, t // tm),
        in_specs=[
            pl.BlockSpec((tm, d), lambda li, i: (i, 0)),
            pl.BlockSpec((None, 1, d), lambda li, i: (li, 0, 0)),
            pl.BlockSpec((None, d, 2 * d), lambda li, i: (li, 0, 0)),
        ],
        out_specs=[plain, plain, tiled, tiled],
        out_shape=[jax.ShapeDtypeStruct((n, t, d), F32)] * 2
                  + [jax.ShapeDtypeStruct((n, t * rows_per, LANES), F32)] * 2,
        compiler_params=_params("parallel", "parallel"),
        name="memory_kv",
    )(mem, gains.reshape(n, 1, d), w_kv)


def _matmul_postnorm_kernel(a_ref, w_ref, g_ref, r_ref, o_ref):
    y = _mm(a_ref[...].astype(BF16), w_ref[...])
    o_ref[...] = r_ref[...] + _rms(y, g_ref[...])


def _matmul_postnorm(a, w, g, res):
    t, k = a.shape
    d = w.shape[1]
    tm = _row_tile(t)
    return pl.pallas_call(
        _matmul_postnorm_kernel,
        grid=(t // tm,),
        in_specs=[
            pl.BlockSpec((tm, k), lambda i: (i, 0)),
            pl.BlockSpec((k, d), lambda i: (0, 0)),
            pl.BlockSpec((1, d), lambda i: (0, 0)),
            pl.BlockSpec((tm, d), lambda i: (i, 0)),
        ],
        out_specs=pl.BlockSpec((tm, d), lambda i: (i, 0)),
        out_shape=jax.ShapeDtypeStruct((t, d), F32),
        compiler_params=_params("parallel"),
        name="matmul_postnorm",
    )(a, w, g.reshape(1, d), res)


def _ffn_rows(x, g_pre, wg_ref, wu_ref, wd_ref, g_post, bounds=(0, D_FF)):
    h = _rms(x, g_pre).astype(BF16)
    y = None
    for lo, hi in zip(bounds[:-1], bounds[1:]):
        cols = slice(lo, hi)
        a = (_silu(_mm(h, wg_ref[:, cols])) * _mm(h, wu_ref[:, cols])).astype(BF16)
        part = _mm(a, wd_ref[cols, :])
        y = part if y is None else y + part
    return x + _rms(y, g_post)


def _ffn_kernel(x_ref, gpre_ref, wg_ref, wu_ref, wd_ref, gpost_ref, o_ref):
    o_ref[...] = _ffn_rows(x_ref[...], gpre_ref[...], wg_ref, wu_ref, wd_ref, gpost_ref[...])


def _ffn(x, g_pre, w_gu, w_down, g_post):
    t, d = x.shape
    tm = _row_tile(t)
    resident = lambda shape, col: pl.BlockSpec(shape, lambda i: (0, col), pipeline_mode=pl.Buffered(1))
    return pl.pallas_call(
        _ffn_kernel,
        grid=(t // tm,),
        in_specs=[
            pl.BlockSpec((tm, d), lambda i: (i, 0)),
            pl.BlockSpec((1, d), lambda i: (0, 0)),
            resident((d, D_FF), 0),
            resident((d, D_FF), 1),
            resident((D_FF, d), 0),
            pl.BlockSpec((1, d), lambda i: (0, 0)),
        ],
        out_specs=pl.BlockSpec((tm, d), lambda i: (i, 0)),
        out_shape=jax.ShapeDtypeStruct((t, d), F32),
        compiler_params=_params("parallel"),
        name="ffn",
    )(x, g_pre.reshape(1, d), w_gu, w_gu, w_down, g_post.reshape(1, d))


def _mem_attention(q, mk, mv):
    heads = []
    for hd in range(X_HEADS):
        sl = slice(hd * X_DH, (hd + 1) * X_DH)
        s = _mm_nt(q[:, sl], mk[:, sl])
        p = jnp.exp(s - jnp.max(s, axis=-1, keepdims=True))
        heads.append(_mm(p.astype(BF16), mv[:, sl]) / jnp.sum(p, axis=-1, keepdims=True))
    return jnp.concatenate(heads, axis=-1)


def _cross_attn_rows(y, g_pre, wq_ref, mems, wo_ref, g_post):
    h = _rms(y, g_pre).astype(BF16)
    q = (_mm(h, wq_ref[...]) * (X_DH ** -0.5)).astype(BF16)
    rows = y.shape[0] // len(mems)
    o = [_mem_attention(q[e * rows:(e + 1) * rows], mk, mv) for e, (mk, mv) in enumerate(mems)]
    o = (o[0] if len(o) == 1 else jnp.concatenate(o, axis=0)).astype(BF16)
    return y + _rms(_mm(o, wo_ref[...]), g_post)


def _cross_attn_kernel(y_ref, gpre_ref, wq_ref, mk_ref, mv_ref, wo_ref, gpost_ref, o_ref, *, tiled_mem):
    gb, tm, d = y_ref.shape
    if tiled_mem:
        tiles = X_DH // LANES
        m = mk_ref.shape[1] // (tiles * X_HEADS)
        load = lambda ref, e: jnp.concatenate(
            [ref[e, pl.ds(lt * X_HEADS + hd, m, stride=tiles * X_HEADS), :]
             for hd in range(X_HEADS) for lt in range(tiles)], axis=1).astype(BF16)
    else:
        load = lambda ref, e: ref[e].astype(BF16)
    mems = [(load(mk_ref, e), load(mv_ref, e)) for e in range(gb)]
    y = _cross_attn_rows(y_ref[...].reshape(gb * tm, d), gpre_ref[...], wq_ref, mems, wo_ref, gpost_ref[...])
    o_ref[...] = y.reshape(gb, tm, d)


def _cross_attn(y, g_pre, w_q, mem_k, mem_v, layer_k, layer_v, w_o, g_post):
    b, l, d = y.shape
    mem_block = mem_k.shape[2:]
    tm = _row_tile(l)
    gb = math.gcd(b, max(1, XATTN_ROWS // tm))
    resident = pl.BlockSpec((d, d), lambda bi, i: (0, 0), pipeline_mode=pl.Buffered(1))
    return pl.pallas_call(
        functools.partial(_cross_attn_kernel, tiled_mem=mem_block[1] != d),
        grid=(b // gb, l // tm),
        in_specs=[
            pl.BlockSpec((gb, tm, d), lambda bi, i: (bi, i, 0)),
            pl.BlockSpec((1, d), lambda bi, i: (0, 0)),
            resident,
            pl.BlockSpec((None, gb) + mem_block, lambda bi, i: (layer_k, bi, 0, 0)),
            pl.BlockSpec((None, gb) + mem_block, lambda bi, i: (layer_v, bi, 0, 0)),
            resident,
            pl.BlockSpec((1, d), lambda bi, i: (0, 0)),
        ],
        out_specs=pl.BlockSpec((gb, tm, d), lambda bi, i: (bi, i, 0)),
        out_shape=jax.ShapeDtypeStruct((b, l, d), F32),
        compiler_params=_params("parallel", "parallel"),
        name="cross_attn",
    )(y, g_pre.reshape(1, d), w_q, mem_k, mem_v, w_o, g_post.reshape(1, d))


def _layer_tail_kernel(a_ref, y_ref, gains_ref, wout_ref, wq_ref, mk_ref, mv_ref, wo_ref, wg_ref, wu_ref, wd_ref,
                       o_ref):
    gain = lambda r: gains_ref[r:r + 1, :]
    y = y_ref[...] + _rms(_mm(a_ref[...].astype(BF16), wout_ref[...]), gain(0))
    mems = [(mk_ref[...].astype(BF16), mv_ref[...].astype(BF16))]
    y = _cross_attn_rows(y, gain(1), wq_ref, mems, wo_ref, gain(2))
    o_ref[...] = _ffn_rows(y, gain(3), wg_ref, wu_ref, wd_ref, gain(4), bounds=TAIL_FFN_BOUNDS)


def _layer_tail(a, y, gains, w_out, w_q, mem_k, mem_v, layer, w_o, w_gu, w_down):
    b, l, d = y.shape
    k = a.shape[2]
    m = mem_k.shape[2]
    tm = _row_tile(l, TAIL_ROW_TILE)
    rows = lambda width: pl.BlockSpec((None, tm, width), lambda bi, i: (bi, i, 0))
    resident = lambda shape, col=0: pl.BlockSpec(shape, lambda bi, i: (0, col), pipeline_mode=pl.Buffered(1))
    mem = pl.BlockSpec((None, None, m, d), lambda bi, i: (layer, bi, 0, 0))
    return pl.pallas_call(
        _layer_tail_kernel,
        grid=(b, l // tm),
        in_specs=[
            rows(k), rows(d), resident(gains.shape),
            resident((k, d)), resident((d, d)), mem, mem, resident((d, d)),
            resident((d, D_FF), 0), resident((d, D_FF), 1), resident((D_FF, d)),
        ],
        out_specs=rows(d),
        out_shape=jax.ShapeDtypeStruct((b, l, d), F32),
        compiler_params=_params("parallel", "parallel"),
        name="layer_tail",
    )(a, y, gains, w_out, w_q, mem_k, mem_v, w_o, w_gu, w_gu, w_down)


def _gdn_kernel(alog_ref, dtb_ref, q_ref, k_ref, v_ref, z_ref, ba_ref, cb_ref, cw_ref, og_ref, s0_ref,
                o_ref, sn_ref,
                xq_ref, xk_ref, xv_ref, slot0_ref, slot1_ref, sm_ref, sb_ref, oq_ref, ob_ref, gl_ref,
                *, seq, chunk, n_chunks, group, hb):
    c_ = chunk
    padded = n_chunks * c_
    rows = min(seq, c_)
    width = hb * LANES
    heads = [pl.program_id(1) * hb + hh for hh in range(hb)]
    cols = lambda hh: slice(hh * LANES, (hh + 1) * LANES)

    for part, (x_ref, src) in enumerate(((xq_ref, q_ref), (xk_ref, k_ref), (xv_ref, v_ref))):
        x_ref[0:SUBLANES, :] = jnp.zeros((SUBLANES, width), F32)
        x_ref[SUBLANES - (CONV_W - 1):SUBLANES, :] = cb_ref[part]
        x_ref[pl.ds(SUBLANES, seq), :] = src[...]
        if padded > seq:
            x_ref[pl.ds(SUBLANES + seq, padded - seq), :] = jnp.zeros((padded - seq, width), F32)

    alog = [jnp.full((1, 1), alog_ref[h], F32) for h in heads]
    dtb = [jnp.full((1, 1), dtb_ref[h], F32) for h in heads]
    r_i =lax.broadcasted_iota(jnp.int32, (c_, c_), 0)
    c_i = lax.broadcasted_iota(jnp.int32, (c_, c_), 1)
    incl = r_i >= c_i
    strict = r_i > c_i
    tril = jnp.where(incl, 1.0, 0.0).astype(BF16)
    eye = jnp.where(r_i == c_i, 1.0, 0.0).astype(F32)
    blk = lambda size: (r_i // size) == (c_i // size)
    diag_blocks = blk(INV_BLOCK)
    merges = [jnp.logical_and(blk(2 * size), jnp.logical_not(blk(size)))
              for size in (INV_BLOCK << i for i in range(int(math.log2(c_ // INV_BLOCK))))]
    lane = lax.broadcasted_iota(jnp.int32, (c_, LANES), 1)

    def conv(win, part, hh):
        w = cw_ref[part][:, cols(hh)]
        base = SUBLANES - (CONV_W - 1)
        y = win[base:base + c_] * w[0:1]
        for j in range(1, CONV_W):
            y = y + win[base + j:base + j + c_] * w[j:j + 1]
        return _silu(y)

    def pad_rows(x):
        if rows == c_:
            return x
        return jnp.concatenate([x, jnp.zeros((c_ - rows, x.shape[1]), x.dtype)], axis=0)

    def rows_of(c):
        return c * c_ if isinstance(c, int) else pl.multiple_of(c * c_, c_)

    def gates(hh, r0, wins):
        h = heads[hh]
        qc = conv(wins[0], 0, hh)
        kc = conv(wins[1], 1, hh)
        v = conv(wins[2], 2, hh)
        q = qc * lax.rsqrt(jnp.sum(qc * qc, axis=-1, keepdims=True) + RMS_EPS) * (GDN_DK ** -0.5)
        k = kc * lax.rsqrt(jnp.sum(kc * kc, axis=-1, keepdims=True) + RMS_EPS)
        ba = pad_rows(ba_ref[pl.ds(r0, rows), :])
        b_raw = jnp.sum(jnp.where(lane == h, ba, 0.0), axis=-1, keepdims=True)
        a_raw = jnp.sum(jnp.where(lane == h + GDN_HEADS, ba, 0.0), axis=-1, keepdims=True)
        beta = jax.nn.sigmoid(b_raw)
        x = a_raw + dtb[hh]
        g = -jnp.exp(alog[hh]) * (jnp.maximum(x, 0.0) + jnp.log1p(jnp.exp(-jnp.abs(x))))
        if padded > seq:
            valid = (lax.broadcasted_iota(jnp.int32, (c_, 1), 0) + r0) < seq
            k = jnp.where(valid, k, 0.0)
            beta = jnp.where(valid, beta, 0.0)
            g = jnp.where(valid, g, 0.0)
        return q, k, v, jnp.broadcast_to(beta, (c_, LANES)), jnp.broadcast_to(g, (c_, LANES))

    def gates_group(gi, slot_ref):
        for i in range(group):
            r0 = rows_of(gi * group + i)
            for hh in range(hb):
                wins = [ref[pl.ds(r0, c_ + SUBLANES), cols(hh)] for ref in (xq_ref, xk_ref, xv_ref)]
                for n, x in enumerate(gates(hh, r0, wins)):
                    slot_ref[n, i * c_:(i + 1) * c_, cols(hh)] = x

    def prep_group(gi, slot_ref):
        hhs = [hh for hh in range(hb) for _ in range(group)]
        local = [i for _ in range(hb) for i in range(group)]
        cs = [gi * group + i for i in local]
        r0s = [rows_of(c) for c in cs]
        every = lambda f, *xs: [f(*args) for args in zip(*xs)]
        slot = lambda n: [lambda hh=hh, i=i: slot_ref[n, i * c_:(i + 1) * c_, cols(hh)] for hh, i in zip(hhs, local)]
        q, k, v, beta = slot(0), slot(1), slot(2), slot(3)
        g_hi, g_lo = zip(*every(lambda get: _split(get()), slot(4)))
        gc = every(lambda hi, lo: _mm(tril, hi) + _mm(tril, lo), g_hi, g_lo)
        kb = every(lambda get: get().astype(BF16), k)
        kk = every(_mm_nt, kb, kb)
        qk = every(lambda get, y: _mm_nt(get().astype(BF16), y), q, kb)
        decay = every(lambda x: jnp.where(incl, jnp.exp(jnp.where(incl, x[:, :c_] - x.T[:c_], 0.0)), 0.0), gc)
        low = every(lambda b_, d_, kk_: jnp.where(strict, b_()[:, :c_] * d_ * kk_, 0.0), beta, decay, kk)
        pw = every(lambda x: -jnp.where(diag_blocks, x, 0.0), low)
        t_inv = every(lambda x: eye + x, pw)
        for _ in range(int(math.log2(INV_BLOCK)) - 1):
            pw = every(_mmb, pw, pw)
            t_inv = every(lambda t_, p_: t_ + _mmb(t_, p_), t_inv, pw)
        for merge in merges:
            off = every(lambda x, t_: _mmb(jnp.where(merge, x, 0.0), t_), low, t_inv)
            t_inv = every(lambda t_, x: t_ - _mmb(t_, x), t_inv, off)
        egc = every(jnp.exp, gc)
        rhs = every(lambda b_, e_, k_, v_: jnp.concatenate([(b_() * e_) * k_(), b_() * v_()], axis=1),
                    beta, egc, k, v)
        w = every(lambda t_, x: _mmb(t_, x).astype(BF16), t_inv, rhs)
        gc_last = every(lambda x: x[c_ - 1:c_, :], gc)
        kdt = every(lambda k_, x, l_: (k_() * jnp.exp(l_ - x)).T.astype(BF16), k, gc, gc_last)
        a = every(lambda x, d_: (x * d_).astype(BF16), qk, decay)
        kw = every(_mm, kdt, w)
        aw = every(_mm, a, w)
        for i, (hh, c, r0) in enumerate(zip(hhs, cs, r0s)):
            sm_ref[hh * n_chunks + c] = kw[i][:, :GDN_DK].astype(BF16)
            sb_ref[hh * n_chunks + c] = kw[i][:, GDN_DK:]
            oq_ref[pl.ds(r0, c_), cols(hh)] = (q[i]() * egc[i] - aw[i][:, :GDN_DK]).astype(BF16)
            ob_ref[pl.ds(r0, c_), cols(hh)] = aw[i][:, GDN_DK:]
            gl_ref[pl.ds(hh * n_chunks + c, 1), :] = jnp.exp(gc_last[i])

    og = og_ref[...]

    def scan(c, states):
        r0 = rows_of(c)
        at = [hh * n_chunks + c for hh in range(hb)]
        sb = [s.astype(BF16) for s in states]
        upd = [_mm(sm_ref[i], x) for i, x in zip(at, sb)]
        o = [_mm(oq_ref[pl.ds(r0, c_), cols(hh)], sb[hh]) + ob_ref[pl.ds(r0, c_), cols(hh)] for hh in range(hb)]
        for hh in range(hb):
            z = z_ref[pl.ds(r0, rows), cols(hh)]
            o_ref[pl.ds(r0, rows), cols(hh)] = (_rms(o[hh][:rows], og) * _silu(z)).astype(o_ref.dtype)
        return tuple(gl_ref[pl.ds(i, 1), :] * s - u + sb_ref[i] for i, s, u in zip(at, states, upd))

    def pipelined(g2, carry):
        prep_group(2 * g2, slot0_ref)
        gates_group(2 * g2 + 1, slot1_ref)
        prep_group(2 * g2 + 1, slot1_ref)
        gates_group(2 * g2 + 2, slot0_ref)
        return carry

    n_groups = n_chunks // group
    pairs = (n_groups - 1) // 2
    gates_group(0, slot0_ref)
    lax.fori_loop(0, pairs, pipelined, 0)
    prep_group(2 * pairs, slot0_ref)
    if n_groups - 2 * pairs == 2:
        gates_group(2 * pairs + 1, slot1_ref)
        prep_group(2 * pairs + 1, slot1_ref)
    final = lax.fori_loop(0, n_chunks, scan, tuple(s0_ref[hh] for hh in range(hb)),
                          unroll=math.gcd(SCAN_UNROLL, n_chunks))
    for hh in range(hb):
        sn_ref[hh] = final[hh]


def _gdn_core(qkvz, ba, conv_buf, conv_w, a_log, dt_bias, o_gain, s0):
    b, seq, _ = qkvz[0].shape
    chunk = GDN_CHUNK
    n_chunks = -(-seq // chunk)
    padded = n_chunks * chunk
    group = math.gcd(GDN_GROUP // GDN_MIN_HEADS, n_chunks)
    hb = math.gcd(GDN_GROUP // group, GDN_HEADS)
    width = hb * LANES
    kern = functools.partial(_gdn_kernel, seq=seq, chunk=chunk, n_chunks=n_chunks, group=group, hb=hb)
    head_cols = pl.BlockSpec((None, seq, width), lambda bi, hi: (bi, 0, hi))
    smem = pl.BlockSpec(memory_space=pltpu.SMEM)
    return pl.pallas_call(
        kern,
        grid=(b, GDN_HEADS // hb),
        in_specs=[
            smem, smem,
            head_cols, head_cols, head_cols, head_cols,
            pl.BlockSpec((None, seq, LANES), lambda bi, hi: (bi, 0, 0)),
            pl.BlockSpec((None, 3, CONV_W - 1, width), lambda bi, hi: (bi, 0, 0, hi)),
            pl.BlockSpec((3, CONV_W, width), lambda bi, hi: (0, 0, hi)),
            pl.BlockSpec((1, GDN_DV), lambda bi, hi: (0, 0)),
            pl.BlockSpec((None, hb, GDN_DK, GDN_DV), lambda bi, hi: (bi, hi, 0, 0)),
        ],
        out_specs=[
            pl.BlockSpec((None, seq, width), lambda bi, hi: (bi, 0, hi)),
            pl.BlockSpec((None, hb, GDN_DK, GDN_DV), lambda bi, hi: (bi, hi, 0, 0)),
        ],
        out_shape=[
            jax.ShapeDtypeStruct((b, seq, GDN_VW), _mixer_dtype(seq)),
            jax.ShapeDtypeStruct((b, GDN_HEADS, GDN_DK, GDN_DV), F32),
        ],
        scratch_shapes=[
            pltpu.VMEM((padded + SUBLANES, width), F32),
            pltpu.VMEM((padded + SUBLANES, width), F32),
            pltpu.VMEM((padded + SUBLANES, width), F32),
            pltpu.VMEM((5, group * chunk, width), F32),
            pltpu.VMEM((5, group * chunk, width), F32),
            pltpu.VMEM((hb * n_chunks, GDN_DK, GDN_DK), BF16),
            pltpu.VMEM((hb * n_chunks, GDN_DK, GDN_DV), F32),
            pltpu.VMEM((padded, width), BF16),
            pltpu.VMEM((padded, width), F32),
            pltpu.VMEM((hb * n_chunks, LANES), F32),
        ],
        compiler_params=_params("parallel", "parallel"),
        name="gdn_core",
    )(a_log, dt_bias, *qkvz, ba, conv_buf, conv_w, o_gain.reshape(1, GDN_DV), s0)


def _rel_bucket(dist):
    max_exact = REL_BUCKETS // 2
    n = jnp.maximum(dist, 0)
    large = max_exact + (jnp.log(jnp.maximum(n, 1).astype(F32) / max_exact)
                         / math.log(REL_MAX_DIST / max_exact) * (REL_BUCKETS - max_exact)).astype(jnp.int32)
    large = jnp.minimum(large, REL_BUCKETS - 1)
    return jnp.where(n < max_exact, n, large)


def _bias_tiles_kernel(rb_ref, o_ref, *, t):
    h = pl.program_id(0)
    d = pl.program_id(1)
    r = lax.broadcasted_iota(jnp.int32, (t, t), 0)
    c = lax.broadcasted_iota(jnp.int32, (t, t), 1)
    dist = d * t + r - c
    bucket = _rel_bucket(dist)
    bias = jnp.zeros((t, t), F32)
    for b in range(REL_BUCKETS):
        bias = jnp.where(bucket == b, rb_ref[b, h], bias)
    o_ref[...] = jnp.where(dist >= 0, bias - rb_ref[REL_BUCKETS - 1, h], NEG_INF)


def _bias_tiles(rel_bias, t):
    assert t >= REL_MAX_DIST
    return pl.pallas_call(
        functools.partial(_bias_tiles_kernel, t=t),
        grid=(DA_HEADS, 2),
        in_specs=[pl.BlockSpec(memory_space=pltpu.SMEM)],
        out_specs=pl.BlockSpec((None, None, t, t), lambda h, d: (h, d, 0, 0)),
        out_shape=jax.ShapeDtypeStruct((DA_HEADS, 2, t, t), F32),
        compiler_params=_params("parallel", "parallel"),
        name="rel_bias_tiles",
    )(rel_bias)


def _lambda(lq1_ref, lk1_ref, lq2_ref, lk2_ref, lam_init):
    dot = lambda a, b: jnp.sum(a[...] * b[...], axis=-1, keepdims=True)
    return jnp.exp(dot(lq1_ref, lk1_ref)) - jnp.exp(dot(lq2_ref, lk2_ref)) + lam_init


def _da_prompt_kernel(q_ref, k_ref, v_ref, b_ref, lq1_ref, lk1_ref, lq2_ref, lk2_ref, sg_ref, o_ref,
                      kb_ref, vb_ref, s_ref, *, t, lam_init):
    seq = q_ref.shape[0]
    kb_ref[...] = k_ref[...].astype(BF16)
    vb_ref[:, :DA_VD] = v_ref[...].astype(BF16)
    vb_ref[:, DA_VD:] = jnp.ones((seq, LANES), BF16)
    lam = _lambda(lq1_ref, lk1_ref, lq2_ref, lk2_ref, lam_init)
    sg = sg_ref[...]
    lane = lax.broadcasted_iota(jnp.int32, (t, LANES), 1)
    stack = lambda x: jnp.concatenate([x, x], axis=0)
    for qi in range(seq // t):
        q = q_ref[qi * t:(qi + 1) * t, :] * (DA_DH ** -0.5)
        qq = jnp.concatenate([jnp.where(lane < DA_DH, q, 0.0), jnp.where(lane >= DA_DH, q, 0.0)], axis=0)
        n_k = (qi + 1) * t
        s_ref[:, :n_k] = _mm_nt(qq.astype(BF16), kb_ref[:n_k, :])
        s_ref[:, qi * t:n_k] += stack(b_ref[0])
        if qi >= 1:
            s_ref[:, (qi - 1) * t:qi * t] += stack(b_ref[1])
        s = s_ref[:, :n_k]
        p = jnp.exp(s - jnp.max(s, axis=-1, keepdims=True)).astype(BF16)
        acc = _mm(p, vb_ref[:n_k, :])
        a = acc[:, :DA_VD] / acc[:, DA_VD:]
        o = a[:t] - lam * a[t:]
        o_ref[qi * t:(qi + 1) * t, :] = (_rms(o, sg) * (1.0 - lam_init)).astype(o_ref.dtype)


def _da_prompt(qkv, bias_tiles, lq1, lk1, lq2, lk2, sub_gain, lam_init):
    b, seq, _ = qkv[0].shape
    t = bias_tiles.shape[-1]
    vec = lambda x: x.reshape(1, -1)
    vspec = lambda w: pl.BlockSpec((1, w), lambda bi, hi: (0, 0))
    head = pl.BlockSpec((None, seq, LANES), lambda bi, hi: (bi, 0, hi))
    return pl.pallas_call(
        functools.partial(_da_prompt_kernel, t=t, lam_init=lam_init),
        grid=(b, DA_HEADS),
        in_specs=[
            head, head, head,
            pl.BlockSpec((None, 2, t, t), lambda bi, hi: (hi, 0, 0, 0)),
            vspec(DA_DH), vspec(DA_DH), vspec(DA_DH), vspec(DA_DH), vspec(DA_VD),
        ],
        out_specs=pl.BlockSpec((None, seq, LANES), lambda bi, hi: (bi, 0, hi)),
        out_shape=jax.ShapeDtypeStruct((b, seq, DA_HEADS * DA_VD), _mixer_dtype(seq)),
        scratch_shapes=[pltpu.VMEM((seq, DA_DH * 2), BF16), pltpu.VMEM((seq, DA_VD + LANES), BF16),
                        pltpu.VMEM((2 * t, seq), F32)],
        compiler_params=_params("parallel", "parallel"),
        name="da_prompt",
    )(*qkv, bias_tiles, vec(lq1), vec(lk1), vec(lq2), vec(lk2), vec(sub_gain))


def _da_sample_kernel(pt_ref, q_ref, kn_ref, vn_ref, rbt_ref, lq1_ref, lk1_ref, lq2_ref, lk2_ref, sg_ref, *rest,
                      pages, n_pages, lq, lam_init):
    del pt_ref
    k_refs = rest[:pages]
    v_refs = rest[pages:2 * pages]
    o_ref, qbd_ref, m_ref, l_ref, acc_ref, s_ref = rest[2 * pages:]
    step = pl.program_id(1)
    rows = DA_HEADS * 2 * lq
    past_len = n_pages * PAGE_SIZE

    @pl.when(step == 0)
    def _():
        q = q_ref[...] * (DA_DH ** -0.5)
        qrep = jnp.concatenate([q] * (DA_HEADS * 2), axis=0)
        r_hc = lax.broadcasted_iota(jnp.int32, (rows, DA_HW), 0) // lq
        c_hc = lax.broadcasted_iota(jnp.int32, (rows, DA_HW), 1) // DA_DH
        qbd_ref[...] = jnp.where(r_hc == c_hc, qrep, 0.0).astype(BF16)
        m_ref[...] = jnp.full(m_ref.shape, NEG_INF, F32)
        l_ref[...] = jnp.zeros_like(l_ref)
        acc_ref[...] = jnp.zeros_like(acc_ref)

    def near_bias(k_start, n_valid):
        row = lax.broadcasted_iota(jnp.int32, (rows, PAGE_SIZE), 0)
        col = lax.broadcasted_iota(jnp.int32, (rows, PAGE_SIZE), 1)
        dist = past_len + row % lq - (k_start + col)
        bucket = _rel_bucket(dist)
        bias = jnp.zeros((rows, PAGE_SIZE), F32)
        for b in range(REL_BUCKETS):
            bias = jnp.where(bucket == b, rbt_ref[:, b:b + 1], bias)
        bias = bias - rbt_ref[:, REL_BUCKETS - 1:REL_BUCKETS]
        return jnp.where(jnp.logical_and(dist >= 0, col < n_valid), bias, NEG_INF)

    def attend(s_blocks, v_blocks):
        s_max = functools.reduce(jnp.maximum, s_blocks)
        m_prev = m_ref[...]
        m_new = jnp.maximum(m_prev, jnp.max(s_max, axis=-1, keepdims=True))
        alpha = jnp.exp(m_prev - m_new)
        p = [jnp.exp(s - m_new) for s in s_blocks]
        l_ref[...] = alpha * l_ref[...] + jnp.sum(functools.reduce(jnp.add, p), axis=-1, keepdims=True)
        pb = jnp.concatenate([x.astype(BF16) for x in p], axis=1)
        pv = [_mm(pb[h * 2 * lq:(h + 1) * 2 * lq], jnp.concatenate([v[h] for v in v_blocks], axis=0))
              for h in range(DA_HEADS)]
        acc_ref[...] = alpha * acc_ref[...] + jnp.concatenate(pv, axis=0)
        m_ref[...] = m_new

    n_tiles = DA_HW // MXU_TILE
    tile_rows = rows // n_tiles
    q_tiles = [qbd_ref[j * tile_rows:(j + 1) * tile_rows, j * MXU_TILE:(j + 1) * MXU_TILE] for j in range(n_tiles)]
    for r in range(pages):
        kt = k_refs[r][...].astype(BF16)
        s_ref[r] = jnp.concatenate(
            [_mm(q_tiles[j], kt[j * MXU_TILE:(j + 1) * MXU_TILE, :]) for j in range(n_tiles)], axis=0)

    last = step == pl.num_programs(1) - 1

    @pl.when(last)
    def _():
        s_ref[pages - 1] += near_bias(past_len - PAGE_SIZE, PAGE_SIZE)

    attend([s_ref[r] for r in range(pages)],
           [[v_refs[r][pl.ds(h, PAGE_SIZE, stride=DA_HEADS), :].astype(BF16) for h in range(DA_HEADS)]
            for r in range(pages)])

    @pl.when(last)
    def _():
        zeros = jnp.zeros((PAGE_SIZE - lq, DA_HW), F32)
        kn = jnp.concatenate([kn_ref[...], zeros], axis=0).astype(BF16)
        vn = jnp.concatenate([vn_ref[...], zeros], axis=0).astype(BF16)
        attend([_mm_nt(qbd_ref[...], kn) + near_bias(past_len, lq)],
               [[vn[:, h * DA_VD:(h + 1) * DA_VD] for h in range(DA_HEADS)]])
        a = acc_ref[...] / l_ref[...]
        lam = _lambda(lq1_ref, lk1_ref, lq2_ref, lk2_ref, lam_init)
        sg = sg_ref[...]
        heads = []
        for h in range(DA_HEADS):
            o = a[h * 2 * lq:h * 2 * lq + lq] - lam * a[h * 2 * lq + lq:(h + 1) * 2 * lq]
            heads.append(_rms(o, sg) * (1.0 - lam_init))
        o_ref[...] = jnp.concatenate(heads, axis=-1)


def _da_sample(qkv, cache_k, cache_v, layer, page_table, rel_bias, lq1, lk1, lq2, lk2, sub_gain, lam_init):
    b, lq, _ = qkv[0].shape
    assert PAGE_SIZE >= REL_MAX_DIST
    n_pages = page_table.shape[1]
    pages = math.gcd(PAGES_PER_STEP, n_pages)
    rows = DA_HEADS * 2 * lq
    assert rows == LANES and lq == SUBLANES
    rbt = jnp.repeat(rel_bias.T, 2 * lq, axis=0)
    vec = lambda x: x.reshape(1, -1)
    vspec = lambda w: pl.BlockSpec((1, w), lambda bi, si, pt: (0, 0))
    new = pl.BlockSpec((None, lq, DA_HW), lambda bi, si, pt: (bi, 0, 0))
    page = lambda r, shape: pl.BlockSpec((None, None) + shape,
                                         lambda bi, si, pt: (layer, pt[bi, si * pages + r], 0, 0))
    k_shape, v_shape = cache_k.shape[2:], cache_v.shape[2:]
    grid_spec = pltpu.PrefetchScalarGridSpec(
        num_scalar_prefetch=1,
        grid=(b, n_pages // pages),
        in_specs=[new, new, new,
                  pl.BlockSpec((rows, REL_BUCKETS), lambda bi, si, pt: (0, 0)),
                  vspec(DA_DH), vspec(DA_DH), vspec(DA_DH), vspec(DA_DH), vspec(DA_VD)]
                 + [page(r, k_shape) for r in range(pages)] + [page(r, v_shape) for r in range(pages)],
        out_specs=pl.BlockSpec((None, lq, DA_HW), lambda bi, si, pt: (bi, 0, 0)),
        scratch_shapes=[pltpu.VMEM((rows, DA_HW), BF16), pltpu.VMEM((rows, LANES), F32),
                        pltpu.VMEM((rows, LANES), F32), pltpu.VMEM((rows, DA_VD), F32),
                        pltpu.VMEM((pages, rows, PAGE_SIZE), F32)],
    )
    return pl.pallas_call(
        functools.partial(_da_sample_kernel, pages=pages, n_pages=n_pages, lq=lq, lam_init=lam_init),
        grid_spec=grid_spec,
        out_shape=jax.ShapeDtypeStruct((b, lq, DA_HW), F32),
        compiler_params=_params("parallel", "arbitrary"),
        name="da_sample",
    )(page_table, *qkv, rbt, vec(lq1), vec(lk1), vec(lq2), vec(lk2), vec(sub_gain),
      *([cache_k] * pages), *([cache_v] * pages))


def _gdn_layer(y, conv_buf, s0, g_pre, w_main, w_ba, conv_w, a_log, dt_bias, o_gain):
    b, seq, d = y.shape
    yf = y.reshape(b * seq, d)
    *qkvz, ba = _norm_matmul(yf, g_pre, w_main, D_MODEL, w_narrow=w_ba)
    qkvz = [x.reshape(b, seq, D_MODEL) for x in qkvz]
    ba = ba.reshape(b, seq, LANES)
    cb = jnp.transpose(conv_buf.reshape(b, CONV_W - 1, 3, D_MODEL), (0, 2, 1, 3))
    o, s_new = _gdn_core(qkvz, ba, cb, conv_w, a_log, dt_bias, o_gain, s0)
    tail = jnp.concatenate([x[:, -(CONV_W - 1):] for x in qkvz[:3]], axis=-1)
    pre = jnp.concatenate([conv_buf, tail], axis=1)
    new_buf = pre[:, -(CONV_W - 1):]
    return o, new_buf, s_new


def _layer_rest(o, y, g_pre, g_post, w_out, w_q, w_o, w_gu, w_down, mem):
    b, seq, d = y.shape
    mem_k, mem_v, layer = mem
    if mem_k.shape[-1] == d and seq % TAIL_ROW_TILE == 0:
        gains = jnp.stack([g_post[0], g_pre[1], g_post[1], g_pre[2], g_post[2]])
        return _layer_tail(o, y, gains, w_out, w_q, mem_k, mem_v, layer, w_o, w_gu, w_down)
    y = _matmul_postnorm(o.reshape(b * seq, -1), w_out, g_post[0], y.reshape(b * seq, d)).reshape(b, seq, d)
    y = _cross_attn(y, g_pre[1], w_q, mem_k, mem_v, layer, layer, w_o, g_post[1])
    return _ffn(y.reshape(b * seq, d), g_pre[2], w_gu, w_down, g_post[2]).reshape(b, seq, d)


def kernel(x_prompt, x_sample, state_gdn, state_conv, cache_k, cache_v, cache_mem_k, cache_mem_v, page_table,
           mem_prompt, rel_bias, norm_pre, norm_post, gdn_w_in, gdn_conv_w, gdn_a_log, gdn_dt_bias, gdn_o_gain,
           gdn_w_out, da_w_in, da_lq1, da_lk1, da_lq2, da_lk2, da_sub_gain, da_w_out, mem_gain, w_xq, w_xkv, w_xo,
           ffn_w_gu, ffn_w_down):
    depth = norm_pre.shape[0]
    b_p, l_p, d = x_prompt.shape
    b_s, l_s, _ = x_sample.shape
    m_len = mem_prompt.shape[1]
    yp, ys = x_prompt, x_sample
    mem_flat = mem_prompt.reshape(b_p * m_len, d)
    ck = jnp.transpose(cache_k, (0, 1, 3, 4, 5, 2)).reshape(cache_k.shape[:2] + (DA_HW, PAGE_SIZE))
    cv = cache_v.reshape(cache_v.shape[:2] + (PAGE_SIZE * DA_HEADS, DA_VD))
    mem_rows = lambda x: jnp.transpose(
        x.reshape(x.shape[:4] + (X_DH // LANES, LANES)), (0, 1, 2, 4, 3, 5)).reshape(x.shape[:2] + (-1, LANES))
    cmk, cmv = mem_rows(cache_mem_k), mem_rows(cache_mem_v)
    bias_tiles = _bias_tiles(rel_bias, min(ATTN_TILE, l_p))
    mk_rows, mv_rows, mk_tiled, mv_tiled = _memory_kv(mem_flat, mem_gain, w_xkv.astype(BF16))
    mk_rows, mv_rows = (x.reshape(depth, b_p, m_len, d) for x in (mk_rows, mv_rows))
    mem_out = lambda x: jnp.transpose(
        x.reshape(depth, b_p, m_len, X_DH // LANES, X_HEADS, LANES), (0, 1, 2, 4, 3, 5)
    ).reshape(depth, b_p, m_len, X_HEADS, X_DH)
    gdn_p, conv_p, gdn_s, conv_s = [], [], [], []
    k_p, v_p, k_s, v_s = [], [], [], []
    for i in range(depth):
        j = i // N_MIXERS
        g_pre, g_post = norm_pre[i], norm_post[i]
        if i % N_MIXERS == 0:
            w_in = gdn_w_in[j]
            n_main = CONV_DIM + GDN_VW
            w_main = w_in[:, :n_main].astype(BF16)
            w_ba = jnp.pad(w_in[:, n_main:], ((0, 0), (0, LANES - 2 * GDN_HEADS))).astype(BF16)
            conv_w = jnp.transpose(gdn_conv_w[j].reshape(CONV_W, 3, D_MODEL), (1, 0, 2))
            w_out = gdn_w_out[j].astype(BF16)
            gw = (g_pre[0], w_main, w_ba, conv_w, gdn_a_log[j], gdn_dt_bias[j], gdn_o_gain[j])
            buf0 = jnp.zeros((b_p, CONV_W - 1, CONV_DIM), F32)
            s0 = jnp.zeros((b_p, GDN_HEADS, GDN_DK, GDN_DV), F32)
            op, cb, st = _gdn_layer(yp, buf0, s0, *gw)
            gdn_p.append(st)
            conv_p.append(cb)
            os_, cb, st = _gdn_layer(ys, state_conv[j], state_gdn[j], *gw)
            gdn_s.append(st)
            conv_s.append(cb)
        else:
            lam_init = 0.8 - 0.6 * math.exp(-0.3 * i)
            w_in = da_w_in[j].astype(BF16)
            w_out = da_w_out[j].astype(BF16)
            lw = (da_lq1[j], da_lk1[j], da_lq2[j], da_lk2[j], da_sub_gain[j], lam_init)
            qkv = [x.reshape(b_p, l_p, DA_HW) for x in _norm_matmul(yp.reshape(b_p * l_p, d), g_pre[0], w_in, DA_HW)]
            op = _da_prompt(qkv, bias_tiles, *lw)
            k_p.append(qkv[1].reshape(b_p, l_p, DA_HEADS, 2, DA_DH))
            v_p.append(qkv[2].reshape(b_p, l_p, DA_HEADS, DA_VD))
            qkv = [x.reshape(b_s, l_s, DA_HW) for x in _norm_matmul(ys.reshape(b_s * l_s, d), g_pre[0], w_in, DA_HW)]
            os_ = _da_sample(qkv, ck, cv, j, page_table, rel_bias, *lw)
            k_s.append(qkv[1].reshape(b_s, l_s, DA_HEADS, 2, DA_DH))
            v_s.append(qkv[2].reshape(b_s, l_s, DA_HEADS, DA_VD))
        rest = (g_pre, g_post, w_out, w_xq[i].astype(BF16), w_xo[i].astype(BF16),
                ffn_w_gu[i].astype(BF16), ffn_w_down[i].astype(BF16))
        yp = _layer_rest(op, yp, *rest, (mk_rows, mv_rows, i))
        ys = _layer_rest(os_, ys, *rest, (cmk, cmv, i))
    return (yp, ys,
            jnp.stack(gdn_p), jnp.stack(conv_p), jnp.stack(k_p), jnp.stack(v_p),
            mem_out(mk_tiled), mem_out(mv_tiled),
            jnp.stack(gdn_s), jnp.stack(conv_s), jnp.stack(k_s), jnp.stack(v_s))
```

```python
import functools
import math

import jax
import jax.numpy as jnp
from jax import lax
from jax.experimental import pallas as pl
from jax.experimental.pallas import tpu as pltpu

F32 = jnp.float32
BF16 = jnp.bfloat16

D_MODEL = 1024
N_MIXERS = 2
GDN_HEADS = 8
GDN_DK = 128
GDN_DV = 128
CONV_W = 4
GDN_CHUNK = 64
INV_BLOCK = 8
GDN_GROUP = 8
GDN_MIN_HEADS = 2
SCAN_UNROLL = 4
GDN_QK = GDN_HEADS * GDN_DK
GDN_VW = GDN_HEADS * GDN_DV
CONV_DIM = 2 * GDN_QK + GDN_VW
DA_HEADS = 8
DA_DH = 64
DA_VD = 2 * DA_DH
DA_HW = DA_HEADS * 2 * DA_DH
PAGE_SIZE = 128
REL_BUCKETS = 32
REL_MAX_DIST = 128
X_HEADS = 4
X_DH = D_MODEL // X_HEADS
D_FF = -(-8 * D_MODEL // (3 * 256)) * 256
RMS_EPS = 1e-6
NEG_INF = -1e30

LANES = 128
SUBLANES = 8
ROW_TILE = 512
TAIL_ROW_TILE = 512
MXU_TILE = 256
TAIL_FFN_BOUNDS = (0, -(-D_FF // (2 * MXU_TILE)) * MXU_TILE, D_FF)
ATTN_TILE = 256
PAGES_PER_STEP = 16
XATTN_ROWS = 64
VMEM_LIMIT = 56 * 1024 * 1024

_NT = (((1,), (1,)), ((), ()))


def _params(*sem):
    return pltpu.CompilerParams(dimension_semantics=sem, vmem_limit_bytes=VMEM_LIMIT)


def _row_tile(t, rows=ROW_TILE):
    return rows if t % rows == 0 else t


def _rms(x, g):
    return x * lax.rsqrt(jnp.mean(x * x, axis=-1, keepdims=True) + RMS_EPS) * g


def _silu(x):
    return x * jax.nn.sigmoid(x)


def _mm(a, b):
    return jnp.dot(a, b, preferred_element_type=F32)


def _mm_nt(a, b):
    return lax.dot_general(a, b, _NT, preferred_element_type=F32)


def _split(x):
    hi = x.astype(BF16)
    return hi, (x - hi.astype(F32)).astype(BF16)


def _mmb(a, b):
    return _mm(a.astype(BF16), b.astype(BF16))


def _norm_matmul_kernel(x_ref, g_ref, *rest, n_w):
    w_refs, o_refs = rest[:n_w], rest[n_w:]
    h = _rms(x_ref[...], g_ref[...]).astype(BF16)
    tn = o_refs[0].shape[1]
    for j, o_ref in enumerate(o_refs[:-1] if n_w == 2 else o_refs):
        o_ref[...] = _mm(h, w_refs[0][:, j * tn:(j + 1) * tn])
    if n_w == 2:
        o_refs[-1][...] = _mm(h, w_refs[1][...])


def _norm_matmul(x, g, w, tn, w_narrow=None):
    t, d = x.shape
    n = w.shape[1]
    tm = _row_tile(t)
    resident = lambda a: pl.BlockSpec(a.shape, lambda i: (0, 0), pipeline_mode=pl.Buffered(1))
    weights = [w] if w_narrow is None else [w, w_narrow]
    widths = [tn] * (n // tn) + ([] if w_narrow is None else [LANES])
    return pl.pallas_call(
        functools.partial(_norm_matmul_kernel, n_w=len(weights)),
        grid=(t // tm,),
        in_specs=[pl.BlockSpec((tm, d), lambda i: (i, 0)), pl.BlockSpec((1, d), lambda i: (0, 0))]
                 + [resident(a) for a in weights],
        out_specs=[pl.BlockSpec((tm, width), lambda i: (i, 0)) for width in widths],
        out_shape=[jax.ShapeDtypeStruct((t, width), F32) for width in widths],
        compiler_params=_params("parallel"),
        name="norm_matmul",
    )(x, g.reshape(1, d), *weights)


def _memory_kv_kernel(x_ref, g_ref, w_ref, k_ref, v_ref, kt_ref, vt_ref):
    h = _rms(x_ref[...], g_ref[...]).astype(BF16)
    d = k_ref.shape[1]
    tiles = X_DH // LANES
    for col, rows_ref, tiled_ref in ((0, k_ref, kt_ref), (d, v_ref, vt_ref)):
        y = _mm(h, w_ref[:, col:col + d])
        rows_ref[...] = y
        for hd in range(X_HEADS):
            for lt in range(tiles):
                at = (hd * tiles + lt) * LANES
                tiled_ref[pl.ds(lt * X_HEADS + hd, y.shape[0], stride=tiles * X_HEADS), :] = y[:, at:at + LANES]


def _memory_kv(mem, gains, w_kv):
    t, d = mem.shape
    n = gains.shape[0]
    tm = _row_tile(t)
    rows_per = d // LANES
    plain = pl.BlockSpec((None, tm, d), lambda li, i: (li, i, 0))
    tiled = pl.BlockSpec((None, tm * rows_per, LANES), lambda li, i: (li, i, 0))
    return pl.pallas_call(
        _memory_kv_kernel,
        grid=(n, t // tm),
        in_specs=[
            pl.BlockSpec((tm, d), lambda li, i: (i, 0)),
            pl.BlockSpec((None, 1, d), lambda li, i: (li, 0, 0)),
            pl.BlockSpec((None, d, 2 * d), lambda li, i: (li, 0, 0)),
        ],
        out_specs=[plain, plain, tiled, tiled],
        out_shape=[jax.ShapeDtypeStruct((n, t, d), F32)] * 2
                  + [jax.ShapeDtypeStruct((n, t * rows_per, LANES), F32)] * 2,
        compiler_params=_params("parallel", "parallel"),
        name="memory_kv",
    )(mem, gains.reshape(n, 1, d), w_kv)


def _matmul_postnorm_kernel(a_ref, w_ref, g_ref, r_ref, o_ref):
    y = _mm(a_ref[...].astype(BF16), w_ref[...])
    o_ref[...] = r_ref[...] + _rms(y, g_ref[...])


def _matmul_postnorm(a, w, g, res):
    t, k = a.shape
    d = w.shape[1]
    tm = _row_tile(t)
    return pl.pallas_call(
        _matmul_postnorm_kernel,
        grid=(t // tm,),
        in_specs=[
            pl.BlockSpec((tm, k), lambda i: (i, 0)),
            pl.BlockSpec((k, d), lambda i: (0, 0)),
            pl.BlockSpec((1, d), lambda i: (0, 0)),
            pl.BlockSpec((tm, d), lambda i: (i, 0)),
        ],
        out_specs=pl.BlockSpec((tm, d), lambda i: (i, 0)),
        out_shape=jax.ShapeDtypeStruct((t, d), F32),
        compiler_params=_params("parallel"),
        name="matmul_postnorm",
    )(a, w, g.reshape(1, d), res)


def _ffn_rows(x, g_pre, wg_ref, wu_ref, wd_ref, g_post, bounds=(0, D_FF)):
    h = _rms(x, g_pre).astype(BF16)
    y = None
    for lo, hi in zip(bounds[:-1], bounds[1:]):
        cols = slice(lo, hi)
        a = (_silu(_mm(h, wg_ref[:, cols])) * _mm(h, wu_ref[:, cols])).astype(BF16)
        part = _mm(a, wd_ref[cols, :])
        y = part if y is None else y + part
    return x + _rms(y, g_post)


def _ffn_kernel(x_ref, gpre_ref, wg_ref, wu_ref, wd_ref, gpost_ref, o_ref):
    o_ref[...] = _ffn_rows(x_ref[...], gpre_ref[...], wg_ref, wu_ref, wd_ref, gpost_ref[...])


def _ffn(x, g_pre, w_gu, w_down, g_post):
    t, d = x.shape
    tm = _row_tile(t)
    resident = lambda shape, col: pl.BlockSpec(shape, lambda i: (0, col), pipeline_mode=pl.Buffered(1))
    return pl.pallas_call(
        _ffn_kernel,
        grid=(t // tm,),
        in_specs=[
            pl.BlockSpec((tm, d), lambda i: (i, 0)),
            pl.BlockSpec((1, d), lambda i: (0, 0)),
            resident((d, D_FF), 0),
            resident((d, D_FF), 1),
            resident((D_FF, d), 0),
            pl.BlockSpec((1, d), lambda i: (0, 0)),
        ],
        out_specs=pl.BlockSpec((tm, d), lambda i: (i, 0)),
        out_shape=jax.ShapeDtypeStruct((t, d), F32),
        compiler_params=_params("parallel"),
        name="ffn",
    )(x, g_pre.reshape(1, d), w_gu, w_gu, w_down, g_post.reshape(1, d))


def _mem_attention(q, mk, mv):
    heads = []
    for hd in range(X_HEADS):
        sl = slice(hd * X_DH, (hd + 1) * X_DH)
        s = _mm_nt(q[:, sl], mk[:, sl])
        p = jnp.exp(s - jnp.max(s, axis=-1, keepdims=True))
        heads.append(_mm(p.astype(BF16), mv[:, sl]) / jnp.sum(p, axis=-1, keepdims=True))
    return jnp.concatenate(heads, axis=-1)


def _cross_attn_rows(y, g_pre, wq_ref, mems, wo_ref, g_post):
    h = _rms(y, g_pre).astype(BF16)
    q = (_mm(h, wq_ref[...]) * (X_DH ** -0.5)).astype(BF16)
    rows = y.shape[0] // len(mems)
    o = [_mem_attention(q[e * rows:(e + 1) * rows], mk, mv) for e, (mk, mv) in enumerate(mems)]
    o = (o[0] if len(o) == 1 else jnp.concatenate(o, axis=0)).astype(BF16)
    return y + _rms(_mm(o, wo_ref[...]), g_post)


def _cross_attn_kernel(y_ref, gpre_ref, wq_ref, mk_ref, mv_ref, wo_ref, gpost_ref, o_ref, *, tiled_mem):
    gb, tm, d = y_ref.shape
    if tiled_mem:
        tiles = X_DH // LANES
        m = mk_ref.shape[1] // (tiles * X_HEADS)
        load = lambda ref, e: jnp.concatenate(
            [ref[e, pl.ds(lt * X_HEADS + hd, m, stride=tiles * X_HEADS), :]
             for hd in range(X_HEADS) for lt in range(tiles)], axis=1).astype(BF16)
    else:
        load = lambda ref, e: ref[e].astype(BF16)
    mems = [(load(mk_ref, e), load(mv_ref, e)) for e in range(gb)]
    y = _cross_attn_rows(y_ref[...].reshape(gb * tm, d), gpre_ref[...], wq_ref, mems, wo_ref, gpost_ref[...])
    o_ref[...] = y.reshape(gb, tm, d)


def _cross_attn(y, g_pre, w_q, mem_k, mem_v, layer_k, layer_v, w_o, g_post):
    b, l, d = y.shape
    mem_block = mem_k.shape[2:]
    tm = _row_tile(l)
    gb = math.gcd(b, max(1, XATTN_ROWS // tm))
    resident = pl.BlockSpec((d, d), lambda bi, i: (0, 0), pipeline_mode=pl.Buffered(1))
    return pl.pallas_call(
        functools.partial(_cross_attn_kernel, tiled_mem=mem_block[1] != d),
        grid=(b // gb, l // tm),
        in_specs=[
            pl.BlockSpec((gb, tm, d), lambda bi, i: (bi, i, 0)),
            pl.BlockSpec((1, d), lambda bi, i: (0, 0)),
            resident,
            pl.BlockSpec((None, gb) + mem_block, lambda bi, i: (layer_k, bi, 0, 0)),
            pl.BlockSpec((None, gb) + mem_block, lambda bi, i: (layer_v, bi, 0, 0)),
            resident,
            pl.BlockSpec((1, d), lambda bi, i: (0, 0)),
        ],
        out_specs=pl.BlockSpec((gb, tm, d), lambda bi, i: (bi, i, 0)),
        out_shape=jax.ShapeDtypeStruct((b, l, d), F32),
        compiler_params=_params("parallel", "parallel"),
        name="cross_attn",
    )(y, g_pre.reshape(1, d), w_q, mem_k, mem_v, w_o, g_post.reshape(1, d))


def _layer_tail_kernel(a_ref, y_ref, gains_ref, wout_ref, wq_ref, mk_ref, mv_ref, wo_ref, wg_ref, wu_ref, wd_ref,
                       o_ref):
    gain = lambda r: gains_ref[r:r + 1, :]
    y = y_ref[...] + _rms(_mm(a_ref[...].astype(BF16), wout_ref[...]), gain(0))
    mems = [(mk_ref[...].astype(BF16), mv_ref[...].astype(BF16))]
    y = _cross_attn_rows(y, gain(1), wq_ref, mems, wo_ref, gain(2))
    o_ref[...] = _ffn_rows(y, gain(3), wg_ref, wu_ref, wd_ref, gain(4), bounds=TAIL_FFN_BOUNDS)


def _layer_tail(a, y, gains, w_out, w_q, mem_k, mem_v, layer, w_o, w_gu, w_down):
    b, l, d = y.shape
    k = a.shape[2]
    m = mem_k.shape[2]
    tm = _row_tile(l, TAIL_ROW_TILE)
    rows = lambda width: pl.BlockSpec((None, tm, width), lambda bi, i: (bi, i, 0))
    resident = lambda shape, col=0: pl.BlockSpec(shape, lambda bi, i: (0, col), pipeline_mode=pl.Buffered(1))
    mem = pl.BlockSpec((None, None, m, d), lambda bi, i: (layer, bi, 0, 0))
    return pl.pallas_call(
        _layer_tail_kernel,
        grid=(b, l // tm),
        in_specs=[
            rows(k), rows(d), resident(gains.shape),
            resident((k, d)), resident((d, d)), mem, mem, resident((d, d)),
            resident((d, D_FF), 0), resident((d, D_FF), 1), resident((D_FF, d)),
        ],
        out_specs=rows(d),
        out_shape=jax.ShapeDtypeStruct((b, l, d), F32),
        compiler_params=_params("parallel", "parallel"),
        name="layer_tail",
    )(a, y, gains, w_out, w_q, mem_k, mem_v, w_o, w_gu, w_gu, w_down)


def _gdn_kernel(alog_ref, dtb_ref, q_ref, k_ref, v_ref, z_ref, ba_ref, cb_ref, cw_ref, og_ref, s0_ref,
                o_ref, sn_ref,
                xq_ref, xk_ref, xv_ref, slot0_ref, slot1_ref, sm_ref, sb_ref, oq_ref, ob_ref, gl_ref,
                *, seq, chunk, n_chunks, group, hb):
    c_ = chunk
    padded = n_chunks * c_
    rows = min(seq, c_)
    width = hb * LANES
    heads = [pl.program_id(1) * hb + hh for hh in range(hb)]
    cols = lambda hh: slice(hh * LANES, (hh + 1) * LANES)

    for part, (x_ref, src) in enumerate(((xq_ref, q_ref), (xk_ref, k_ref), (xv_ref, v_ref))):
        x_ref[0:SUBLANES, :] = jnp.zeros((SUBLANES, width), F32)
        x_ref[SUBLANES - (CONV_W - 1):SUBLANES, :] = cb_ref[part]
        x_ref[pl.ds(SUBLANES, seq), :] = src[...]
        if padded > seq:
            x_ref[pl.ds(SUBLANES + seq, padded - seq), :] = jnp.zeros((padded - seq, width), F32)

    alog = [jnp.full((1, 1), alog_ref[h], F32) for h in heads]
    dtb = [jnp.full((1, 1), dtb_ref[h], F32) for h in heads]
    r_i =lax.broadcasted_iota(jnp.int32, (c_, c_), 0)
    c_i = lax.broadcasted_iota(jnp.int32, (c_, c_), 1)
    incl = r_i >= c_i
    strict = r_i > c_i
    tril = jnp.where(incl, 1.0, 0.0).astype(BF16)
    eye = jnp.where(r_i == c_i, 1.0, 0.0).astype(F32)
    blk = lambda size: (r_i // size) == (c_i // size)
    diag_blocks = blk(INV_BLOCK)
    merges = [jnp.logical_and(blk(2 * size), jnp.logical_not(blk(size)))
              for size in (INV_BLOCK << i for i in range(int(math.log2(c_ // INV_BLOCK))))]
    lane = lax.broadcasted_iota(jnp.int32, (c_, LANES), 1)

    def conv(win, part, hh):
        w = cw_ref[part][:, cols(hh)]
        base = SUBLANES - (CONV_W - 1)
        y = win[base:base + c_] * w[0:1]
        for j in range(1, CONV_W):
            y = y + win[base + j:base + j + c_] * w[j:j + 1]
        return _silu(y)

    def pad_rows(x):
        if rows == c_:
            return x
        return jnp.concatenate([x, jnp.zeros((c_ - rows, x.shape[1]), x.dtype)], axis=0)

    def rows_of(c):
        return c * c_ if isinstance(c, int) else pl.multiple_of(c * c_, c_)

    def gates(hh, r0, wins):
        h = heads[hh]
        qc = conv(wins[0], 0, hh)
        kc = conv(wins[1], 1, hh)
        v = conv(wins[2], 2, hh)
        q = qc * lax.rsqrt(jnp.sum(qc * qc, axis=-1, keepdims=True) + RMS_EPS) * (GDN_DK ** -0.5)
        k = kc * lax.rsqrt(jnp.sum(kc * kc, axis=-1, keepdims=True) + RMS_EPS)
        ba = pad_rows(ba_ref[pl.ds(r0, rows), :])
        b_raw = jnp.sum(jnp.where(lane == h, ba, 0.0), axis=-1, keepdims=True)
        a_raw = jnp.sum(jnp.where(lane == h + GDN_HEADS, ba, 0.0), axis=-1, keepdims=True)
        beta = jax.nn.sigmoid(b_raw)
        x = a_raw + dtb[hh]
        g = -jnp.exp(alog[hh]) * (jnp.maximum(x, 0.0) + jnp.log1p(jnp.exp(-jnp.abs(x))))
        if padded > seq:
            valid = (lax.broadcasted_iota(jnp.int32, (c_, 1), 0) + r0) < seq
            k = jnp.where(valid, k, 0.0)
            beta = jnp.where(valid, beta, 0.0)
            g = jnp.where(valid, g, 0.0)
        return q, k, v, jnp.broadcast_to(beta, (c_, LANES)), jnp.broadcast_to(g, (c_, LANES))

    def gates_group(gi, slot_ref):
        for i in range(group):
            r0 = rows_of(gi * group + i)
            for hh in range(hb):
                wins = [ref[pl.ds(r0, c_ + SUBLANES), cols(hh)] for ref in (xq_ref, xk_ref, xv_ref)]
                for n, x in enumerate(gates(hh, r0, wins)):
                    slot_ref[n, i * c_:(i + 1) * c_, cols(hh)] = x

    def prep_group(gi, slot_ref):
        hhs = [hh for hh in range(hb) for _ in range(group)]
        local = [i for _ in range(hb) for i in range(group)]
        cs = [gi * group + i for i in local]
        r0s = [rows_of(c) for c in cs]
        every = lambda f, *xs: [f(*args) for args in zip(*xs)]
        slot = lambda n: [lambda hh=hh, i=i: slot_ref[n, i * c_:(i + 1) * c_, cols(hh)] for hh, i in zip(hhs, local)]
        q, k, v, beta = slot(0), slot(1), slot(2), slot(3)
        g_hi, g_lo = zip(*every(lambda get: _split(get()), slot(4)))
        gc = every(lambda hi, lo: _mm(tril, hi) + _mm(tril, lo), g_hi, g_lo)
        kb = every(lambda get: get().astype(BF16), k)
        kk = every(_mm_nt, kb, kb)
        qk = every(lambda get, y: _mm_nt(get().astype(BF16), y), q, kb)
        decay = every(lambda x: jnp.where(incl, jnp.exp(jnp.where(incl, x[:, :c_] - x.T[:c_], 0.0)), 0.0), gc)
        low = every(lambda b_, d_, kk_: jnp.where(strict, b_()[:, :c_] * d_ * kk_, 0.0), beta, decay, kk)
        pw = every(lambda x: -jnp.where(diag_blocks, x, 0.0), low)
        t_inv = every(lambda x: eye + x, pw)
        for _ in range(int(math.log2(INV_BLOCK)) - 1):
            pw = every(_mmb, pw, pw)
            t_inv = every(lambda t_, p_: t_ + _mmb(t_, p_), t_inv, pw)
        for merge in merges:
            off = every(lambda x, t_: _mmb(jnp.where(merge, x, 0.0), t_), low, t_inv)
            t_inv = every(lambda t_, x: t_ - _mmb(t_, x), t_inv, off)
        egc = every(jnp.exp, gc)
        rhs = every(lambda b_, e_, k_, v_: jnp.concatenate([(b_() * e_) * k_(), b_() * v_()], axis=1),
                    beta, egc, k, v)
        w = every(lambda t_, x: _mmb(t_, x).astype(BF16), t_inv, rhs)
        gc_last = every(lambda x: x[c_ - 1:c_, :], gc)
        kdt = every(lambda k_, x, l_: (k_() * jnp.exp(l_ - x)).T.astype(BF16), k, gc, gc_last)
        a = every(lambda x, d_: (x * d_).astype(BF16), qk, decay)
        kw = every(_mm, kdt, w)
        aw = every(_mm, a, w)
        for i, (hh, c, r0) in enumerate(zip(hhs, cs, r0s)):
            sm_ref[hh * n_chunks + c] = kw[i][:, :GDN_DK].astype(BF16)
            sb_ref[hh * n_chunks + c] = kw[i][:, GDN_DK:]
            oq_ref[pl.ds(r0, c_), cols(hh)] = (q[i]() * egc[i] - aw[i][:, :GDN_DK]).astype(BF16)
            ob_ref[pl.ds(r0, c_), cols(hh)] = aw[i][:, GDN_DK:]
            gl_ref[pl.ds(hh * n_chunks + c, 1), :] = jnp.exp(gc_last[i])

    og = og_ref[...]

    def scan(c, states):
        r0 = rows_of(c)
        at = [hh * n_chunks + c for hh in range(hb)]
        sb = [s.astype(BF16) for s in states]
        upd = [_mm(sm_ref[i], x) for i, x in zip(at, sb)]
        o = [_mm(oq_ref[pl.ds(r0, c_), cols(hh)], sb[hh]) + ob_ref[pl.ds(r0, c_), cols(hh)] for hh in range(hb)]
        for hh in range(hb):
            z = z_ref[pl.ds(r0, rows), cols(hh)]
            o_ref[pl.ds(r0, rows), cols(hh)] = _rms(o[hh][:rows], og) * _silu(z)
        return tuple(gl_ref[pl.ds(i, 1), :] * s - u + sb_ref[i] for i, s, u in zip(at, states, upd))

    def pipelined(g2, carry):
        prep_group(2 * g2, slot0_ref)
        gates_group(2 * g2 + 1, slot1_ref)
        prep_group(2 * g2 + 1, slot1_ref)
        gates_group(2 * g2 + 2, slot0_ref)
        return carry

    n_groups = n_chunks // group
    pairs = (n_groups - 1) // 2
    gates_group(0, slot0_ref)
    lax.fori_loop(0, pairs, pipelined, 0)
    prep_group(2 * pairs, slot0_ref)
    if n_groups - 2 * pairs == 2:
        gates_group(2 * pairs + 1, slot1_ref)
        prep_group(2 * pairs + 1, slot1_ref)
    final = lax.fori_loop(0, n_chunks, scan, tuple(s0_ref[hh] for hh in range(hb)),
                          unroll=math.gcd(SCAN_UNROLL, n_chunks))
    for hh in range(hb):
        sn_ref[hh] = final[hh]


def _gdn_core(qkvz, ba, conv_buf, conv_w, a_log, dt_bias, o_gain, s0):
    b, seq, _ = qkvz[0].shape
    chunk = GDN_CHUNK
    n_chunks = -(-seq // chunk)
    padded = n_chunks * chunk
    group = math.gcd(GDN_GROUP // GDN_MIN_HEADS, n_chunks)
    hb = math.gcd(GDN_GROUP // group, GDN_HEADS)
    width = hb * LANES
    kern = functools.partial(_gdn_kernel, seq=seq, chunk=chunk, n_chunks=n_chunks, group=group, hb=hb)
    head_cols = pl.BlockSpec((None, seq, width), lambda bi, hi: (bi, 0, hi))
    smem = pl.BlockSpec(memory_space=pltpu.SMEM)
    return pl.pallas_call(
        kern,
        grid=(b, GDN_HEADS // hb),
        in_specs=[
            smem, smem,
            head_cols, head_cols, head_cols, head_cols,
            pl.BlockSpec((None, seq, LANES), lambda bi, hi: (bi, 0, 0)),
            pl.BlockSpec((None, 3, CONV_W - 1, width), lambda bi, hi: (bi, 0, 0, hi)),
            pl.BlockSpec((3, CONV_W, width), lambda bi, hi: (0, 0, hi)),
            pl.BlockSpec((1, GDN_DV), lambda bi, hi: (0, 0)),
            pl.BlockSpec((None, hb, GDN_DK, GDN_DV), lambda bi, hi: (bi, hi, 0, 0)),
        ],
        out_specs=[
            pl.BlockSpec((None, seq, width), lambda bi, hi: (bi, 0, hi)),
            pl.BlockSpec((None, hb, GDN_DK, GDN_DV), lambda bi, hi: (bi, hi, 0, 0)),
        ],
        out_shape=[
            jax.ShapeDtypeStruct((b, seq, GDN_VW), F32),
            jax.ShapeDtypeStruct((b, GDN_HEADS, GDN_DK, GDN_DV), F32),
        ],
        scratch_shapes=[
            pltpu.VMEM((padded + SUBLANES, width), F32),
            pltpu.VMEM((padded + SUBLANES, width), F32),
            pltpu.VMEM((padded + SUBLANES, width), F32),
            pltpu.VMEM((5, group * chunk, width), F32),
            pltpu.VMEM((5, group * chunk, width), F32),
            pltpu.VMEM((hb * n_chunks, GDN_DK, GDN_DK), BF16),
            pltpu.VMEM((hb * n_chunks, GDN_DK, GDN_DV), F32),
            pltpu.VMEM((padded, width), BF16),
            pltpu.VMEM((padded, width), F32),
            pltpu.VMEM((hb * n_chunks, LANES), F32),
        ],
        compiler_params=_params("parallel", "parallel"),
        name="gdn_core",
    )(a_log, dt_bias, *qkvz, ba, conv_buf, conv_w, o_gain.reshape(1, GDN_DV), s0)


def _rel_bucket(dist):
    max_exact = REL_BUCKETS // 2
    n = jnp.maximum(dist, 0)
    large = max_exact + (jnp.log(jnp.maximum(n, 1).astype(F32) / max_exact)
                         / math.log(REL_MAX_DIST / max_exact) * (REL_BUCKETS - max_exact)).astype(jnp.int32)
    large = jnp.minimum(large, REL_BUCKETS - 1)
    return jnp.where(n < max_exact, n, large)


def _bias_tiles_kernel(rb_ref, o_ref, *, t):
    h = pl.program_id(0)
    d = pl.program_id(1)
    r = lax.broadcasted_iota(jnp.int32, (t, t), 0)
    c = lax.broadcasted_iota(jnp.int32, (t, t), 1)
    dist = d * t + r - c
    bucket = _rel_bucket(dist)
    bias = jnp.zeros((t, t), F32)
    for b in range(REL_BUCKETS):
        bias = jnp.where(bucket == b, rb_ref[b, h], bias)
    o_ref[...] = jnp.where(dist >= 0, bias - rb_ref[REL_BUCKETS - 1, h], NEG_INF)


def _bias_tiles(rel_bias, t):
    assert t >= REL_MAX_DIST
    return pl.pallas_call(
        functools.partial(_bias_tiles_kernel, t=t),
        grid=(DA_HEADS, 2),
        in_specs=[pl.BlockSpec(memory_space=pltpu.SMEM)],
        out_specs=pl.BlockSpec((None, None, t, t), lambda h, d: (h, d, 0, 0)),
        out_shape=jax.ShapeDtypeStruct((DA_HEADS, 2, t, t), F32),
        compiler_params=_params("parallel", "parallel"),
        name="rel_bias_tiles",
    )(rel_bias)


def _lambda(lq1_ref, lk1_ref, lq2_ref, lk2_ref, lam_init):
    dot = lambda a, b: jnp.sum(a[...] * b[...], axis=-1, keepdims=True)
    return jnp.exp(dot(lq1_ref, lk1_ref)) - jnp.exp(dot(lq2_ref, lk2_ref)) + lam_init


def _da_prompt_kernel(q_ref, k_ref, v_ref, b_ref, lq1_ref, lk1_ref, lq2_ref, lk2_ref, sg_ref, o_ref,
                      kb_ref, vb_ref, s_ref, *, t, lam_init):
    seq = q_ref.shape[0]
    kb_ref[...] = k_ref[...].astype(BF16)
    vb_ref[:, :DA_VD] = v_ref[...].astype(BF16)
    vb_ref[:, DA_VD:] = jnp.ones((seq, LANES), BF16)
    lam = _lambda(lq1_ref, lk1_ref, lq2_ref, lk2_ref, lam_init)
    sg = sg_ref[...]
    lane = lax.broadcasted_iota(jnp.int32, (t, LANES), 1)
    stack = lambda x: jnp.concatenate([x, x], axis=0)
    for qi in range(seq // t):
        q = q_ref[qi * t:(qi + 1) * t, :] * (DA_DH ** -0.5)
        qq = jnp.concatenate([jnp.where(lane < DA_DH, q, 0.0), jnp.where(lane >= DA_DH, q, 0.0)], axis=0)
        n_k = (qi + 1) * t
        s_ref[:, :n_k] = _mm_nt(qq.astype(BF16), kb_ref[:n_k, :])
        s_ref[:, qi * t:n_k] += stack(b_ref[0])
        if qi >= 1:
            s_ref[:, (qi - 1) * t:qi * t] += stack(b_ref[1])
        halves = []
        for br in range(2):
            s = s_ref[br * t:(br + 1) * t, :n_k]
            p = jnp.exp(s - jnp.max(s, axis=-1, keepdims=True)).astype(BF16)
            acc = _mm(p, vb_ref[:n_k, :])
            halves.append(acc[:, :DA_VD] / acc[:, DA_VD:])
        o = halves[0] - lam * halves[1]
        o_ref[qi * t:(qi + 1) * t, :] = _rms(o, sg) * (1.0 - lam_init)


def _da_prompt(qkv, bias_tiles, lq1, lk1, lq2, lk2, sub_gain, lam_init):
    b, seq, _ = qkv[0].shape
    t = bias_tiles.shape[-1]
    vec = lambda x: x.reshape(1, -1)
    vspec = lambda w: pl.BlockSpec((1, w), lambda bi, hi: (0, 0))
    head = pl.BlockSpec((None, seq, LANES), lambda bi, hi: (bi, 0, hi))
    return pl.pallas_call(
        functools.partial(_da_prompt_kernel, t=t, lam_init=lam_init),
        grid=(b, DA_HEADS),
        in_specs=[
            head, head, head,
            pl.BlockSpec((None, 2, t, t), lambda bi, hi: (hi, 0, 0, 0)),
            vspec(DA_DH), vspec(DA_DH), vspec(DA_DH), vspec(DA_DH), vspec(DA_VD),
        ],
        out_specs=pl.BlockSpec((None, seq, LANES), lambda bi, hi: (bi, 0, hi)),
        out_shape=jax.ShapeDtypeStruct((b, seq, DA_HEADS * DA_VD), F32),
        scratch_shapes=[pltpu.VMEM((seq, DA_DH * 2), BF16), pltpu.VMEM((seq, DA_VD + LANES), BF16),
                        pltpu.VMEM((2 * t, seq), F32)],
        compiler_params=_params("parallel", "parallel"),
        name="da_prompt",
    )(*qkv, bias_tiles, vec(lq1), vec(lk1), vec(lq2), vec(lk2), vec(sub_gain))


def _da_sample_kernel(pt_ref, q_ref, kn_ref, vn_ref, rbt_ref, lq1_ref, lk1_ref, lq2_ref, lk2_ref, sg_ref, *rest,
                      pages, n_pages, lq, lam_init):
    del pt_ref
    k_refs = rest[:pages]
    v_refs = rest[pages:2 * pages]
    o_ref, qbd_ref, m_ref, l_ref, acc_ref, s_ref = rest[2 * pages:]
    step = pl.program_id(1)
    rows = DA_HEADS * 2 * lq
    past_len = n_pages * PAGE_SIZE

    @pl.when(step == 0)
    def _():
        q = q_ref[...] * (DA_DH ** -0.5)
        qrep = jnp.concatenate([q] * (DA_HEADS * 2), axis=0)
        r_hc = lax.broadcasted_iota(jnp.int32, (rows, DA_HW), 0) // lq
        c_hc = lax.broadcasted_iota(jnp.int32, (rows, DA_HW), 1) // DA_DH
        qbd_ref[...] = jnp.where(r_hc == c_hc, qrep, 0.0).astype(BF16)
        m_ref[...] = jnp.full(m_ref.shape, NEG_INF, F32)
        l_ref[...] = jnp.zeros_like(l_ref)
        acc_ref[...] = jnp.zeros_like(acc_ref)

    def near_bias(k_start, n_valid):
        row = lax.broadcasted_iota(jnp.int32, (rows, PAGE_SIZE), 0)
        col = lax.broadcasted_iota(jnp.int32, (rows, PAGE_SIZE), 1)
        dist = past_len + row % lq - (k_start + col)
        bucket = _rel_bucket(dist)
        bias = jnp.zeros((rows, PAGE_SIZE), F32)
        for b in range(REL_BUCKETS):
            bias = jnp.where(bucket == b, rbt_ref[:, b:b + 1], bias)
        bias = bias - rbt_ref[:, REL_BUCKETS - 1:REL_BUCKETS]
        return jnp.where(jnp.logical_and(dist >= 0, col < n_valid), bias, NEG_INF)

    def attend(s_blocks, v_blocks):
        s_max = functools.reduce(jnp.maximum, s_blocks)
        m_prev = m_ref[...]
        m_new = jnp.maximum(m_prev, jnp.max(s_max, axis=-1, keepdims=True))
        alpha = jnp.exp(m_prev - m_new)
        p = [jnp.exp(s - m_new) for s in s_blocks]
        l_ref[...] = alpha * l_ref[...] + jnp.sum(functools.reduce(jnp.add, p), axis=-1, keepdims=True)
        pb = jnp.concatenate([x.astype(BF16) for x in p], axis=1)
        pv = [_mm(pb[h * 2 * lq:(h + 1) * 2 * lq], jnp.concatenate([v[h] for v in v_blocks], axis=0))
              for h in range(DA_HEADS)]
        acc_ref[...] = alpha * acc_ref[...] + jnp.concatenate(pv, axis=0)
        m_ref[...] = m_new

    n_tiles = DA_HW // MXU_TILE
    tile_rows = rows // n_tiles
    q_tiles = [qbd_ref[j * tile_rows:(j + 1) * tile_rows, j * MXU_TILE:(j + 1) * MXU_TILE] for j in range(n_tiles)]
    for r in range(pages):
        kt = k_refs[r][...].astype(BF16)
        s_ref[r] = jnp.concatenate(
            [_mm(q_tiles[j], kt[j * MXU_TILE:(j + 1) * MXU_TILE, :]) for j in range(n_tiles)], axis=0)

    last = step == pl.num_programs(1) - 1

    @pl.when(last)
    def _():
        s_ref[pages - 1] += near_bias(past_len - PAGE_SIZE, PAGE_SIZE)

    attend([s_ref[r] for r in range(pages)],
           [[v_refs[r][pl.ds(h, PAGE_SIZE, stride=DA_HEADS), :].astype(BF16) for h in range(DA_HEADS)]
            for r in range(pages)])

    @pl.when(last)
    def _():
        zeros = jnp.zeros((PAGE_SIZE - lq, DA_HW), F32)
        kn = jnp.concatenate([kn_ref[...], zeros], axis=0).astype(BF16)
        vn = jnp.concatenate([vn_ref[...], zeros], axis=0).astype(BF16)
        attend([_mm_nt(qbd_ref[...], kn) + near_bias(past_len, lq)],
               [[vn[:, h * DA_VD:(h + 1) * DA_VD] for h in range(DA_HEADS)]])
        a = acc_ref[...] / l_ref[...]
        lam = _lambda(lq1_ref, lk1_ref, lq2_ref, lk2_ref, lam_init)
        sg = sg_ref[...]
        heads = []
        for h in range(DA_HEADS):
            o = a[h * 2 * lq:h * 2 * lq + lq] - lam * a[h * 2 * lq + lq:(h + 1) * 2 * lq]
            heads.append(_rms(o, sg) * (1.0 - lam_init))
        o_ref[...] = jnp.concatenate(heads, axis=-1)


def _da_sample(qkv, cache_k, cache_v, layer, page_table, rel_bias, lq1, lk1, lq2, lk2, sub_gain, lam_init):
    b, lq, _ = qkv[0].shape
    assert PAGE_SIZE >= REL_MAX_DIST
    n_pages = page_table.shape[1]
    pages = math.gcd(PAGES_PER_STEP, n_pages)
    rows = DA_HEADS * 2 * lq
    assert rows == LANES and lq == SUBLANES
    rbt = jnp.repeat(rel_bias.T, 2 * lq, axis=0)
    vec = lambda x: x.reshape(1, -1)
    vspec = lambda w: pl.BlockSpec((1, w), lambda bi, si, pt: (0, 0))
    new = pl.BlockSpec((None, lq, DA_HW), lambda bi, si, pt: (bi, 0, 0))
    page = lambda r, shape: pl.BlockSpec((None, None) + shape,
                                         lambda bi, si, pt: (layer, pt[bi, si * pages + r], 0, 0))
    k_shape, v_shape = cache_k.shape[2:], cache_v.shape[2:]
    grid_spec = pltpu.PrefetchScalarGridSpec(
        num_scalar_prefetch=1,
        grid=(b, n_pages // pages),
        in_specs=[new, new, new,
                  pl.BlockSpec((rows, REL_BUCKETS), lambda bi, si, pt: (0, 0)),
                  vspec(DA_DH), vspec(DA_DH), vspec(DA_DH), vspec(DA_DH), vspec(DA_VD)]
                 + [page(r, k_shape) for r in range(pages)] + [page(r, v_shape) for r in range(pages)],
        out_specs=pl.BlockSpec((None, lq, DA_HW), lambda bi, si, pt: (bi, 0, 0)),
        scratch_shapes=[pltpu.VMEM((rows, DA_HW), BF16), pltpu.VMEM((rows, LANES), F32),
                        pltpu.VMEM((rows, LANES), F32), pltpu.VMEM((rows, DA_VD), F32),
                        pltpu.VMEM((pages, rows, PAGE_SIZE), F32)],
    )
    return pl.pallas_call(
        functools.partial(_da_sample_kernel, pages=pages, n_pages=n_pages, lq=lq, lam_init=lam_init),
        grid_spec=grid_spec,
        out_shape=jax.ShapeDtypeStruct((b, lq, DA_HW), F32),
        compiler_params=_params("parallel", "arbitrary"),
        name="da_sample",
    )(page_table, *qkv, rbt, vec(lq1), vec(lk1), vec(lq2), vec(lk2), vec(sub_gain),
      *([cache_k] * pages), *([cache_v] * pages))


def _gdn_layer(y, conv_buf, s0, g_pre, w_main, w_ba, conv_w, a_log, dt_bias, o_gain):
    b, seq, d = y.shape
    yf = y.reshape(b * seq, d)
    *qkvz, ba = _norm_matmul(yf, g_pre, w_main, D_MODEL, w_narrow=w_ba)
    qkvz = [x.reshape(b, seq, D_MODEL) for x in qkvz]
    ba = ba.reshape(b, seq, LANES)
    cb = jnp.transpose(conv_buf.reshape(b, CONV_W - 1, 3, D_MODEL), (0, 2, 1, 3))
    o, s_new = _gdn_core(qkvz, ba, cb, conv_w, a_log, dt_bias, o_gain, s0)
    tail = jnp.concatenate([x[:, -(CONV_W - 1):] for x in qkvz[:3]], axis=-1)
    pre = jnp.concatenate([conv_buf, tail], axis=1)
    new_buf = pre[:, -(CONV_W - 1):]
    return o, new_buf, s_new


def _layer_rest(o, y, g_pre, g_post, w_out, w_q, w_o, w_gu, w_down, mem):
    b, seq, d = y.shape
    mem_k, mem_v, layer = mem
    if mem_k.shape[-1] == d and seq % TAIL_ROW_TILE == 0:
        gains = jnp.stack([g_post[0], g_pre[1], g_post[1], g_pre[2], g_post[2]])
        return _layer_tail(o, y, gains, w_out, w_q, mem_k, mem_v, layer, w_o, w_gu, w_down)
    y = _matmul_postnorm(o.reshape(b * seq, -1), w_out, g_post[0], y.reshape(b * seq, d)).reshape(b, seq, d)
    y = _cross_attn(y, g_pre[1], w_q, mem_k, mem_v, layer, layer, w_o, g_post[1])
    return _ffn(y.reshape(b * seq, d), g_pre[2], w_gu, w_down, g_post[2]).reshape(b, seq, d)


def kernel(x_prompt, x_sample, state_gdn, state_conv, cache_k, cache_v, cache_mem_k, cache_mem_v, page_table,
           mem_prompt, rel_bias, norm_pre, norm_post, gdn_w_in, gdn_conv_w, gdn_a_log, gdn_dt_bias, gdn_o_gain,
           gdn_w_out, da_w_in, da_lq1, da_lk1, da_lq2, da_lk2, da_sub_gain, da_w_out, mem_gain, w_xq, w_xkv, w_xo,
           ffn_w_gu, ffn_w_down):
    depth = norm_pre.shape[0]
    b_p, l_p, d = x_prompt.shape
    b_s, l_s, _ = x_sample.shape
    m_len = mem_prompt.shape[1]
    yp, ys = x_prompt, x_sample
    mem_flat = mem_prompt.reshape(b_p * m_len, d)
    ck = jnp.transpose(cache_k, (0, 1, 3, 4, 5, 2)).reshape(cache_k.shape[:2] + (DA_HW, PAGE_SIZE))
    cv = cache_v.reshape(cache_v.shape[:2] + (PAGE_SIZE * DA_HEADS, DA_VD))
    mem_rows = lambda x: jnp.transpose(
        x.reshape(x.shape[:4] + (X_DH // LANES, LANES)), (0, 1, 2, 4, 3, 5)).reshape(x.shape[:2] + (-1, LANES))
    cmk, cmv = mem_rows(cache_mem_k), mem_rows(cache_mem_v)
    bias_tiles = _bias_tiles(rel_bias, min(ATTN_TILE, l_p))
    mk_rows, mv_rows, mk_tiled, mv_tiled = _memory_kv(mem_flat, mem_gain, w_xkv.astype(BF16))
    mk_rows, mv_rows = (x.reshape(depth, b_p, m_len, d) for x in (mk_rows, mv_rows))
    mem_out = lambda x: jnp.transpose(
        x.reshape(depth, b_p, m_len, X_DH // LANES, X_HEADS, LANES), (0, 1, 2, 4, 3, 5)
    ).reshape(depth, b_p, m_len, X_HEADS, X_DH)
    gdn_p, conv_p, gdn_s, conv_s = [], [], [], []
    k_p, v_p, k_s, v_s = [], [], [], []
    for i in range(depth):
        j = i // N_MIXERS
        g_pre, g_post = norm_pre[i], norm_post[i]
        if i % N_MIXERS == 0:
            w_in = gdn_w_in[j]
            n_main = CONV_DIM + GDN_VW
            w_main = w_in[:, :n_main].astype(BF16)
            w_ba = jnp.pad(w_in[:, n_main:], ((0, 0), (0, LANES - 2 * GDN_HEADS))).astype(BF16)
            conv_w = jnp.transpose(gdn_conv_w[j].reshape(CONV_W, 3, D_MODEL), (1, 0, 2))
            w_out = gdn_w_out[j].astype(BF16)
            gw = (g_pre[0], w_main, w_ba, conv_w, gdn_a_log[j], gdn_dt_bias[j], gdn_o_gain[j])
            buf0 = jnp.zeros((b_p, CONV_W - 1, CONV_DIM), F32)
            s0 = jnp.zeros((b_p, GDN_HEADS, GDN_DK, GDN_DV), F32)
            op, cb, st = _gdn_layer(yp, buf0, s0, *gw)
            gdn_p.append(st)
            conv_p.append(cb)
            os_, cb, st = _gdn_layer(ys, state_conv[j], state_gdn[j], *gw)
            gdn_s.append(st)
            conv_s.append(cb)
        else:
            lam_init = 0.8 - 0.6 * math.exp(-0.3 * i)
            w_in = da_w_in[j].astype(BF16)
            w_out = da_w_out[j].astype(BF16)
            lw = (da_lq1[j], da_lk1[j], da_lq2[j], da_lk2[j], da_sub_gain[j], lam_init)
            qkv = [x.reshape(b_p, l_p, DA_HW) for x in _norm_matmul(yp.reshape(b_p * l_p, d), g_pre[0], w_in, DA_HW)]
            op = _da_prompt(qkv, bias_tiles, *lw)
            k_p.append(qkv[1].reshape(b_p, l_p, DA_HEADS, 2, DA_DH))
            v_p.append(qkv[2].reshape(b_p, l_p, DA_HEADS, DA_VD))
            qkv = [x.reshape(b_s, l_s, DA_HW) for x in _norm_matmul(ys.reshape(b_s * l_s, d), g_pre[0], w_in, DA_HW)]
            os_ = _da_sample(qkv, ck, cv, j, page_table, rel_bias, *lw)
            k_s.append(qkv[1].reshape(b_s, l_s, DA_HEADS, 2, DA_DH))
            v_s.append(qkv[2].reshape(b_s, l_s, DA_HEADS, DA_VD))
        rest = (g_pre, g_post, w_out, w_xq[i].astype(BF16), w_xo[i].astype(BF16),
                ffn_w_gu[i].astype(BF16), ffn_w_down[i].astype(BF16))
        yp = _layer_rest(op, yp, *rest, (mk_rows, mv_rows, i))
        ys = _layer_rest(os_, ys, *rest, (cmk, cmv, i))
    return (yp, ys,
            jnp.stack(gdn_p), jnp.stack(conv_p), jnp.stack(k_p), jnp.stack(v_p),
            mem_out(mk_tiled), mem_out(mv_tiled),
            jnp.stack(gdn_s), jnp.stack(conv_s), jnp.stack(k_s), jnp.stack(v_s))
```
